```python
import math
import jax, jax.numpy as jnp
from jax import lax
import numpy as np

D_MODEL = 1024
BATCH = 8
SEQ = 4096
DEPTH = 1

N_HEADS = 8
HEAD_DIM = 64
ATTN_WIDTH = N_HEADS * HEAD_DIM
MOBA_BLOCK = 256
MOBA_TOPK = 3
Q_BLOCK = 128
POOL_WINDOWS = (2, 4, 8, 16)
N_POOL_GROUPS = 4
POOL_WIDTH = D_MODEL // 2
POOL_GROUP = POOL_WIDTH // N_POOL_GROUPS
N_BUCKETS = 32
MAX_DISTANCE = 128
D_FF = 2816
CONV_WIDTH = 3
N_BRANCHES = 2
ALPHA = (2.0 * DEPTH) ** 0.25
BETA = (8.0 * DEPTH) ** -0.25
LN_EPS = 1e-5
IN_COLS = 3 * ATTN_WIDTH + POOL_WIDTH + N_BRANCHES * D_MODEL
NEG = -1e30

kernel_name = "hybrid_moba_pool_gated_deepnorm"


def layer_norm(x, g, b):
    xf = x.astype(jnp.float32)
    mu = xf.mean(-1, keepdims=True)
    var = jnp.square(xf - mu).mean(-1, keepdims=True)
    return ((xf - mu) * lax.rsqrt(var + LN_EPS) * g.astype(jnp.float32) + b.astype(jnp.float32)).astype(x.dtype)


def rel_bucket(dist):
    max_exact = N_BUCKETS // 2
    n = jnp.maximum(dist, 0)
    nf = jnp.maximum(n, 1).astype(jnp.float32)
    large = max_exact + (jnp.log(nf / max_exact) / math.log(MAX_DISTANCE / max_exact)
                         * (N_BUCKETS - max_exact)).astype(jnp.int32)
    large = jnp.minimum(large, N_BUCKETS - 1)
    return jnp.where(n < max_exact, n, large)


def moba_attention(q, k, v, rel_bias):
    B, S, H, Dh = q.shape
    S_pad = -(-S // MOBA_BLOCK) * MOBA_BLOCK
    pad = ((0, 0), (0, S_pad - S), (0, 0), (0, 0))
    q, k, v = [jnp.pad(a, pad).transpose(0, 2, 1, 3) for a in (q, k, v)]
    NB = S_pad // MOBA_BLOCK
    NC = S_pad // Q_BLOCK
    kb = k.reshape(B, H, NB, MOBA_BLOCK, Dh)
    vb = v.reshape(B, H, NB, MOBA_BLOCK, Dh)
    scale = 1.0 / math.sqrt(Dh)

    kmean = kb.astype(jnp.float32).mean(axis=3)
    gate = jnp.einsum('bhsd,bhnd->bhsn', q.astype(jnp.float32), kmean)
    q_blk = jnp.arange(S_pad) // MOBA_BLOCK
    past = jnp.arange(NB)[None, :] < q_blk[:, None]
    gate = jnp.where(past[None, None], gate, -jnp.inf)
    k_sel = min(MOBA_TOPK, NB)
    _, sel = lax.top_k(gate, k_sel)
    valid = sel < q_blk[None, None, :, None]

    def to_chunks(a):
        tail = a.shape[3:]
        a = a.reshape((B, H, NC, Q_BLOCK) + tail)
        a = jnp.moveaxis(a, 2, 1)
        return a.reshape((B * NC, H, Q_BLOCK) + tail)

    q_c, sel_c, valid_c = to_chunks(q), to_chunks(sel), to_chunks(valid)
    b_ids = jnp.repeat(jnp.arange(B, dtype=jnp.int32), NC)
    c_ids = jnp.tile(jnp.arange(NC, dtype=jnp.int32), B)
    h_idx3 = jnp.arange(H)[:, None, None]
    h_idx4 = jnp.arange(H)[:, None, None, None]
    offs = jnp.arange(MOBA_BLOCK)

    def chunk_fn(args):
        qc, selc, validc, b, c = args
        kb_b = kb[b]
        vb_b = vb[b]
        k_g = kb_b[h_idx3, selc]
        v_g = vb_b[h_idx3, selc]
        own = (c * Q_BLOCK) // MOBA_BLOCK
        k_own = lax.dynamic_index_in_dim(kb_b, own, axis=1, keepdims=False)
        v_own = lax.dynamic_index_in_dim(vb_b, own, axis=1, keepdims=False)
        q_pos = c * Q_BLOCK + jnp.arange(Q_BLOCK)

        s_sel = jnp.einsum('hqd,hqjkd->hqjk', qc, k_g).astype(jnp.float32) * scale
        k_pos_sel = selc[..., None] * MOBA_BLOCK + offs
        bias_sel = rel_bias[h_idx4, rel_bucket(q_pos[None, :, None, None] - k_pos_sel)]
        s_sel = jnp.where(validc[..., None], s_sel + bias_sel.astype(jnp.float32), NEG)

        s_own = jnp.einsum('hqd,hkd->hqk', qc, k_own).astype(jnp.float32) * scale
        d_own = q_pos[:, None] - (own * MOBA_BLOCK + offs)[None, :]
        bias_own = rel_bias[:, rel_bucket(d_own)].astype(jnp.float32)
        s_own = jnp.where((d_own >= 0)[None], s_own + bias_own, NEG)

        logits = jnp.concatenate([s_sel.reshape(H, Q_BLOCK, k_sel * MOBA_BLOCK), s_own], axis=-1)
        p = jax.nn.softmax(logits, axis=-1).astype(v.dtype)
        p_sel = p[..., :k_sel * MOBA_BLOCK].reshape(H, Q_BLOCK, k_sel, MOBA_BLOCK)
        p_own = p[..., k_sel * MOBA_BLOCK:]
        return (jnp.einsum('hqjk,hqjkd->hqd', p_sel, v_g)
                + jnp.einsum('hqk,hkd->hqd', p_own, v_own))

    out = lax.map(chunk_fn, (q_c, sel_c, valid_c, b_ids, c_ids))
    out = out.reshape(B, NC, H, Q_BLOCK, Dh).transpose(0, 1, 3, 2, 4)
    return out.reshape(B, S_pad, H * Dh)[:, :S]


def multiscale_pool(p, w_group, scale):
    B, S, _ = p.shape
    pf = p.astype(jnp.float32)
    cs = jnp.pad(jnp.cumsum(pf, axis=1), ((0, 0), (1, 0), (0, 0)))
    t = jnp.arange(S)
    outs = []
    for g, w in enumerate(POOL_WINDOWS):
        csg = cs[..., g * POOL_GROUP:(g + 1) * POOL_GROUP]
        lag = jnp.pad(csg, ((0, 0), (w - 1, 0), (0, 0)))[:, :S]
        cnt = jnp.minimum(t + 1, w).astype(jnp.float32)
        outs.append((csg[:, 1:] - lag) / cnt[None, :, None])
    pooled = jnp.stack(outs, axis=2)
    diff = (pooled - pf.reshape(B, S, N_POOL_GROUPS, POOL_GROUP)).astype(p.dtype)
    y = jnp.einsum('bsgc,gcd->bsgd', diff, w_group).reshape(B, S, POOL_WIDTH)
    return y * scale


def causal_dwconv(a, w, b):
    S = a.shape[1]
    ap = jnp.pad(a, ((0, 0), (CONV_WIDTH - 1, 0), (0, 0)))
    y = b
    for i in range(CONV_WIDTH):
        y = y + ap[:, i:i + S] * w[i]
    return y


def setup_inputs(seed: int = 0) -> dict:
    key = jax.random.key(seed)
    ks = jax.random.split(key, 20)
    f32 = jnp.float32
    nrm = lambda k, shape, s: jax.random.normal(k, shape, f32) * s
    x = jax.random.normal(ks[0], (BATCH, SEQ, D_MODEL), f32)
    w_in = nrm(ks[1], (DEPTH, D_MODEL, IN_COLS), D_MODEL ** -0.5)
    col_scale = jnp.ones((IN_COLS,), f32).at[2 * ATTN_WIDTH:3 * ATTN_WIDTH].set(BETA)
    w_in = w_in * col_scale
    rel_bias = nrm(ks[2], (N_HEADS, N_BUCKETS), 0.5)
    w_pool_group = nrm(ks[3], (DEPTH, N_POOL_GROUPS, POOL_GROUP, POOL_GROUP), POOL_GROUP ** -0.5)
    pool_scale = 1.0 + nrm(ks[4], (DEPTH, POOL_WIDTH), 0.02)
    w_branch_attn = nrm(ks[5], (DEPTH, ATTN_WIDTH, D_MODEL), ATTN_WIDTH ** -0.5 * BETA)
    w_branch_pool = nrm(ks[6], (DEPTH, POOL_WIDTH, D_MODEL), POOL_WIDTH ** -0.5 * BETA)
    w_out = nrm(ks[7], (DEPTH, D_MODEL, D_MODEL), D_MODEL ** -0.5 * BETA)
    ln1_g = 1.0 + nrm(ks[8], (DEPTH, D_MODEL), 0.02)
    ln1_b = nrm(ks[9], (DEPTH, D_MODEL), 0.02)
    w_ffn_in = nrm(ks[10], (DEPTH, D_MODEL, 2 * D_FF), D_MODEL ** -0.5)
    conv_w = nrm(ks[11], (DEPTH, CONV_WIDTH, D_FF), CONV_WIDTH ** -0.5)
    conv_b = nrm(ks[12], (DEPTH, D_FF), 0.01)
    w_ffn_out = nrm(ks[13], (DEPTH, D_FF, D_MODEL), D_FF ** -0.5 * BETA)
    ln2_g = 1.0 + nrm(ks[14], (DEPTH, D_MODEL), 0.02)
    ln2_b = nrm(ks[15], (DEPTH, D_MODEL), 0.02)
    return {"x": x, "w_in": w_in, "rel_bias": rel_bias, "w_pool_group": w_pool_group,
            "pool_scale": pool_scale, "w_branch_attn": w_branch_attn, "w_branch_pool": w_branch_pool,
            "w_out": w_out, "ln1_g": ln1_g, "ln1_b": ln1_b, "w_ffn_in": w_ffn_in,
            "conv_w": conv_w, "conv_b": conv_b, "w_ffn_out": w_ffn_out,
            "ln2_g": ln2_g, "ln2_b": ln2_b}


def reference(x, w_in, rel_bias, w_pool_group, pool_scale, w_branch_attn, w_branch_pool,
              w_out, ln1_g, ln1_b, w_ffn_in, conv_w, conv_b, w_ffn_out, ln2_g, ln2_b):
    B, S, D = x.shape
    h = x
    for l in range(DEPTH):
        proj = h @ w_in[l]
        q, k, v, p, g = jnp.split(proj, [ATTN_WIDTH, 2 * ATTN_WIDTH, 3 * ATTN_WIDTH,
                                         3 * ATTN_WIDTH + POOL_WIDTH], axis=-1)
        hs = (B, S, N_HEADS, HEAD_DIM)
        y_attn = moba_attention(q.reshape(hs), k.reshape(hs), v.reshape(hs), rel_bias) @ w_branch_attn[l]
        y_pool = multiscale_pool(p, w_pool_group[l], pool_scale[l]) @ w_branch_pool[l]
        gates = jax.nn.sigmoid(g.astype(jnp.float32)).astype(h.dtype).reshape(B, S, N_BRANCHES, D)
        mixed = gates[:, :, 0] * y_attn + gates[:, :, 1] * y_pool
        h = layer_norm(ALPHA * h + mixed @ w_out[l], ln1_g[l], ln1_b[l])
        a, u = jnp.split(h @ w_ffn_in[l], 2, axis=-1)
        a = causal_dwconv(a, conv_w[l], conv_b[l])
        f = (jax.nn.gelu(a, approximate=False) * u) @ w_ffn_out[l]
        h = layer_norm(ALPHA * h + f, ln2_g[l], ln2_b[l])
    return h
```

```python
import functools
import math

import jax
import jax.numpy as jnp
from jax import lax
from jax.experimental import pallas as pl
from jax.experimental.pallas import tpu as pltpu

N_HEADS = 8
HEAD_DIM = 64
ATTN_WIDTH = N_HEADS * HEAD_DIM
MOBA_BLOCK = 256
MOBA_TOPK = 3
POOL_WINDOWS = (2, 4, 8, 16)
POOL_GROUP = 128
POOL_WIDTH = POOL_GROUP * len(POOL_WINDOWS)
N_BUCKETS = 32
MAX_DISTANCE = 128
CONV_WIDTH = 3
LN_EPS = 1e-5
NEG = -1e30

LANES = 128
K_PAD = LANES
POOL_HALO = max(POOL_WINDOWS)
CONV_HALO = 8
FF_CHUNK = 256
TOKEN_TILE = 512
VMEM_LIMIT = 56 * 1024 * 1024

F32 = jnp.float32
BF16 = jnp.bfloat16
NT_DIMS = (((1,), (1,)), ((), ()))
TN_DIMS = (((0,), (0,)), ((), ()))


def _params(n_grid):
    return pltpu.CompilerParams(dimension_semantics=("arbitrary",) * n_grid,
                                vmem_limit_bytes=VMEM_LIMIT)


def _const_spec(shape):
    nd = len(shape)
    return pl.BlockSpec(shape, lambda *_: (0,) * nd)


def _rel_bias_kernel(rb_ref, own_ref, prev_ref):
    h = pl.program_id(0)
    kk = lax.broadcasted_iota(jnp.int32, (MOBA_BLOCK, MOBA_BLOCK), 0)
    qq = lax.broadcasted_iota(jnp.int32, (MOBA_BLOCK, MOBA_BLOCK), 1)
    max_exact = N_BUCKETS // 2

    def table(dist):
        n = jnp.maximum(dist, 0)
        nf = jnp.maximum(n, 1).astype(F32)
        large = max_exact + (jnp.log(nf / max_exact) / math.log(MAX_DISTANCE / max_exact)
                             * (N_BUCKETS - max_exact)).astype(jnp.int32)
        large = jnp.minimum(large, N_BUCKETS - 1)
        bucket = jnp.where(n < max_exact, n, large)
        out = jnp.zeros(dist.shape, F32)
        for b in range(N_BUCKETS):
            out = jnp.where(bucket == b, rb_ref[h, b], out)
        return out

    own_ref[...] = table(qq - kk)
    prev_ref[...] = table(qq - kk + MOBA_BLOCK)


def _rel_bias_tables(rel_bias):
    shape = jax.ShapeDtypeStruct((N_HEADS, MOBA_BLOCK, MOBA_BLOCK), F32)
    spec = pl.BlockSpec((None, MOBA_BLOCK, MOBA_BLOCK), lambda h: (h, 0, 0))
    return pl.pallas_call(
        _rel_bias_kernel,
        grid=(N_HEADS,),
        in_specs=[pl.BlockSpec(memory_space=pltpu.SMEM)],
        out_specs=[spec, spec],
        out_shape=[shape, shape],
        compiler_params=_params(1),
        name="rel_bias_tables",
    )(rel_bias)


def _qkv_kernel(x_ref, wq_ref, wk_ref, wv_ref, qt_ref, k_ref, vt_ref):
    xb = x_ref[...].astype(BF16)
    qt = lax.dot_general(wq_ref[...], xb, NT_DIMS, preferred_element_type=F32)
    vt = lax.dot_general(wv_ref[...], xb, NT_DIMS, preferred_element_type=F32)
    k_ref[...] = jnp.dot(xb, wk_ref[...], preferred_element_type=F32).astype(BF16)
    for c in range(qt_ref.shape[0]):
        cols = slice(c * MOBA_BLOCK, (c + 1) * MOBA_BLOCK)
        qt_ref[c] = qt[:, cols].astype(BF16)
        vt_ref[c] = vt[:, cols].astype(BF16)


def _qkv_proj(x, wq_t, wk_pad, wv_t):
    b, s, d = x.shape
    nb = s // MOBA_BLOCK
    bpt = TOKEN_TILE // MOBA_BLOCK
    t_shape = jax.ShapeDtypeStruct((b, nb, ATTN_WIDTH, MOBA_BLOCK), BF16)
    t_spec = pl.BlockSpec((None, bpt, ATTN_WIDTH, MOBA_BLOCK), lambda i, t: (i, t, 0, 0))
    return pl.pallas_call(
        _qkv_kernel,
        grid=(b, s // TOKEN_TILE),
        in_specs=[pl.BlockSpec((None, TOKEN_TILE, d), lambda i, t: (i, t, 0)),
                  _const_spec(wq_t.shape), _const_spec(wk_pad.shape), _const_spec(wv_t.shape)],
        out_specs=[t_spec,
                   pl.BlockSpec((None, TOKEN_TILE, N_HEADS * K_PAD), lambda i, t: (i, t, 0)),
                   t_spec],
        out_shape=[t_shape, jax.ShapeDtypeStruct((b, s, N_HEADS * K_PAD), BF16), t_shape],
        compiler_params=_params(2),
        name="qkv_proj",
    )(x, wq_t, wk_pad, wv_t)


def _attn_kernel(rb_ref, qt_ref, k_ref, vt_ref, bown_ref, bprev_ref, o_ref,
                 kmean_ref, pen_ref, m_ref, l_ref, acc_ref):
    h = pl.program_id(1)
    i = pl.program_id(2)
    nb = vt_ref.shape[0]
    blk = MOBA_BLOCK

    @pl.when(i == 0)
    def _():
        for j in range(nb):
            kj = k_ref[j * blk:(j + 1) * blk, :].astype(F32)
            kmean_ref[j:j + 1, :] = jnp.sum(kj, axis=0, keepdims=True) * (1.0 / blk)

    qa = jnp.concatenate([qt_ref[...], jnp.zeros((K_PAD - HEAD_DIM, blk), BF16)], axis=0)

    km = kmean_ref[...]
    km_hi = km.astype(BF16)
    rem = km - km_hi.astype(F32)
    km_mid = rem.astype(BF16)
    km_lo = (rem - km_mid.astype(F32)).astype(BF16)
    g3 = jnp.dot(jnp.concatenate([km_hi, km_mid, km_lo], axis=0), qa, preferred_element_type=F32)
    gate = g3[0:nb] + g3[nb:2 * nb] + g3[2 * nb:3 * nb]

    bidx = lax.broadcasted_iota(jnp.int32, (nb, blk), 0)
    rank = jnp.zeros((nb, blk), jnp.int32)
    for jp in range(nb):
        row = gate[jp:jp + 1, :]
        beats = jnp.where(row > gate, 1, jnp.where(row == gate, jnp.where(bidx > jp, 1, 0), 0))
        rank = rank + jnp.where(i > jp, beats, 0)
    selected = jnp.where(bidx < i, jnp.where(rank < MOBA_TOPK, 1, 0), 0)
    far_bias = jnp.where(bidx < i - 1, rb_ref[h, N_BUCKETS - 1], 0.0)
    pen_ref[...] = jnp.where(selected == 1, far_bias, NEG)

    def scores(j):
        start = pl.multiple_of(j * blk, blk)
        return jnp.dot(k_ref[pl.ds(start, blk), :], qa, preferred_element_type=F32)

    kk = lax.broadcasted_iota(jnp.int32, (blk, blk), 0)
    qq = lax.broadcasted_iota(jnp.int32, (blk, blk), 1)
    s = jnp.where(qq >= kk, scores(i) + bown_ref[...], NEG)
    m0 = jnp.max(s, axis=0, keepdims=True)
    p = jnp.exp(s - m0)
    m_ref[...] = m0
    l_ref[...] = jnp.sum(p, axis=0, keepdims=True)
    acc_ref[...] = jnp.dot(vt_ref[i], p.astype(BF16), preferred_element_type=F32)

    def update(j, s):
        m_old = m_ref[...]
        m_new = jnp.maximum(m_old, jnp.max(s, axis=0, keepdims=True))
        alpha = jnp.exp(m_old - m_new)
        p = jnp.exp(s - m_new)
        l_ref[...] = alpha * l_ref[...] + jnp.sum(p, axis=0, keepdims=True)
        acc_ref[...] = alpha * acc_ref[...] + jnp.dot(vt_ref[j], p.astype(BF16),
                                                      preferred_element_type=F32)
        m_ref[...] = m_new

    @pl.when(i >= 1)
    def _():
        j = i - 1
        update(j, scores(j) + bprev_ref[...] + pen_ref[pl.ds(j, 1), :])

    def far_body(j, carry):
        update(j, scores(j) + pen_ref[pl.ds(j, 1), :])
        return carry

    lax.fori_loop(0, i - 1, far_body, 0)

    o_ref[...] = (acc_ref[...] / l_ref[...]).astype(o_ref.dtype)


def _moba_attention(rel_bias, qt, k, vt, bias_own, bias_prev):
    b, nb, _, blk = qt.shape
    s = nb * blk
    head_tile = pl.BlockSpec((None, None, HEAD_DIM, blk), lambda bi, h, i: (bi, i, h, 0))
    bias_spec = pl.BlockSpec((None, blk, blk), lambda bi, h, i: (h, 0, 0))
    return pl.pallas_call(
        _attn_kernel,
        grid=(b, N_HEADS, nb),
        in_specs=[pl.BlockSpec(memory_space=pltpu.SMEM),
                  head_tile,
                  pl.BlockSpec((None, s, K_PAD), lambda bi, h, i: (bi, 0, h)),
                  pl.BlockSpec((None, nb, HEAD_DIM, blk), lambda bi, h, i: (bi, 0, h, 0)),
                  bias_spec, bias_spec],
        out_specs=head_tile,
        out_shape=jax.ShapeDtypeStruct((b, nb, ATTN_WIDTH, blk), BF16),
        scratch_shapes=[pltpu.VMEM((nb, K_PAD), F32),
                        pltpu.VMEM((nb, blk), F32),
                        pltpu.VMEM((1, blk), F32),
                        pltpu.VMEM((1, blk), F32),
                        pltpu.VMEM((HEAD_DIM, blk), F32)],
        compiler_params=_params(3),
        name="moba_attention",
    )(rel_bias, qt, k, vt, bias_own, bias_prev)


def _layer_norm(z, g, b):
    mu = jnp.mean(z, axis=-1, keepdims=True)
    zc = z - mu
    var = jnp.mean(zc * zc, axis=-1, keepdims=True)
    return zc * lax.rsqrt(var + LN_EPS) * g + b


def _mixer_kernel(alpha, x_ref, at_ref, wpg_ref, wpool_ref, pscale_ref, wba_ref, wbp_ref, wout_ref,
                  g_ref, b_ref, h_ref, pbuf_ref):
    t = pl.program_id(1)
    tile = x_ref.shape[0]
    d = x_ref.shape[1]
    x = x_ref[...]
    proj = jnp.dot(x.astype(BF16), wpg_ref[...], preferred_element_type=F32)
    p = proj[:, :POOL_WIDTH]

    @pl.when(t == 0)
    def _():
        pbuf_ref[0:POOL_HALO, :] = jnp.zeros((POOL_HALO, POOL_WIDTH), F32)

    pbuf_ref[POOL_HALO:POOL_HALO + tile, :] = p
    pos = t * tile + lax.broadcasted_iota(jnp.int32, (tile, POOL_GROUP), 0)
    ys = []
    for gi, w in enumerate(POOL_WINDOWS):
        cols = slice(gi * POOL_GROUP, (gi + 1) * POOL_GROUP)
        pg = p[:, cols]
        win = pg
        for back in range(1, w):
            win = win + pbuf_ref[POOL_HALO - back:POOL_HALO - back + tile, cols]
        cnt = jnp.minimum(pos + 1, w).astype(F32)
        diff = win / cnt - pg
        ys.append(jnp.dot(diff.astype(BF16), wpool_ref[gi], preferred_element_type=F32))
    pbuf_ref[0:POOL_HALO, :] = pbuf_ref[tile:tile + POOL_HALO, :]
    y = jnp.concatenate(ys, axis=1) * pscale_ref[...]
    y_pool = jnp.dot(y.astype(BF16), wbp_ref[...], preferred_element_type=F32)

    y_attn = jnp.concatenate(
        [lax.dot_general(at_ref[c], wba_ref[...], TN_DIMS, preferred_element_type=F32)
         for c in range(at_ref.shape[0])], axis=0)

    gates = 1.0 / (1.0 + jnp.exp(-proj[:, POOL_WIDTH:]))
    mixed = gates[:, :d] * y_attn + gates[:, d:] * y_pool
    z = alpha * x + jnp.dot(mixed.astype(BF16), wout_ref[...], preferred_element_type=F32)
    h_ref[...] = _layer_norm(z, g_ref[...], b_ref[...])


def _mixer(alpha, x, attn_t, wpg, wpool, pscale, wba, wbp, wout, ln_g, ln_b):
    b, s, d = x.shape
    bpt = TOKEN_TILE // MOBA_BLOCK
    consts = (wpg, wpool, pscale, wba, wbp, wout, ln_g, ln_b)
    return pl.pallas_call(
        functools.partial(_mixer_kernel, alpha),
        grid=(b, s // TOKEN_TILE),
        in_specs=[pl.BlockSpec((None, TOKEN_TILE, d), lambda i, t: (i, t, 0)),
                  pl.BlockSpec((None, bpt, ATTN_WIDTH, MOBA_BLOCK), lambda i, t: (i, t, 0, 0))]
                 + [_const_spec(c.shape) for c in consts],
        out_specs=pl.BlockSpec((None, TOKEN_TILE, d), lambda i, t: (i, t, 0)),
        out_shape=jax.ShapeDtypeStruct((b, s, d), F32),
        scratch_shapes=[pltpu.VMEM((POOL_HALO + TOKEN_TILE, POOL_WIDTH), F32)],
        compiler_params=_params(2),
        name="mixer",
    )(x, attn_t, *consts)


def _ffn_kernel(alpha, h_ref, wa_ref, wu_ref, cw_ref, cb_ref, wo_ref, g_ref, b_ref, o_ref,
                abuf_ref, carry_ref, f_ref):
    t = pl.program_id(1)
    tile = h_ref.shape[0]
    n_chunks = wa_ref.shape[0]
    h = h_ref[...]
    hb = h.astype(BF16)

    @pl.when(t == 0)
    def _():
        carry_ref[...] = jnp.zeros(carry_ref.shape, F32)

    f_ref[...] = jnp.zeros(f_ref.shape, F32)

    def chunk(c, carry):
        a = jnp.dot(hb, wa_ref[c], preferred_element_type=F32)
        u = jnp.dot(hb, wu_ref[c], preferred_element_type=F32)
        abuf_ref[0:CONV_HALO, :] = carry_ref[c]
        abuf_ref[CONV_HALO:CONV_HALO + tile, :] = a
        carry_ref[c] = a[tile - CONV_HALO:tile, :]
        cw = cw_ref[c]
        y = cb_ref[c]
        for tap in range(CONV_WIDTH):
            back = CONV_WIDTH - 1 - tap
            src = a if back == 0 else abuf_ref[CONV_HALO - back:CONV_HALO - back + tile, :]
            y = y + src * cw[tap:tap + 1, :]
        act = 0.5 * y * (1.0 + lax.erf(y * math.sqrt(0.5))) * u
        f_ref[...] += jnp.dot(act.astype(BF16), wo_ref[c], preferred_element_type=F32)
        return carry

    lax.fori_loop(0, n_chunks, chunk, 0)
    o_ref[...] = _layer_norm(alpha * h + f_ref[...], g_ref[...], b_ref[...])


def _conv_ffn(alpha, h, wa, wu, cw, cb, wo, ln_g, ln_b):
    b, s, d = h.shape
    n_chunks = wa.shape[0]
    consts = (wa, wu, cw, cb, wo, ln_g, ln_b)
    return pl.pallas_call(
        functools.partial(_ffn_kernel, alpha),
        grid=(b, s // TOKEN_TILE),
        in_specs=[pl.BlockSpec((None, TOKEN_TILE, d), lambda i, t: (i, t, 0))]
                 + [_const_spec(c.shape) for c in consts],
        out_specs=pl.BlockSpec((None, TOKEN_TILE, d), lambda i, t: (i, t, 0)),
        out_shape=jax.ShapeDtypeStruct((b, s, d), F32),
        scratch_shapes=[pltpu.VMEM((CONV_HALO + TOKEN_TILE, FF_CHUNK), F32),
                        pltpu.VMEM((n_chunks, CONV_HALO, FF_CHUNK), F32),
                        pltpu.VMEM((TOKEN_TILE, d), F32)],
        compiler_params=_params(2),
        name="conv_ffn",
    )(h, *consts)


def _chunk_cols(w, n_chunks):
    r = w.shape[0]
    return w.reshape(r, n_chunks, -1).transpose(1, 0, 2)


def kernel(x, w_in, rel_bias, w_pool_group, pool_scale, w_branch_attn, w_branch_pool, w_out, ln1_g,
           ln1_b, w_ffn_in, conv_w, conv_b, w_ffn_out, ln2_g, ln2_b):
    b, s, d = x.shape
    depth = w_in.shape[0]
    d_ff = w_ffn_out.shape[1]
    assert s % TOKEN_TILE == 0 and TOKEN_TILE % MOBA_BLOCK == 0 and d_ff % FF_CHUNK == 0
    assert w_in.shape[2] == 3 * ATTN_WIDTH + POOL_WIDTH + 2 * d
    n_chunks = d_ff // FF_CHUNK
    alpha = (2.0 * depth) ** 0.25
    scale = 1.0 / math.sqrt(HEAD_DIM)

    bias_own, bias_prev = _rel_bias_tables(rel_bias)
    h = x
    for l in range(depth):
        w = w_in[l]
        wq_t = (w[:, :ATTN_WIDTH] * scale).T.astype(BF16)
        wk = w[:, ATTN_WIDTH:2 * ATTN_WIDTH].reshape(d, N_HEADS, HEAD_DIM)
        wk_pad = jnp.pad(wk, ((0, 0), (0, 0), (0, K_PAD - HEAD_DIM))).reshape(d, N_HEADS * K_PAD)
        wv_t = w[:, 2 * ATTN_WIDTH:3 * ATTN_WIDTH].T.astype(BF16)
        qt, k, vt = _qkv_proj(h, wq_t, wk_pad.astype(BF16), wv_t)
        attn_t = _moba_attention(rel_bias, qt, k, vt, bias_own, bias_prev)
        h1 = _mixer(alpha, h, attn_t,
                    w[:, 3 * ATTN_WIDTH:].astype(BF16),
                    w_pool_group[l].astype(BF16),
                    pool_scale[l].reshape(1, POOL_WIDTH),
                    w_branch_attn[l].astype(BF16),
                    w_branch_pool[l].astype(BF16),
                    w_out[l].astype(BF16),
                    ln1_g[l].reshape(1, d), ln1_b[l].reshape(1, d))
        wf = w_ffn_in[l].astype(BF16)
        h = _conv_ffn(alpha, h1,
                      _chunk_cols(wf[:, :d_ff], n_chunks),
                      _chunk_cols(wf[:, d_ff:], n_chunks),
                      _chunk_cols(conv_w[l], n_chunks),
                      _chunk_cols(conv_b[l].reshape(1, d_ff), n_chunks),
                      w_ffn_out[l].astype(BF16).reshape(n_chunks, FF_CHUNK, d),
                      ln2_g[l].reshape(1, d), ln2_b[l].reshape(1, d))
    return h
```

```python
import functools
import math

import jax
import jax.numpy as jnp
from jax import lax
from jax.experimental import pallas as pl
from jax.experimental.pallas import tpu as pltpu

N_HEADS = 8
HEAD_DIM = 64
ATTN_WIDTH = N_HEADS * HEAD_DIM
MOBA_BLOCK = 256
MOBA_TOPK = 3
POOL_WINDOWS = (2, 4, 8, 16)
POOL_GROUP = 128
POOL_WIDTH = POOL_GROUP * len(POOL_WINDOWS)
N_BUCKETS = 32
MAX_DISTANCE = 128
CONV_WIDTH = 3
LN_EPS = 1e-5
NEG = -1e30
M_INIT = 0.5 * NEG

LANES = 128
K_PAD = LANES
V_ONES = 16
POOL_HALO = max(POOL_WINDOWS)
CONV_HALO = 8
FF_CHUNK = 256
TOKEN_TILE = 512
VMEM_LIMIT = 56 * 1024 * 1024

F32 = jnp.float32
BF16 = jnp.bfloat16
NT_DIMS = (((1,), (1,)), ((), ()))
TN_DIMS = (((0,), (0,)), ((), ()))


def _params(n_grid):
    return pltpu.CompilerParams(dimension_semantics=("arbitrary",) * n_grid,
                                vmem_limit_bytes=VMEM_LIMIT)


def _const_spec(shape):
    nd = len(shape)
    return pl.BlockSpec(shape, lambda *_: (0,) * nd)


def _rel_bias_kernel(rb_ref, own_ref, prev_ref):
    h = pl.program_id(0)
    kk = lax.broadcasted_iota(jnp.int32, (MOBA_BLOCK, MOBA_BLOCK), 0)
    qq = lax.broadcasted_iota(jnp.int32, (MOBA_BLOCK, MOBA_BLOCK), 1)
    max_exact = N_BUCKETS // 2

    def table(dist):
        n = jnp.maximum(dist, 0)
        nf = jnp.maximum(n, 1).astype(F32)
        large = max_exact + (jnp.log(nf / max_exact) / math.log(MAX_DISTANCE / max_exact)
                             * (N_BUCKETS - max_exact)).astype(jnp.int32)
        large = jnp.minimum(large, N_BUCKETS - 1)
        bucket = jnp.where(n < max_exact, n, large)
        out = jnp.zeros(dist.shape, F32)
        for b in range(N_BUCKETS):
            out = jnp.where(bucket == b, rb_ref[h, b], out)
        return out

    own_ref[...] = table(qq - kk)
    prev_ref[...] = table(qq - kk + MOBA_BLOCK)


def _rel_bias_tables(rel_bias):
    shape = jax.ShapeDtypeStruct((N_HEADS, MOBA_BLOCK, MOBA_BLOCK), F32)
    spec = pl.BlockSpec((None, MOBA_BLOCK, MOBA_BLOCK), lambda h: (h, 0, 0))
    return pl.pallas_call(
        _rel_bias_kernel,
        grid=(N_HEADS,),
        in_specs=[pl.BlockSpec(memory_space=pltpu.SMEM)],
        out_specs=[spec, spec],
        out_shape=[shape, shape],
        compiler_params=_params(1),
        name="rel_bias_tables",
    )(rel_bias)


def _qkv_kernel(x_ref, wq_ref, wk_ref, wv_ref, qt_ref, k_ref, vt_ref):
    xb = x_ref[...].astype(BF16)
    qt = lax.dot_general(wq_ref[...], xb, NT_DIMS, preferred_element_type=F32)
    vt = lax.dot_general(wv_ref[...], xb, NT_DIMS, preferred_element_type=F32)
    k_ref[...] = jnp.dot(xb, wk_ref[...], preferred_element_type=F32).astype(BF16)
    for c in range(qt_ref.shape[0]):
        cols = slice(c * MOBA_BLOCK, (c + 1) * MOBA_BLOCK)
        qt_ref[c] = qt[:, cols].astype(BF16)
        vt_ref[c] = vt[:, cols].astype(BF16)


def _qkv_proj(x, wq_t, wk_pad, wv_t):
    b, s, d = x.shape
    nb = s // MOBA_BLOCK
    bpt = TOKEN_TILE // MOBA_BLOCK
    t_shape = jax.ShapeDtypeStruct((b, nb, ATTN_WIDTH, MOBA_BLOCK), BF16)
    t_spec = pl.BlockSpec((None, bpt, ATTN_WIDTH, MOBA_BLOCK), lambda i, t: (i, t, 0, 0))
    return pl.pallas_call(
        _qkv_kernel,
        grid=(b, s // TOKEN_TILE),
        in_specs=[pl.BlockSpec((None, TOKEN_TILE, d), lambda i, t: (i, t, 0)),
                  _const_spec(wq_t.shape), _const_spec(wk_pad.shape), _const_spec(wv_t.shape)],
        out_specs=[t_spec,
                   pl.BlockSpec((None, TOKEN_TILE, N_HEADS * K_PAD), lambda i, t: (i, t, 0)),
                   t_spec],
        out_shape=[t_shape, jax.ShapeDtypeStruct((b, s, N_HEADS * K_PAD), BF16), t_shape],
        compiler_params=_params(2),
        name="qkv_proj",
    )(x, wq_t, wk_pad, wv_t)


def _select_blocks(gate, i):
    nb = gate.shape[0]
    bidx = lax.broadcasted_iota(jnp.int32, gate.shape, 0)
    avail = jnp.where(bidx < i, 1, 0)
    chosen = jnp.zeros(gate.shape, jnp.int32)
    for _ in range(MOBA_TOPK):
        live = avail == 1
        top = jnp.max(jnp.where(live, gate, -jnp.inf), axis=0, keepdims=True)
        cand = jnp.where(live, jnp.where(gate == top, bidx, nb), nb)
        pick = jnp.where(bidx == jnp.min(cand, axis=0, keepdims=True), 1, 0)
        chosen = chosen + pick
        avail = avail - pick
    return chosen


def _attn_kernel(rb_ref, qt_ref, k_ref, vt_ref, bown_ref, bprev_ref, o_ref,
                 kmean_ref, qa_ref, pen_ref, s_ref, m_ref, acc_ref):
    i = pl.program_id(1)
    nb = vt_ref.shape[0]
    blk = MOBA_BLOCK

    @pl.when(i == 0)
    def _():
        for j in range(nb):
            kj = k_ref[j * blk:(j + 1) * blk, :].astype(F32)
            kmean_ref[j:j + 1, :] = jnp.sum(kj, axis=0, keepdims=True) * (1.0 / blk)

    bidx = lax.broadcasted_iota(jnp.int32, (nb, blk), 0)
    for h in range(N_HEADS):
        qa = jnp.concatenate([qt_ref[h * HEAD_DIM:(h + 1) * HEAD_DIM, :],
                              jnp.zeros((K_PAD - HEAD_DIM, blk), BF16)], axis=0)
        qa_ref[h] = qa
        km = kmean_ref[:, h * K_PAD:(h + 1) * K_PAD]
        km_hi = km.astype(BF16)
        rem = km - km_hi.astype(F32)
        km_mid = rem.astype(BF16)
        km_lo = (rem - km_mid.astype(F32)).astype(BF16)
        g3 = jnp.dot(jnp.concatenate([km_hi, km_mid, km_lo], axis=0), qa, preferred_element_type=F32)
        gate = g3[0:nb] + g3[nb:2 * nb] + g3[2 * nb:3 * nb]
        chosen = _select_blocks(gate, i)
        far_bias = jnp.where(bidx < i - 1, rb_ref[h, N_BUCKETS - 1], 0.0)
        pen_ref[h] = jnp.where(chosen == 1, far_bias, NEG)

    m_ref[...] = jnp.full(m_ref.shape, M_INIT, F32)
    acc_ref[...] = jnp.zeros(acc_ref.shape, F32)

    ones = jnp.ones((V_ONES, blk), BF16)
    causal = (lax.broadcasted_iota(jnp.int32, (blk, blk), 1)
              >= lax.broadcasted_iota(jnp.int32, (blk, blk), 0))

    def put_scores(h, j, slot):
        start = pl.multiple_of(j * blk, blk)
        s_ref[slot, h] = jnp.dot(k_ref[pl.ds(start, blk), h * K_PAD:(h + 1) * K_PAD], qa_ref[h],
                                 preferred_element_type=F32)

    def absorb(h, j, slot, kind):
        s = s_ref[slot, h]
        if kind == "own":
            s = jnp.where(causal, s + bown_ref[h], NEG)
        elif kind == "prev":
            s = s + bprev_ref[h] + pen_ref[h, pl.ds(j, 1), :]
        else:
            s = s + pen_ref[h, pl.ds(j, 1), :]
        m_old = m_ref[h]
        m_new = jnp.maximum(m_old, jnp.max(s, axis=0, keepdims=True))
        p = jnp.exp(s - m_new).astype(BF16)
        va = jnp.concatenate([vt_ref[j, h * HEAD_DIM:(h + 1) * HEAD_DIM, :], ones], axis=0)
        acc_ref[h] = jnp.exp(m_old - m_new) * acc_ref[h] + jnp.dot(va, p, preferred_element_type=F32)
        m_ref[h] = m_new

    def stage(j, slot, kind, refill=None):
        for h in range(N_HEADS):
            absorb(h, j, slot, kind)
            if refill is not None:
                put_scores(h, refill, slot)

    for h in range(N_HEADS):
        put_scores(h, 0, 0)
    for h in range(N_HEADS):
        put_scores(h, jnp.minimum(1, i), 1)

    def pair_body(a, carry):
        j0 = 2 * a
        stage(j0, 0, "far", refill=j0 + 2)
        stage(j0 + 1, 1, "far", refill=j0 + 3)
        return carry

    n_far = jnp.maximum(i - 1, 0)
    lax.fori_loop(0, lax.shift_right_logical(n_far, 1), pair_body, 0)

    @pl.when(i == 0)
    def _():
        stage(i, 0, "own")

    @pl.when((i & 1) == 1)
    def _():
        stage(i - 1, 0, "prev")
        stage(i, 1, "own")

    @pl.when(jnp.logical_and(i >= 2, (i & 1) == 0))
    def _():
        stage(i - 2, 0, "far", refill=i)
        stage(i - 1, 1, "prev")
        stage(i, 0, "own")

    for h in range(N_HEADS):
        acc = acc_ref[h]
        o_ref[h * HEAD_DIM:(h + 1) * HEAD_DIM, :] = (
            acc[0:HEAD_DIM] / acc[HEAD_DIM:HEAD_DIM + 1]).astype(o_ref.dtype)


def _moba_attention(rel_bias, qt, k, vt, bias_own, bias_prev):
    b, nb, _, blk = qt.shape
    s = nb * blk
    q_tile = pl.BlockSpec((None, None, ATTN_WIDTH, blk), lambda bi, i: (bi, i, 0, 0))
    return pl.pallas_call(
        _attn_kernel,
        grid=(b, nb),
        in_specs=[pl.BlockSpec(memory_space=pltpu.SMEM),
                  q_tile,
                  pl.BlockSpec((None, s, N_HEADS * K_PAD), lambda bi, i: (bi, 0, 0)),
                  pl.BlockSpec((None, nb, ATTN_WIDTH, blk), lambda bi, i: (bi, 0, 0, 0)),
                  _const_spec(bias_own.shape), _const_spec(bias_prev.shape)],
        out_specs=q_tile,
        out_shape=jax.ShapeDtypeStruct((b, nb, ATTN_WIDTH, blk), BF16),
        scratch_shapes=[pltpu.VMEM((nb, N_HEADS * K_PAD), F32),
                        pltpu.VMEM((N_HEADS, K_PAD, blk), BF16),
                        pltpu.VMEM((N_HEADS, nb, blk), F32),
                        pltpu.VMEM((2, N_HEADS, blk, blk), F32),
                        pltpu.VMEM((N_HEADS, 1, blk), F32),
                        pltpu.VMEM((N_HEADS, HEAD_DIM + V_ONES, blk), F32)],
        compiler_params=_params(2),
        name="moba_attention",
    )(rel_bias, qt, k, vt, bias_own, bias_prev)


def _layer_norm(z, g, b):
    mu = jnp.mean(z, axis=-1, keepdims=True)
    zc = z - mu
    var = jnp.mean(zc * zc, axis=-1, keepdims=True)
    return zc * lax.rsqrt(var + LN_EPS) * g + b


def _mixer_kernel(alpha, x_ref, at_ref, wpg_ref, wpool_ref, pscale_ref, wba_ref, wbp_ref, wout_ref,
                  g_ref, b_ref, h_ref, pbuf_ref):
    t = pl.program_id(1)
    tile = x_ref.shape[0]
    d = x_ref.shape[1]
    x = x_ref[...]
    proj = jnp.dot(x.astype(BF16), wpg_ref[...], preferred_element_type=F32)
    p = proj[:, :POOL_WIDTH]

    @pl.when(t == 0)
    def _():
        pbuf_ref[0:POOL_HALO, :] = jnp.zeros((POOL_HALO, POOL_WIDTH), F32)

    pbuf_ref[POOL_HALO:POOL_HALO + tile, :] = p
    pos = t * tile + lax.broadcasted_iota(jnp.int32, (tile, POOL_GROUP), 0)
    ys = []
    for gi, w in enumerate(POOL_WINDOWS):
        cols = slice(gi * POOL_GROUP, (gi + 1) * POOL_GROUP)
        pg = p[:, cols]
        win = pg
        for back in range(1, w):
            win = win + pbuf_ref[POOL_HALO - back:POOL_HALO - back + tile, cols]
        cnt = jnp.minimum(pos + 1, w).astype(F32)
        diff = win / cnt - pg
        ys.append(jnp.dot(diff.astype(BF16), wpool_ref[gi], preferred_element_type=F32))
    pbuf_ref[0:POOL_HALO, :] = pbuf_ref[tile:tile + POOL_HALO, :]
    y = jnp.concatenate(ys, axis=1) * pscale_ref[...]
    y_pool = jnp.dot(y.astype(BF16), wbp_ref[...], preferred_element_type=F32)

    y_attn = jnp.concatenate(
        [lax.dot_general(at_ref[c], wba_ref[...], TN_DIMS, preferred_element_type=F32)
         for c in range(at_ref.shape[0])], axis=0)

    gates = 1.0 / (1.0 + jnp.exp(-proj[:, POOL_WIDTH:]))
    mixed = gates[:, :d] * y_attn + gates[:, d:] * y_pool
    z = alpha * x + jnp.dot(mixed.astype(BF16), wout_ref[...], preferred_element_type=F32)
    h_ref[...] = _layer_norm(z, g_ref[...], b_ref[...])


def _mixer(alpha, x, attn_t, wpg, wpool, pscale, wba, wbp, wout, ln_g, ln_b):
    b, s, d = x.shape
    bpt = TOKEN_TILE // MOBA_BLOCK
    consts = (wpg, wpool, pscale, wba, wbp, wout, ln_g, ln_b)
    return pl.pallas_call(
        functools.partial(_mixer_kernel, alpha),
        grid=(b, s // TOKEN_TILE),
        in_specs=[pl.BlockSpec((None, TOKEN_TILE, d), lambda i, t: (i, t, 0)),
                  pl.BlockSpec((None, bpt, ATTN_WIDTH, MOBA_BLOCK), lambda i, t: (i, t, 0, 0))]
                 + [_const_spec(c.shape) for c in consts],
        out_specs=pl.BlockSpec((None, TOKEN_TILE, d), lambda i, t: (i, t, 0)),
        out_shape=jax.ShapeDtypeStruct((b, s, d), F32),
        scratch_shapes=[pltpu.VMEM((POOL_HALO + TOKEN_TILE, POOL_WIDTH), F32)],
        compiler_params=_params(2),
        name="mixer",
    )(x, attn_t, *consts)


def _ffn_kernel(alpha, h_ref, wa_ref, wu_ref, cw_ref, cb_ref, wo_ref, g_ref, b_ref, o_ref,
                abuf_ref, carry_ref, f_ref):
    t = pl.program_id(1)
    tile = h_ref.shape[0]
    n_chunks = wa_ref.shape[0]
    h = h_ref[...]
    hb = h.astype(BF16)

    @pl.when(t == 0)
    def _():
        carry_ref[...] = jnp.zeros(carry_ref.shape, F32)

    f_ref[...] = jnp.zeros(f_ref.shape, F32)

    def chunk(c, carry):
        a = jnp.dot(hb, wa_ref[c], preferred_element_type=F32)
        u = jnp.dot(hb, wu_ref[c], preferred_element_type=F32)
        abuf_ref[0:CONV_HALO, :] = carry_ref[c]
        abuf_ref[CONV_HALO:CONV_HALO + tile, :] = a
        carry_ref[c] = a[tile - CONV_HALO:tile, :]
        cw = cw_ref[c]
        y = cb_ref[c]
        for tap in range(CONV_WIDTH):
            back = CONV_WIDTH - 1 - tap
            src = a if back == 0 else abuf_ref[CONV_HALO - back:CONV_HALO - back + tile, :]
            y = y + src * cw[tap:tap + 1, :]
        act = 0.5 * y * (1.0 + lax.erf(y * math.sqrt(0.5))) * u
        f_ref[...] += jnp.dot(act.astype(BF16), wo_ref[c], preferred_element_type=F32)
        return carry

    lax.fori_loop(0, n_chunks, chunk, 0)
    o_ref[...] = _layer_norm(alpha * h + f_ref[...], g_ref[...], b_ref[...])


def _conv_ffn(alpha, h, wa, wu, cw, cb, wo, ln_g, ln_b):
    b, s, d = h.shape
    n_chunks = wa.shape[0]
    consts = (wa, wu, cw, cb, wo, ln_g, ln_b)
    return pl.pallas_call(
        functools.partial(_ffn_kernel, alpha),
        grid=(b, s // TOKEN_TILE),
        in_specs=[pl.BlockSpec((None, TOKEN_TILE, d), lambda i, t: (i, t, 0))]
                 + [_const_spec(c.shape) for c in consts],
        out_specs=pl.BlockSpec((None, TOKEN_TILE, d), lambda i, t: (i, t, 0)),
        out_shape=jax.ShapeDtypeStruct((b, s, d), F32),
        scratch_shapes=[pltpu.VMEM((CONV_HALO + TOKEN_TILE, FF_CHUNK), F32),
                        pltpu.VMEM((n_chunks, CONV_HALO, FF_CHUNK), F32),
                        pltpu.VMEM((TOKEN_TILE, d), F32)],
        compiler_params=_params(2),
        name="conv_ffn",
    )(h, *consts)


def _chunk_cols(w, n_chunks):
    r = w.shape[0]
    return w.reshape(r, n_chunks, -1).transpose(1, 0, 2)


def kernel(x, w_in, rel_bias, w_pool_group, pool_scale, w_branch_attn, w_branch_pool, w_out, ln1_g,
           ln1_b, w_ffn_in, conv_w, conv_b, w_ffn_out, ln2_g, ln2_b):
    b, s, d = x.shape
    depth = w_in.shape[0]
    d_ff = w_ffn_out.shape[1]
    assert s % TOKEN_TILE == 0 and TOKEN_TILE % MOBA_BLOCK == 0 and d_ff % FF_CHUNK == 0
    assert w_in.shape[2] == 3 * ATTN_WIDTH + POOL_WIDTH + 2 * d
    n_chunks = d_ff // FF_CHUNK
    alpha = (2.0 * depth) ** 0.25
    scale = 1.0 / math.sqrt(HEAD_DIM)

    bias_own, bias_prev = _rel_bias_tables(rel_bias)
    h = x
    for l in range(depth):
        w = w_in[l]
        wq_t = (w[:, :ATTN_WIDTH] * scale).T.astype(BF16)
        wk = w[:, ATTN_WIDTH:2 * ATTN_WIDTH].reshape(d, N_HEADS, HEAD_DIM)
        wk_pad = jnp.pad(wk, ((0, 0), (0, 0), (0, K_PAD - HEAD_DIM))).reshape(d, N_HEADS * K_PAD)
        wv_t = w[:, 2 * ATTN_WIDTH:3 * ATTN_WIDTH].T.astype(BF16)
        qt, k, vt = _qkv_proj(h, wq_t, wk_pad.astype(BF16), wv_t)
        attn_t = _moba_attention(rel_bias, qt, k, vt, bias_own, bias_prev)
        h1 = _mixer(alpha, h, attn_t,
                    w[:, 3 * ATTN_WIDTH:].astype(BF16),
                    w_pool_group[l].astype(BF16),
                    pool_scale[l].reshape(1, POOL_WIDTH),
                    w_branch_attn[l].astype(BF16),
                    w_branch_pool[l].astype(BF16),
                    w_out[l].astype(BF16),
                    ln1_g[l].reshape(1, d), ln1_b[l].reshape(1, d))
        wf = w_ffn_in[l].astype(BF16)
        h = _conv_ffn(alpha, h1,
                      _chunk_cols(wf[:, :d_ff], n_chunks),
                      _chunk_cols(wf[:, d_ff:], n_chunks),
                      _chunk_cols(conv_w[l], n_chunks),
                      _chunk_cols(conv_b[l].reshape(1, d_ff), n_chunks),
                      w_ffn_out[l].astype(BF16).reshape(n_chunks, FF_CHUNK, d),
                      ln2_g[l].reshape(1, d), ln2_b[l].reshape(1, d))
    return h
```

```python
import functools
import math

import jax
import jax.numpy as jnp
from jax import lax
from jax.experimental import pallas as pl
from jax.experimental.pallas import tpu as pltpu

N_HEADS = 8
HEAD_DIM = 64
ATTN_WIDTH = N_HEADS * HEAD_DIM
MOBA_BLOCK = 256
MOBA_TOPK = 3
POOL_WINDOWS = (2, 4, 8, 16)
POOL_GROUP = 128
POOL_WIDTH = POOL_GROUP * len(POOL_WINDOWS)
N_BUCKETS = 32
MAX_DISTANCE = 128
CONV_WIDTH = 3
LN_EPS = 1e-5
NEG = -1e30
M_INIT = 0.5 * NEG

LANES = 128
K_PAD = LANES
V_ONES = 16
POOL_HALO = max(POOL_WINDOWS)
CONV_HALO = 8
FF_CHUNK = 256
TOKEN_TILE = 512
VMEM_LIMIT = 56 * 1024 * 1024

F32 = jnp.float32
BF16 = jnp.bfloat16
NT_DIMS = (((1,), (1,)), ((), ()))
TN_DIMS = (((0,), (0,)), ((), ()))


def _params(n_grid):
    return pltpu.CompilerParams(dimension_semantics=("arbitrary",) * n_grid,
                                vmem_limit_bytes=VMEM_LIMIT)


def _const_spec(shape):
    nd = len(shape)
    return pl.BlockSpec(shape, lambda *_: (0,) * nd)


def _rel_bias_kernel(rb_ref, own_ref, prev_ref):
    h = pl.program_id(0)
    kk = lax.broadcasted_iota(jnp.int32, (MOBA_BLOCK, MOBA_BLOCK), 0)
    qq = lax.broadcasted_iota(jnp.int32, (MOBA_BLOCK, MOBA_BLOCK), 1)
    max_exact = N_BUCKETS // 2

    def table(dist):
        n = jnp.maximum(dist, 0)
        nf = jnp.maximum(n, 1).astype(F32)
        large = max_exact + (jnp.log(nf / max_exact) / math.log(MAX_DISTANCE / max_exact)
                             * (N_BUCKETS - max_exact)).astype(jnp.int32)
        large = jnp.minimum(large, N_BUCKETS - 1)
        bucket = jnp.where(n < max_exact, n, large)
        out = jnp.zeros(dist.shape, F32)
        for b in range(N_BUCKETS):
            out = jnp.where(bucket == b, rb_ref[h, b], out)
        return out

    own_ref[...] = table(qq - kk)
    prev_ref[...] = table(qq - kk + MOBA_BLOCK)


def _rel_bias_tables(rel_bias):
    shape = jax.ShapeDtypeStruct((N_HEADS, MOBA_BLOCK, MOBA_BLOCK), F32)
    spec = pl.BlockSpec((None, MOBA_BLOCK, MOBA_BLOCK), lambda h: (h, 0, 0))
    return pl.pallas_call(
        _rel_bias_kernel,
        grid=(N_HEADS,),
        in_specs=[pl.BlockSpec(memory_space=pltpu.SMEM)],
        out_specs=[spec, spec],
        out_shape=[shape, shape],
        compiler_params=_params(1),
        name="rel_bias_tables",
    )(rel_bias)


def _qkv_kernel(x_ref, wq_ref, wk_ref, wv_ref, qt_ref, k_ref, vt_ref):
    xb = x_ref[...].astype(BF16)
    qt = lax.dot_general(wq_ref[...], xb, NT_DIMS, preferred_element_type=F32)
    vt = lax.dot_general(wv_ref[...], xb, NT_DIMS, preferred_element_type=F32)
    k_ref[...] = jnp.dot(xb, wk_ref[...], preferred_element_type=F32).astype(BF16)
    for c in range(qt_ref.shape[0]):
        cols = slice(c * MOBA_BLOCK, (c + 1) * MOBA_BLOCK)
        qt_ref[c] = qt[:, cols].astype(BF16)
        vt_ref[c] = vt[:, cols].astype(BF16)


def _qkv_proj(x, wq_t, wk_pad, wv_t):
    b, s, d = x.shape
    nb = s // MOBA_BLOCK
    bpt = TOKEN_TILE // MOBA_BLOCK
    t_shape = jax.ShapeDtypeStruct((b, nb, ATTN_WIDTH, MOBA_BLOCK), BF16)
    t_spec = pl.BlockSpec((None, bpt, ATTN_WIDTH, MOBA_BLOCK), lambda i, t: (i, t, 0, 0))
    return pl.pallas_call(
        _qkv_kernel,
        grid=(b, s // TOKEN_TILE),
        in_specs=[pl.BlockSpec((None, TOKEN_TILE, d), lambda i, t: (i, t, 0)),
                  _const_spec(wq_t.shape), _const_spec(wk_pad.shape), _const_spec(wv_t.shape)],
        out_specs=[t_spec,
                   pl.BlockSpec((None, TOKEN_TILE, N_HEADS * K_PAD), lambda i, t: (i, t, 0)),
                   t_spec],
        out_shape=[t_shape, jax.ShapeDtypeStruct((b, s, N_HEADS * K_PAD), BF16), t_shape],
        compiler_params=_params(2),
        name="qkv_proj",
    )(x, wq_t, wk_pad, wv_t)


def _select_blocks(gate, i):
    nb = gate.shape[0]
    bidx = lax.broadcasted_iota(jnp.int32, gate.shape, 0)
    avail = jnp.where(bidx < i, 1, 0)
    chosen = jnp.zeros(gate.shape, jnp.int32)
    for _ in range(MOBA_TOPK):
        live = avail == 1
        top = jnp.max(jnp.where(live, gate, -jnp.inf), axis=0, keepdims=True)
        cand = jnp.where(live, jnp.where(gate == top, bidx, nb), nb)
        pick = jnp.where(bidx == jnp.min(cand, axis=0, keepdims=True), 1, 0)
        chosen = chosen + pick
        avail = avail - pick
    return chosen


def _attn_kernel(rb_ref, qt_ref, k_ref, vt_ref, bown_ref, bprev_ref, o_ref,
                 kmean_ref, qa_ref, pen_ref, s_ref, m_ref, acc_ref):
    i = pl.program_id(1)
    nb = vt_ref.shape[0]
    blk = MOBA_BLOCK

    @pl.when(i == 0)
    def _():
        for j in range(nb):
            kj = k_ref[j * blk:(j + 1) * blk, :].astype(F32)
            kmean_ref[j:j + 1, :] = jnp.sum(kj, axis=0, keepdims=True) * (1.0 / blk)

    bidx = lax.broadcasted_iota(jnp.int32, (nb, blk), 0)
    for h in range(N_HEADS):
        qa = jnp.concatenate([qt_ref[h * HEAD_DIM:(h + 1) * HEAD_DIM, :],
                              jnp.zeros((K_PAD - HEAD_DIM, blk), BF16)], axis=0)
        qa_ref[h] = qa
        km = kmean_ref[:, h * K_PAD:(h + 1) * K_PAD]
        km_hi = km.astype(BF16)
        rem = km - km_hi.astype(F32)
        km_mid = rem.astype(BF16)
        km_lo = (rem - km_mid.astype(F32)).astype(BF16)
        g3 = jnp.dot(jnp.concatenate([km_hi, km_mid, km_lo], axis=0), qa, preferred_element_type=F32)
        gate = g3[0:nb] + g3[nb:2 * nb] + g3[2 * nb:3 * nb]
        chosen = _select_blocks(gate, i)
        far_bias = jnp.where(bidx < i - 1, rb_ref[h, N_BUCKETS - 1], 0.0)
        pen_ref[h] = jnp.where(chosen == 1, far_bias, NEG)

    m_ref[...] = jnp.full(m_ref.shape, M_INIT, F32)
    acc_ref[...] = jnp.zeros(acc_ref.shape, F32)

    ones = jnp.ones((V_ONES, blk), BF16)
    causal = (lax.broadcasted_iota(jnp.int32, (blk, blk), 1)
              >= lax.broadcasted_iota(jnp.int32, (blk, blk), 0))

    def put_scores(h, j, slot):
        start = pl.multiple_of(j * blk, blk)
        s_ref[slot, h] = jnp.dot(k_ref[pl.ds(start, blk), h * K_PAD:(h + 1) * K_PAD], qa_ref[h],
                                 preferred_element_type=F32)

    def absorb(h, j, slot, kind):
        s = s_ref[slot, h]
        if kind == "own":
            s = jnp.where(causal, s + bown_ref[h], NEG)
        elif kind == "prev":
            s = s + bprev_ref[h] + pen_ref[h, pl.ds(j, 1), :]
        else:
            s = s + pen_ref[h, pl.ds(j, 1), :]
        m_old = m_ref[h]
        m_new = jnp.maximum(m_old, jnp.max(s, axis=0, keepdims=True))
        p = jnp.exp(s - m_new).astype(BF16)
        va = jnp.concatenate([vt_ref[j, h * HEAD_DIM:(h + 1) * HEAD_DIM, :], ones], axis=0)
        acc_ref[h] = jnp.exp(m_old - m_new) * acc_ref[h] + jnp.dot(va, p, preferred_element_type=F32)
        m_ref[h] = m_new

    def stage(j, slot, kind, refill=None):
        for h in range(N_HEADS):
            absorb(h, j, slot, kind)
            if refill is not None:
                put_scores(h, refill, slot)

    for h in range(N_HEADS):
        put_scores(h, 0, 0)
    for h in range(N_HEADS):
        put_scores(h, jnp.minimum(1, i), 1)

    def pair_body(a, carry):
        j0 = 2 * a
        stage(j0, 0, "far", refill=j0 + 2)
        stage(j0 + 1, 1, "far", refill=j0 + 3)
        return carry

    n_far = jnp.maximum(i - 1, 0)
    lax.fori_loop(0, lax.shift_right_logical(n_far, 1), pair_body, 0)

    @pl.when(i == 0)
    def _():
        stage(i, 0, "own")

    @pl.when((i & 1) == 1)
    def _():
        stage(i - 1, 0, "prev")
        stage(i, 1, "own")

    @pl.when(jnp.logical_and(i >= 2, (i & 1) == 0))
    def _():
        stage(i - 2, 0, "far", refill=i)
        stage(i - 1, 1, "prev")
        stage(i, 0, "own")

    for h in range(N_HEADS):
        acc = acc_ref[h]
        o_ref[h * HEAD_DIM:(h + 1) * HEAD_DIM, :] = (
            acc[0:HEAD_DIM] / acc[HEAD_DIM:HEAD_DIM + 1]).astype(o_ref.dtype)


def _moba_attention(rel_bias, qt, k, vt, bias_own, bias_prev):
    b, nb, _, blk = qt.shape
    s = nb * blk
    q_tile = pl.BlockSpec((None, None, ATTN_WIDTH, blk), lambda bi, i: (bi, i, 0, 0))
    return pl.pallas_call(
        _attn_kernel,
        grid=(b, nb),
        in_specs=[pl.BlockSpec(memory_space=pltpu.SMEM),
                  q_tile,
                  pl.BlockSpec((None, s, N_HEADS * K_PAD), lambda bi, i: (bi, 0, 0)),
                  pl.BlockSpec((None, nb, ATTN_WIDTH, blk), lambda bi, i: (bi, 0, 0, 0)),
                  _const_spec(bias_own.shape), _const_spec(bias_prev.shape)],
        out_specs=q_tile,
        out_shape=jax.ShapeDtypeStruct((b, nb, ATTN_WIDTH, blk), BF16),
        scratch_shapes=[pltpu.VMEM((nb, N_HEADS * K_PAD), F32),
                        pltpu.VMEM((N_HEADS, K_PAD, blk), BF16),
                        pltpu.VMEM((N_HEADS, nb, blk), F32),
                        pltpu.VMEM((2, N_HEADS, blk, blk), F32),
                        pltpu.VMEM((N_HEADS, 1, blk), F32),
                        pltpu.VMEM((N_HEADS, HEAD_DIM + V_ONES, blk), F32)],
        compiler_params=_params(2),
        name="moba_attention",
    )(rel_bias, qt, k, vt, bias_own, bias_prev)


def _layer_norm(z, g, b):
    mu = jnp.mean(z, axis=-1, keepdims=True)
    zc = z - mu
    var = jnp.mean(zc * zc, axis=-1, keepdims=True)
    return zc * lax.rsqrt(var + LN_EPS) * g + b


def _mixer_kernel(alpha, x_ref, at_ref, wpg_ref, wpool_ref, pscale_ref, wba_ref, wbp_ref, wout_ref,
                  g_ref, b_ref, h_ref, pbuf_ref):
    t = pl.program_id(1)
    tile = x_ref.shape[0]
    d = x_ref.shape[1]
    x = x_ref[...]
    proj = jnp.dot(x.astype(BF16), wpg_ref[...], preferred_element_type=F32)
    p = proj[:, :POOL_WIDTH]

    @pl.when(t == 0)
    def _():
        pbuf_ref[0:POOL_HALO, :] = jnp.zeros((POOL_HALO, POOL_WIDTH), F32)

    pbuf_ref[POOL_HALO:POOL_HALO + tile, :] = p
    pos = t * tile + lax.broadcasted_iota(jnp.int32, (tile, POOL_GROUP), 0)
    ys = []
    for gi, w in enumerate(POOL_WINDOWS):
        cols = slice(gi * POOL_GROUP, (gi + 1) * POOL_GROUP)
        pg = p[:, cols]
        win = pg
        for back in range(1, w):
            win = win + pbuf_ref[POOL_HALO - back:POOL_HALO - back + tile, cols]
        cnt = jnp.minimum(pos + 1, w).astype(F32)
        diff = win / cnt - pg
        ys.append(jnp.dot(diff.astype(BF16), wpool_ref[gi], preferred_element_type=F32))
    pbuf_ref[0:POOL_HALO, :] = pbuf_ref[tile:tile + POOL_HALO, :]
    y = jnp.concatenate(ys, axis=1) * pscale_ref[...]
    y_pool = jnp.dot(y.astype(BF16), wbp_ref[...], preferred_element_type=F32)

    y_attn = jnp.concatenate(
        [lax.dot_general(at_ref[c], wba_ref[...], TN_DIMS, preferred_element_type=F32)
         for c in range(at_ref.shape[0])], axis=0)

    gates = 1.0 / (1.0 + jnp.exp(-proj[:, POOL_WIDTH:]))
    mixed = gates[:, :d] * y_attn + gates[:, d:] * y_pool
    z = alpha * x + jnp.dot(mixed.astype(BF16), wout_ref[...], preferred_element_type=F32)
    h_ref[...] = _layer_norm(z, g_ref[...], b_ref[...])


def _mixer(alpha, x, attn_t, wpg, wpool, pscale, wba, wbp, wout, ln_g, ln_b):
    b, s, d = x.shape
    bpt = TOKEN_TILE // MOBA_BLOCK
    consts = (wpg, wpool, pscale, wba, wbp, wout, ln_g, ln_b)
    return pl.pallas_call(
        functools.partial(_mixer_kernel, alpha),
        grid=(b, s // TOKEN_TILE),
        in_specs=[pl.BlockSpec((None, TOKEN_TILE, d), lambda i, t: (i, t, 0)),
                  pl.BlockSpec((None, bpt, ATTN_WIDTH, MOBA_BLOCK), lambda i, t: (i, t, 0, 0))]
                 + [_const_spec(c.shape) for c in consts],
        out_specs=pl.BlockSpec((None, TOKEN_TILE, d), lambda i, t: (i, t, 0)),
        out_shape=jax.ShapeDtypeStruct((b, s, d), F32),
        scratch_shapes=[pltpu.VMEM((POOL_HALO + TOKEN_TILE, POOL_WIDTH), F32)],
        compiler_params=_params(2),
        name="mixer",
    )(x, attn_t, *consts)


def _ffn_kernel(alpha, h_ref, wa_ref, wu_ref, cw_ref, cb_ref, wo_ref, g_ref, b_ref, o_ref,
                carry_ref):
    t = pl.program_id(1)
    tile = h_ref.shape[0]
    n_chunks = wa_ref.shape[0]
    h = h_ref[...]
    hb = h.astype(BF16)

    @pl.when(t == 0)
    def _():
        carry_ref[...] = jnp.zeros(carry_ref.shape, F32)

    row = lax.broadcasted_iota(jnp.int32, (CONV_HALO, FF_CHUNK), 0)

    def project(c):
        return (jnp.dot(hb, wa_ref[c], preferred_element_type=F32),
                jnp.dot(hb, wu_ref[c], preferred_element_type=F32))

    def shifted(a, prev, back):
        rolled = pltpu.roll(a, back, 0)
        top = rolled[0:CONV_HALO]
        for r in range(back):
            top = jnp.where(row == r, prev[CONV_HALO - back + r:CONV_HALO - back + r + 1, :], top)
        return jnp.concatenate([top, rolled[CONV_HALO:]], axis=0)

    f = None
    nxt = project(0)
    for c in range(n_chunks):
        a, u = nxt
        if c + 1 < n_chunks:
            nxt = project(c + 1)
        prev = carry_ref[c]
        carry_ref[c] = a[tile - CONV_HALO:tile, :]
        cw = cw_ref[c]
        y = cb_ref[c]
        for tap in range(CONV_WIDTH):
            back = CONV_WIDTH - 1 - tap
            src = a if back == 0 else shifted(a, prev, back)
            y = y + src * cw[tap:tap + 1, :]
        act = 0.5 * y * (1.0 + lax.erf(y * math.sqrt(0.5))) * u
        part = jnp.dot(act.astype(BF16), wo_ref[c], preferred_element_type=F32)
        f = part if f is None else f + part
    o_ref[...] = _layer_norm(alpha * h + f, g_ref[...], b_ref[...])


def _conv_ffn(alpha, h, wa, wu, cw, cb, wo, ln_g, ln_b):
    b, s, d = h.shape
    n_chunks = wa.shape[0]
    consts = (wa, wu, cw, cb, wo, ln_g, ln_b)
    return pl.pallas_call(
        functools.partial(_ffn_kernel, alpha),
        grid=(b, s // TOKEN_TILE),
        in_specs=[pl.BlockSpec((None, TOKEN_TILE, d), lambda i, t: (i, t, 0))]
                 + [_const_spec(c.shape) for c in consts],
        out_specs=pl.BlockSpec((None, TOKEN_TILE, d), lambda i, t: (i, t, 0)),
        out_shape=jax.ShapeDtypeStruct((b, s, d), F32),
        scratch_shapes=[pltpu.VMEM((n_chunks, CONV_HALO, FF_CHUNK), F32)],
        compiler_params=_params(2),
        name="conv_ffn",
    )(h, *consts)


def _chunk_cols(w, n_chunks):
    r = w.shape[0]
    return w.reshape(r, n_chunks, -1).transpose(1, 0, 2)


def kernel(x, w_in, rel_bias, w_pool_group, pool_scale, w_branch_attn, w_branch_pool, w_out, ln1_g,
           ln1_b, w_ffn_in, conv_w, conv_b, w_ffn_out, ln2_g, ln2_b):
    b, s, d = x.shape
    depth = w_in.shape[0]
    d_ff = w_ffn_out.shape[1]
    assert s % TOKEN_TILE == 0 and TOKEN_TILE % MOBA_BLOCK == 0 and d_ff % FF_CHUNK == 0
    assert w_in.shape[2] == 3 * ATTN_WIDTH + POOL_WIDTH + 2 * d
    n_chunks = d_ff // FF_CHUNK
    alpha = (2.0 * depth) ** 0.25
    scale = 1.0 / math.sqrt(HEAD_DIM)

    bias_own, bias_prev = _rel_bias_tables(rel_bias)
    h = x
    for l in range(depth):
        w = w_in[l]
        wq_t = (w[:, :ATTN_WIDTH] * scale).T.astype(BF16)
        wk = w[:, ATTN_WIDTH:2 * ATTN_WIDTH].reshape(d, N_HEADS, HEAD_DIM)
        wk_pad = jnp.pad(wk, ((0, 0), (0, 0), (0, K_PAD - HEAD_DIM))).reshape(d, N_HEADS * K_PAD)
        wv_t = w[:, 2 * ATTN_WIDTH:3 * ATTN_WIDTH].T.astype(BF16)
        qt, k, vt = _qkv_proj(h, wq_t, wk_pad.astype(BF16), wv_t)
        attn_t = _moba_attention(rel_bias, qt, k, vt, bias_own, bias_prev)
        h1 = _mixer(alpha, h, attn_t,
                    w[:, 3 * ATTN_WIDTH:].astype(BF16),
                    w_pool_group[l].astype(BF16),
                    pool_scale[l].reshape(1, POOL_WIDTH),
                    w_branch_attn[l].astype(BF16),
                    w_branch_pool[l].astype(BF16),
                    w_out[l].astype(BF16),
                    ln1_g[l].reshape(1, d), ln1_b[l].reshape(1, d))
        wf = w_ffn_in[l].astype(BF16)
        h = _conv_ffn(alpha, h1,
                      _chunk_cols(wf[:, :d_ff], n_chunks),
                      _chunk_cols(wf[:, d_ff:], n_chunks),
                      _chunk_cols(conv_w[l], n_chunks),
                      _chunk_cols(conv_b[l].reshape(1, d_ff), n_chunks),
                      w_ffn_out[l].astype(BF16).reshape(n_chunks, FF_CHUNK, d),
                      ln2_g[l].reshape(1, d), ln2_b[l].reshape(1, d))
    return h
```

```python
import functools
import math

import jax
import jax.numpy as jnp
from jax import lax
from jax.experimental import pallas as pl
from jax.experimental.pallas import tpu as pltpu

N_HEADS = 8
HEAD_DIM = 64
ATTN_WIDTH = N_HEADS * HEAD_DIM
MOBA_BLOCK = 256
MOBA_TOPK = 3
POOL_WINDOWS = (2, 4, 8, 16)
POOL_GROUP = 128
POOL_WIDTH = POOL_GROUP * len(POOL_WINDOWS)
N_BUCKETS = 32
MAX_DISTANCE = 128
CONV_WIDTH = 3
LN_EPS = 1e-5
NEG = -1e30
M_INIT = 0.5 * NEG
LOG2E = math.log2(math.e)
PEN_SPLIT = 3

LANES = 128
K_PAD = LANES
V_ONES = 16
POOL_HALO = max(POOL_WINDOWS)
CONV_HALO = 8
FF_CHUNK = 256
TOKEN_TILE = 512
VMEM_LIMIT = 56 * 1024 * 1024

F32 = jnp.float32
BF16 = jnp.bfloat16
NT_DIMS = (((1,), (1,)), ((), ()))
TN_DIMS = (((0,), (0,)), ((), ()))


def _params(n_grid):
    return pltpu.CompilerParams(dimension_semantics=("arbitrary",) * n_grid,
                                vmem_limit_bytes=VMEM_LIMIT)


def _const_spec(shape):
    nd = len(shape)
    return pl.BlockSpec(shape, lambda *_: (0,) * nd)


def _rel_bias_kernel(rb_ref, own_ref, prev_ref):
    h = pl.program_id(0)
    kk = lax.broadcasted_iota(jnp.int32, (MOBA_BLOCK, MOBA_BLOCK), 0)
    qq = lax.broadcasted_iota(jnp.int32, (MOBA_BLOCK, MOBA_BLOCK), 1)
    max_exact = N_BUCKETS // 2

    def table(dist):
        n = jnp.maximum(dist, 0)
        nf = jnp.maximum(n, 1).astype(F32)
        large = max_exact + (jnp.log(nf / max_exact) / math.log(MAX_DISTANCE / max_exact)
                             * (N_BUCKETS - max_exact)).astype(jnp.int32)
        large = jnp.minimum(large, N_BUCKETS - 1)
        bucket = jnp.where(n < max_exact, n, large)
        out = jnp.zeros(dist.shape, F32)
        for b in range(N_BUCKETS):
            out = jnp.where(bucket == b, rb_ref[h, b], out)
        return out

    own_ref[...] = table(qq - kk) * LOG2E
    prev_ref[...] = table(qq - kk + MOBA_BLOCK) * LOG2E


def _rel_bias_tables(rel_bias):
    shape = jax.ShapeDtypeStruct((N_HEADS, MOBA_BLOCK, MOBA_BLOCK), F32)
    spec = pl.BlockSpec((None, MOBA_BLOCK, MOBA_BLOCK), lambda h: (h, 0, 0))
    return pl.pallas_call(
        _rel_bias_kernel,
        grid=(N_HEADS,),
        in_specs=[pl.BlockSpec(memory_space=pltpu.SMEM)],
        out_specs=[spec, spec],
        out_shape=[shape, shape],
        compiler_params=_params(1),
        name="rel_bias_tables",
    )(rel_bias)


def _qkv_kernel(n_blocks, x_ref, wq_ref, wk_ref, wv_ref, qt_ref, k_ref, vt_ref):
    xb = x_ref[...].astype(BF16)
    qt = lax.dot_general(wq_ref[...], xb, NT_DIMS, preferred_element_type=F32)
    vt = lax.dot_general(wv_ref[...], xb, NT_DIMS, preferred_element_type=F32)
    k = jnp.dot(xb, wk_ref[...], preferred_element_type=F32)
    tile = k.shape[0]
    blk_id = (pl.program_id(1) * (tile // MOBA_BLOCK)
              + lax.shift_right_logical(lax.broadcasted_iota(jnp.int32, k.shape, 0),
                                        MOBA_BLOCK.bit_length() - 1))
    pad_lane = (lax.broadcasted_iota(jnp.int32, k.shape, 1) & (K_PAD - 1)) - HEAD_DIM
    hot = pad_lane == blk_id
    for rep in range(1, PEN_SPLIT):
        hot = hot | (pad_lane == blk_id + rep * n_blocks)
    k_ref[...] = jnp.where(hot, 1.0, k).astype(BF16)
    for c in range(qt_ref.shape[0]):
        cols = slice(c * MOBA_BLOCK, (c + 1) * MOBA_BLOCK)
        qt_ref[c] = qt[:, cols].astype(BF16)
        vt_ref[c] = vt[:, cols].astype(BF16)


def _qkv_proj(x, wq_t, wk_pad, wv_t):
    b, s, d = x.shape
    nb = s // MOBA_BLOCK
    bpt = TOKEN_TILE // MOBA_BLOCK
    t_shape = jax.ShapeDtypeStruct((b, nb, ATTN_WIDTH, MOBA_BLOCK), BF16)
    t_spec = pl.BlockSpec((None, bpt, ATTN_WIDTH, MOBA_BLOCK), lambda i, t: (i, t, 0, 0))
    return pl.pallas_call(
        functools.partial(_qkv_kernel, nb),
        grid=(b, s // TOKEN_TILE),
        in_specs=[pl.BlockSpec((None, TOKEN_TILE, d), lambda i, t: (i, t, 0)),
                  _const_spec(wq_t.shape), _const_spec(wk_pad.shape), _const_spec(wv_t.shape)],
        out_specs=[t_spec,
                   pl.BlockSpec((None, TOKEN_TILE, N_HEADS * K_PAD), lambda i, t: (i, t, 0)),
                   t_spec],
        out_shape=[t_shape, jax.ShapeDtypeStruct((b, s, N_HEADS * K_PAD), BF16), t_shape],
        compiler_params=_params(2),
        name="qkv_proj",
    )(x, wq_t, wk_pad, wv_t)


def _split_bf16(x):
    parts = []
    rem = x
    for _ in range(PEN_SPLIT):
        part = rem.astype(BF16)
        parts.append(part)
        rem = rem - part.astype(F32)
    return parts


def _select_blocks(gate, i):
    nb = gate.shape[0]
    bidx = lax.broadcasted_iota(jnp.int32, gate.shape, 0)
    avail = jnp.where(bidx < i, 1, 0)
    chosen = jnp.zeros(gate.shape, jnp.int32)
    for _ in range(MOBA_TOPK):
        live = avail == 1
        top = jnp.max(jnp.where(live, gate, -jnp.inf), axis=0, keepdims=True)
        cand = jnp.where(live, jnp.where(gate == top, bidx, nb), nb)
        pick = jnp.where(bidx == jnp.min(cand, axis=0, keepdims=True), 1, 0)
        chosen = chosen + pick
        avail = avail - pick
    return chosen


def _attn_kernel(rb_ref, qt_ref, k_ref, vt_ref, bown_ref, bprev_ref, o_ref,
                 kmean_ref, qa_ref, s_ref, m_ref, acc_ref):
    i = pl.program_id(1)
    nb = vt_ref.shape[0]
    blk = MOBA_BLOCK

    @pl.when(i == 0)
    def _():
        for j in range(nb):
            kj = k_ref[j * blk:(j + 1) * blk, :].astype(F32)
            kmean_ref[j:j + 1, :] = jnp.sum(kj, axis=0, keepdims=True) * (1.0 / blk)

    bidx = lax.broadcasted_iota(jnp.int32, (nb, blk), 0)
    for h in range(N_HEADS):
        q = qt_ref[h * HEAD_DIM:(h + 1) * HEAD_DIM, :]
        qz = jnp.concatenate([q, jnp.zeros((K_PAD - HEAD_DIM, blk), BF16)], axis=0)
        parts = _split_bf16(kmean_ref[:, h * K_PAD:(h + 1) * K_PAD])
        g3 = jnp.dot(jnp.concatenate(parts, axis=0), qz, preferred_element_type=F32)
        gate = g3[0:nb]
        for r in range(1, PEN_SPLIT):
            gate = gate + g3[r * nb:(r + 1) * nb]
        chosen = _select_blocks(gate, i)
        far_bias = jnp.where(bidx < i - 1, rb_ref[h, N_BUCKETS - 1] * LOG2E, 0.0)
        pen = jnp.where(bidx == i, 0.0, jnp.where(chosen == 1, far_bias, NEG))
        qa_ref[h] = jnp.concatenate(
            [q] + _split_bf16(pen)
            + [jnp.zeros((K_PAD - HEAD_DIM - PEN_SPLIT * nb, blk), BF16)], axis=0)

    m_ref[...] = jnp.full(m_ref.shape, M_INIT, F32)
    acc_ref[...] = jnp.zeros(acc_ref.shape, F32)

    ones = jnp.ones((V_ONES, blk), BF16)
    causal = (lax.broadcasted_iota(jnp.int32, (blk, blk), 1)
              >= lax.broadcasted_iota(jnp.int32, (blk, blk), 0))

    def put_scores(h, j, slot):
        start = pl.multiple_of(j * blk, blk)
        s_ref[slot, h] = jnp.dot(k_ref[pl.ds(start, blk), h * K_PAD:(h + 1) * K_PAD], qa_ref[h],
                                 preferred_element_type=F32)

    def absorb(h, j, slot, kind):
        s = s_ref[slot, h]
        if kind == "own":
            s = jnp.where(causal, s + bown_ref[h], NEG)
        elif kind == "prev":
            s = s + bprev_ref[h]
        m_old = m_ref[h]
        m_new = jnp.maximum(m_old, jnp.max(s, axis=0, keepdims=True))
        p = jnp.exp2(s - m_new).astype(BF16)
        va = jnp.concatenate([vt_ref[j, h * HEAD_DIM:(h + 1) * HEAD_DIM, :], ones], axis=0)
        acc_ref[h] = jnp.exp2(m_old - m_new) * acc_ref[h] + jnp.dot(va, p, preferred_element_type=F32)
        m_ref[h] = m_new

    def stage(j, slot, kind, refill=None):
        for h in range(N_HEADS):
            absorb(h, j, slot, kind)
            if refill is not None:
                put_scores(h, refill, slot)

    for h in range(N_HEADS):
        put_scores(h, 0, 0)
    for h in range(N_HEADS):
        put_scores(h, jnp.minimum(1, i), 1)

    def pair_body(a, carry):
        j0 = 2 * a
        stage(j0, 0, "far", refill=j0 + 2)
        stage(j0 + 1, 1, "far", refill=j0 + 3)
        return carry

    n_far = jnp.maximum(i - 1, 0)
    lax.fori_loop(0, lax.shift_right_logical(n_far, 1), pair_body, 0)

    @pl.when(i == 0)
    def _():
        stage(i, 0, "own")

    @pl.when((i & 1) == 1)
    def _():
        stage(i - 1, 0, "prev")
        stage(i, 1, "own")

    @pl.when(jnp.logical_and(i >= 2, (i & 1) == 0))
    def _():
        stage(i - 2, 0, "far", refill=i)
        stage(i - 1, 1, "prev")
        stage(i, 0, "own")

    for h in range(N_HEADS):
        acc = acc_ref[h]
        o_ref[h * HEAD_DIM:(h + 1) * HEAD_DIM, :] = (
            acc[0:HEAD_DIM] / acc[HEAD_DIM:HEAD_DIM + 1]).astype(o_ref.dtype)


def _moba_attention(rel_bias, qt, k, vt, bias_own, bias_prev):
    b, nb, _, blk = qt.shape
    s = nb * blk
    q_tile = pl.BlockSpec((None, None, ATTN_WIDTH, blk), lambda bi, i: (bi, i, 0, 0))
    return pl.pallas_call(
        _attn_kernel,
        grid=(b, nb),
        in_specs=[pl.BlockSpec(memory_space=pltpu.SMEM),
                  q_tile,
                  pl.BlockSpec((None, s, N_HEADS * K_PAD), lambda bi, i: (bi, 0, 0)),
                  pl.BlockSpec((None, nb, ATTN_WIDTH, blk), lambda bi, i: (bi, 0, 0, 0)),
                  _const_spec(bias_own.shape), _const_spec(bias_prev.shape)],
        out_specs=q_tile,
        out_shape=jax.ShapeDtypeStruct((b, nb, ATTN_WIDTH, blk), BF16),
        scratch_shapes=[pltpu.VMEM((nb, N_HEADS * K_PAD), F32),
                        pltpu.VMEM((N_HEADS, K_PAD, blk), BF16),
                        pltpu.VMEM((2, N_HEADS, blk, blk), F32),
                        pltpu.VMEM((N_HEADS, 1, blk), F32),
                        pltpu.VMEM((N_HEADS, HEAD_DIM + V_ONES, blk), F32)],
        compiler_params=_params(2),
        name="moba_attention",
    )(rel_bias, qt, k, vt, bias_own, bias_prev)


def _layer_norm(z, g, b):
    mu = jnp.mean(z, axis=-1, keepdims=True)
    zc = z - mu
    var = jnp.mean(zc * zc, axis=-1, keepdims=True)
    return zc * lax.rsqrt(var + LN_EPS) * g + b


def _mixer_kernel(alpha, x_ref, at_ref, wpg_ref, wpool_ref, pscale_ref, wba_ref, wbp_ref, wout_ref,
                  g_ref, b_ref, h_ref, pbuf_ref):
    t = pl.program_id(1)
    tile = x_ref.shape[0]
    d = x_ref.shape[1]
    x = x_ref[...]
    proj = jnp.dot(x.astype(BF16), wpg_ref[...], preferred_element_type=F32)
    p = proj[:, :POOL_WIDTH]

    @pl.when(t == 0)
    def _():
        pbuf_ref[0:POOL_HALO, :] = jnp.zeros((POOL_HALO, POOL_WIDTH), F32)

    pbuf_ref[POOL_HALO:POOL_HALO + tile, :] = p
    pos = t * tile + lax.broadcasted_iota(jnp.int32, (tile, POOL_GROUP), 0)
    ys = []
    for gi, w in enumerate(POOL_WINDOWS):
        cols = slice(gi * POOL_GROUP, (gi + 1) * POOL_GROUP)
        pg = p[:, cols]
        win = pg
        for back in range(1, w):
            win = win + pbuf_ref[POOL_HALO - back:POOL_HALO - back + tile, cols]
        cnt = jnp.minimum(pos + 1, w).astype(F32)
        diff = win / cnt - pg
        ys.append(jnp.dot(diff.astype(BF16), wpool_ref[gi], preferred_element_type=F32))
    pbuf_ref[0:POOL_HALO, :] = pbuf_ref[tile:tile + POOL_HALO, :]
    y = jnp.concatenate(ys, axis=1) * pscale_ref[...]
    y_pool = jnp.dot(y.astype(BF16), wbp_ref[...], preferred_element_type=F32)

    y_attn = jnp.concatenate(
        [lax.dot_general(at_ref[c], wba_ref[...], TN_DIMS, preferred_element_type=F32)
         for c in range(at_ref.shape[0])], axis=0)

    gates = 1.0 / (1.0 + jnp.exp(-proj[:, POOL_WIDTH:]))
    mixed = gates[:, :d] * y_attn + gates[:, d:] * y_pool
    z = alpha * x + jnp.dot(mixed.astype(BF16), wout_ref[...], preferred_element_type=F32)
    h_ref[...] = _layer_norm(z, g_ref[...], b_ref[...])


def _mixer(alpha, x, attn_t, wpg, wpool, pscale, wba, wbp, wout, ln_g, ln_b):
    b, s, d = x.shape
    bpt = TOKEN_TILE // MOBA_BLOCK
    consts = (wpg, wpool, pscale, wba, wbp, wout, ln_g, ln_b)
    return pl.pallas_call(
        functools.partial(_mixer_kernel, alpha),
        grid=(b, s // TOKEN_TILE),
        in_specs=[pl.BlockSpec((None, TOKEN_TILE, d), lambda i, t: (i, t, 0)),
                  pl.BlockSpec((None, bpt, ATTN_WIDTH, MOBA_BLOCK), lambda i, t: (i, t, 0, 0))]
                 + [_const_spec(c.shape) for c in consts],
        out_specs=pl.BlockSpec((None, TOKEN_TILE, d), lambda i, t: (i, t, 0)),
        out_shape=jax.ShapeDtypeStruct((b, s, d), F32),
        scratch_shapes=[pltpu.VMEM((POOL_HALO + TOKEN_TILE, POOL_WIDTH), F32)],
        compiler_params=_params(2),
        name="mixer",
    )(x, attn_t, *consts)


def _ffn_kernel(alpha, h_ref, wa_ref, wu_ref, cw_ref, cb_ref, wo_ref, g_ref, b_ref, o_ref,
                carry_ref):
    t = pl.program_id(1)
    tile = h_ref.shape[0]
    n_chunks = wa_ref.shape[0]
    h = h_ref[...]
    hb = h.astype(BF16)

    @pl.when(t == 0)
    def _():
        carry_ref[...] = jnp.zeros(carry_ref.shape, F32)

    row = lax.broadcasted_iota(jnp.int32, (CONV_HALO, FF_CHUNK), 0)

    def project(c):
        return (jnp.dot(hb, wa_ref[c], preferred_element_type=F32),
                jnp.dot(hb, wu_ref[c], preferred_element_type=F32))

    def shifted(a, prev, back):
        rolled = pltpu.roll(a, back, 0)
        top = rolled[0:CONV_HALO]
        for r in range(back):
            top = jnp.where(row == r, prev[CONV_HALO - back + r:CONV_HALO - back + r + 1, :], top)
        return jnp.concatenate([top, rolled[CONV_HALO:]], axis=0)

    f = None
    nxt = project(0)
    for c in range(n_chunks):
        a, u = nxt
        if c + 1 < n_chunks:
            nxt = project(c + 1)
        prev = carry_ref[c]
        carry_ref[c] = a[tile - CONV_HALO:tile, :]
        cw = cw_ref[c]
        y = cb_ref[c]
        for tap in range(CONV_WIDTH):
            back = CONV_WIDTH - 1 - tap
            src = a if back == 0 else shifted(a, prev, back)
            y = y + src * cw[tap:tap + 1, :]
        act = 0.5 * y * (1.0 + lax.erf(y * math.sqrt(0.5))) * u
        part = jnp.dot(act.astype(BF16), wo_ref[c], preferred_element_type=F32)
        f = part if f is None else f + part
    o_ref[...] = _layer_norm(alpha * h + f, g_ref[...], b_ref[...])


def _conv_ffn(alpha, h, wa, wu, cw, cb, wo, ln_g, ln_b):
    b, s, d = h.shape
    n_chunks = wa.shape[0]
    consts = (wa, wu, cw, cb, wo, ln_g, ln_b)
    return pl.pallas_call(
        functools.partial(_ffn_kernel, alpha),
        grid=(b, s // TOKEN_TILE),
        in_specs=[pl.BlockSpec((None, TOKEN_TILE, d), lambda i, t: (i, t, 0))]
                 + [_const_spec(c.shape) for c in consts],
        out_specs=pl.BlockSpec((None, TOKEN_TILE, d), lambda i, t: (i, t, 0)),
        out_shape=jax.ShapeDtypeStruct((b, s, d), F32),
        scratch_shapes=[pltpu.VMEM((n_chunks, CONV_HALO, FF_CHUNK), F32)],
        compiler_params=_params(2),
        name="conv_ffn",
    )(h, *consts)


def _chunk_cols(w, n_chunks):
    r = w.shape[0]
    return w.reshape(r, n_chunks, -1).transpose(1, 0, 2)


def kernel(x, w_in, rel_bias, w_pool_group, pool_scale, w_branch_attn, w_branch_pool, w_out, ln1_g,
           ln1_b, w_ffn_in, conv_w, conv_b, w_ffn_out, ln2_g, ln2_b):
    b, s, d = x.shape
    depth = w_in.shape[0]
    d_ff = w_ffn_out.shape[1]
    assert s % TOKEN_TILE == 0 and TOKEN_TILE % MOBA_BLOCK == 0 and d_ff % FF_CHUNK == 0
    assert w_in.shape[2] == 3 * ATTN_WIDTH + POOL_WIDTH + 2 * d
    n_chunks = d_ff // FF_CHUNK
    alpha = (2.0 * depth) ** 0.25
    assert PEN_SPLIT * (s // MOBA_BLOCK) <= K_PAD - HEAD_DIM - 16
    scale = LOG2E / math.sqrt(HEAD_DIM)

    bias_own, bias_prev = _rel_bias_tables(rel_bias)
    h = x
    for l in range(depth):
        w = w_in[l]
        wq_t = (w[:, :ATTN_WIDTH] * scale).T.astype(BF16)
        wk = w[:, ATTN_WIDTH:2 * ATTN_WIDTH].reshape(d, N_HEADS, HEAD_DIM)
        wk_pad = jnp.pad(wk, ((0, 0), (0, 0), (0, K_PAD - HEAD_DIM))).reshape(d, N_HEADS * K_PAD)
        wv_t = w[:, 2 * ATTN_WIDTH:3 * ATTN_WIDTH].T.astype(BF16)
        qt, k, vt = _qkv_proj(h, wq_t, wk_pad.astype(BF16), wv_t)
        attn_t = _moba_attention(rel_bias, qt, k, vt, bias_own, bias_prev)
        h1 = _mixer(alpha, h, attn_t,
                    w[:, 3 * ATTN_WIDTH:].astype(BF16),
                    w_pool_group[l].astype(BF16),
                    pool_scale[l].reshape(1, POOL_WIDTH),
                    w_branch_attn[l].astype(BF16),
                    w_branch_pool[l].astype(BF16),
                    w_out[l].astype(BF16),
                    ln1_g[l].reshape(1, d), ln1_b[l].reshape(1, d))
        wf = w_ffn_in[l].astype(BF16)
        h = _conv_ffn(alpha, h1,
                      _chunk_cols(wf[:, :d_ff], n_chunks),
                      _chunk_cols(wf[:, d_ff:], n_chunks),
                      _chunk_cols(conv_w[l], n_chunks),
                      _chunk_cols(conv_b[l].reshape(1, d_ff), n_chunks),
                      w_ffn_out[l].astype(BF16).reshape(n_chunks, FF_CHUNK, d),
                      ln2_g[l].reshape(1, d), ln2_b[l].reshape(1, d))
    return h
```

```python
import functools
import math

import jax
import jax.numpy as jnp
from jax import lax
from jax.experimental import pallas as pl
from jax.experimental.pallas import tpu as pltpu

N_HEADS = 8
HEAD_DIM = 64
ATTN_WIDTH = N_HEADS * HEAD_DIM
MOBA_BLOCK = 256
MOBA_TOPK = 3
POOL_WINDOWS = (2, 4, 8, 16)
POOL_GROUP = 128
POOL_WIDTH = POOL_GROUP * len(POOL_WINDOWS)
N_BUCKETS = 32
MAX_DISTANCE = 128
CONV_WIDTH = 3
LN_EPS = 1e-5
NEG = -1e30
M_INIT = 0.5 * NEG
LOG2E = math.log2(math.e)
PEN_SPLIT = 3

LANES = 128
K_PAD = LANES
V_ONES = 16
POOL_HALO = max(POOL_WINDOWS)
CONV_HALO = 8
FF_CHUNK = 256
TOKEN_TILE = 512
FFN_TILE = 1024
VMEM_LIMIT = 56 * 1024 * 1024

F32 = jnp.float32
BF16 = jnp.bfloat16
NT_DIMS = (((1,), (1,)), ((), ()))
TN_DIMS = (((0,), (0,)), ((), ()))


def _params(n_grid):
    return pltpu.CompilerParams(dimension_semantics=("arbitrary",) * n_grid,
                                vmem_limit_bytes=VMEM_LIMIT)


def _const_spec(shape):
    nd = len(shape)
    return pl.BlockSpec(shape, lambda *_: (0,) * nd)


def _rel_bias_kernel(rb_ref, own_ref, prev_ref):
    h = pl.program_id(0)
    kk = lax.broadcasted_iota(jnp.int32, (MOBA_BLOCK, MOBA_BLOCK), 0)
    qq = lax.broadcasted_iota(jnp.int32, (MOBA_BLOCK, MOBA_BLOCK), 1)
    max_exact = N_BUCKETS // 2

    def table(dist):
        n = jnp.maximum(dist, 0)
        nf = jnp.maximum(n, 1).astype(F32)
        large = max_exact + (jnp.log(nf / max_exact) / math.log(MAX_DISTANCE / max_exact)
                             * (N_BUCKETS - max_exact)).astype(jnp.int32)
        large = jnp.minimum(large, N_BUCKETS - 1)
        bucket = jnp.where(n < max_exact, n, large)
        out = jnp.zeros(dist.shape, F32)
        for b in range(N_BUCKETS):
            out = jnp.where(bucket == b, rb_ref[h, b], out)
        return out

    own_ref[...] = table(qq - kk) * LOG2E
    prev_ref[...] = table(qq - kk + MOBA_BLOCK) * LOG2E


def _rel_bias_tables(rel_bias):
    shape = jax.ShapeDtypeStruct((N_HEADS, MOBA_BLOCK, MOBA_BLOCK), F32)
    spec = pl.BlockSpec((None, MOBA_BLOCK, MOBA_BLOCK), lambda h: (h, 0, 0))
    return pl.pallas_call(
        _rel_bias_kernel,
        grid=(N_HEADS,),
        in_specs=[pl.BlockSpec(memory_space=pltpu.SMEM)],
        out_specs=[spec, spec],
        out_shape=[shape, shape],
        compiler_params=_params(1),
        name="rel_bias_tables",
    )(rel_bias)


def _qkv_kernel(n_blocks, x_ref, wq_ref, wk_ref, wv_ref, qt_ref, k_ref, vt_ref):
    xb = x_ref[...].astype(BF16)
    qt = lax.dot_general(wq_ref[...], xb, NT_DIMS, preferred_element_type=F32)
    vt = lax.dot_general(wv_ref[...], xb, NT_DIMS, preferred_element_type=F32)
    k = jnp.dot(xb, wk_ref[...], preferred_element_type=F32)
    tile = k.shape[0]
    blk_id = (pl.program_id(1) * (tile // MOBA_BLOCK)
              + lax.shift_right_logical(lax.broadcasted_iota(jnp.int32, k.shape, 0),
                                        MOBA_BLOCK.bit_length() - 1))
    pad_lane = (lax.broadcasted_iota(jnp.int32, k.shape, 1) & (K_PAD - 1)) - HEAD_DIM
    hot = pad_lane == blk_id
    for rep in range(1, PEN_SPLIT):
        hot = hot | (pad_lane == blk_id + rep * n_blocks)
    k_ref[...] = jnp.where(hot, 1.0, k).astype(BF16)
    for c in range(qt_ref.shape[0]):
        cols = slice(c * MOBA_BLOCK, (c + 1) * MOBA_BLOCK)
        qt_ref[c] = qt[:, cols].astype(BF16)
        vt_ref[c] = vt[:, cols].astype(BF16)


def _qkv_proj(x, wq_t, wk_pad, wv_t):
    b, s, d = x.shape
    nb = s // MOBA_BLOCK
    bpt = TOKEN_TILE // MOBA_BLOCK
    t_shape = jax.ShapeDtypeStruct((b, nb, ATTN_WIDTH, MOBA_BLOCK), BF16)
    t_spec = pl.BlockSpec((None, bpt, ATTN_WIDTH, MOBA_BLOCK), lambda i, t: (i, t, 0, 0))
    return pl.pallas_call(
        functools.partial(_qkv_kernel, nb),
        grid=(b, s // TOKEN_TILE),
        in_specs=[pl.BlockSpec((None, TOKEN_TILE, d), lambda i, t: (i, t, 0)),
                  _const_spec(wq_t.shape), _const_spec(wk_pad.shape), _const_spec(wv_t.shape)],
        out_specs=[t_spec,
                   pl.BlockSpec((None, TOKEN_TILE, N_HEADS * K_PAD), lambda i, t: (i, t, 0)),
                   t_spec],
        out_shape=[t_shape, jax.ShapeDtypeStruct((b, s, N_HEADS * K_PAD), BF16), t_shape],
        compiler_params=_params(2),
        name="qkv_proj",
    )(x, wq_t, wk_pad, wv_t)


def _split_bf16(x):
    parts = []
    rem = x
    for _ in range(PEN_SPLIT):
        part = rem.astype(BF16)
        parts.append(part)
        rem = rem - part.astype(F32)
    return parts


def _select_blocks(gate, i):
    nb = gate.shape[0]
    bidx = lax.broadcasted_iota(jnp.int32, gate.shape, 0)
    avail = jnp.where(bidx < i, 1, 0)
    chosen = jnp.zeros(gate.shape, jnp.int32)
    for _ in range(MOBA_TOPK):
        live = avail == 1
        top = jnp.max(jnp.where(live, gate, -jnp.inf), axis=0, keepdims=True)
        cand = jnp.where(live, jnp.where(gate == top, bidx, nb), nb)
        pick = jnp.where(bidx == jnp.min(cand, axis=0, keepdims=True), 1, 0)
        chosen = chosen + pick
        avail = avail - pick
    return chosen


def _attn_kernel(rb_ref, qt_ref, k_ref, vt_ref, bown_ref, bprev_ref, o_ref,
                 kmean_ref, qa_ref, s_ref, m_ref, acc_ref):
    i = pl.program_id(1)
    nb = vt_ref.shape[0]
    blk = MOBA_BLOCK

    @pl.when(i == 0)
    def _():
        for j in range(nb):
            kj = k_ref[j * blk:(j + 1) * blk, :].astype(F32)
            kmean_ref[j:j + 1, :] = jnp.sum(kj, axis=0, keepdims=True) * (1.0 / blk)

    def put_scores(h, j, slot):
        start = pl.multiple_of(j * blk, blk)
        s_ref[slot, h] = jnp.dot(k_ref[pl.ds(start, blk), h * K_PAD:(h + 1) * K_PAD], qa_ref[h],
                                 preferred_element_type=F32)

    bidx = lax.broadcasted_iota(jnp.int32, (nb, blk), 0)
    gates = []
    for h in range(N_HEADS):
        q = qt_ref[h * HEAD_DIM:(h + 1) * HEAD_DIM, :]
        qz = jnp.concatenate([q, jnp.zeros((K_PAD - HEAD_DIM, blk), BF16)], axis=0)
        parts = _split_bf16(kmean_ref[:, h * K_PAD:(h + 1) * K_PAD])
        g3 = jnp.dot(jnp.concatenate(parts, axis=0), qz, preferred_element_type=F32)
        gate = g3[0:nb]
        for r in range(1, PEN_SPLIT):
            gate = gate + g3[r * nb:(r + 1) * nb]
        gates.append(gate)
    for h in range(N_HEADS):
        q = qt_ref[h * HEAD_DIM:(h + 1) * HEAD_DIM, :]
        chosen = _select_blocks(gates[h], i)
        far_bias = jnp.where(bidx < i - 1, rb_ref[h, N_BUCKETS - 1] * LOG2E, 0.0)
        pen = jnp.where(bidx == i, 0.0, jnp.where(chosen == 1, far_bias, NEG))
        qa_ref[h] = jnp.concatenate(
            [q] + _split_bf16(pen)
            + [jnp.zeros((K_PAD - HEAD_DIM - PEN_SPLIT * nb, blk), BF16)], axis=0)
        put_scores(h, 0, 0)
        put_scores(h, jnp.minimum(1, i), 1)

    m_ref[...] = jnp.full(m_ref.shape, M_INIT, F32)
    acc_ref[...] = jnp.zeros(acc_ref.shape, F32)

    ones = jnp.ones((V_ONES, blk), BF16)
    causal = (lax.broadcasted_iota(jnp.int32, (blk, blk), 1)
              >= lax.broadcasted_iota(jnp.int32, (blk, blk), 0))

    def absorb(h, j, slot, kind):
        s = s_ref[slot, h]
        if kind == "own":
            s = jnp.where(causal, s + bown_ref[h], NEG)
        elif kind == "prev":
            s = s + bprev_ref[h]
        m_old = m_ref[h]
        m_new = jnp.maximum(m_old, jnp.max(s, axis=0, keepdims=True))
        p = jnp.exp2(s - m_new).astype(BF16)
        va = jnp.concatenate([vt_ref[j, h * HEAD_DIM:(h + 1) * HEAD_DIM, :], ones], axis=0)
        acc_ref[h] = jnp.exp2(m_old - m_new) * acc_ref[h] + jnp.dot(va, p, preferred_element_type=F32)
        m_ref[h] = m_new

    def stage(j, slot, kind, refill=None):
        for h in range(N_HEADS):
            absorb(h, j, slot, kind)
            if refill is not None:
                put_scores(h, refill, slot)

    def pair_body(a, carry):
        j0 = 2 * a
        stage(j0, 0, "far", refill=j0 + 2)
        stage(j0 + 1, 1, "far", refill=j0 + 3)
        return carry

    n_far = jnp.maximum(i - 1, 0)
    lax.fori_loop(0, lax.shift_right_logical(n_far, 1), pair_body, 0)

    @pl.when(i == 0)
    def _():
        stage(i, 0, "own")

    @pl.when((i & 1) == 1)
    def _():
        stage(i - 1, 0, "prev")
        stage(i, 1, "own")

    @pl.when(jnp.logical_and(i >= 2, (i & 1) == 0))
    def _():
        stage(i - 2, 0, "far", refill=i)
        stage(i - 1, 1, "prev")
        stage(i, 0, "own")

    for h in range(N_HEADS):
        acc = acc_ref[h]
        o_ref[h * HEAD_DIM:(h + 1) * HEAD_DIM, :] = (
            acc[0:HEAD_DIM] / acc[HEAD_DIM:HEAD_DIM + 1]).astype(o_ref.dtype)


def _moba_attention(rel_bias, qt, k, vt, bias_own, bias_prev):
    b, nb, _, blk = qt.shape
    s = nb * blk
    q_tile = pl.BlockSpec((None, None, ATTN_WIDTH, blk), lambda bi, i: (bi, i, 0, 0))
    return pl.pallas_call(
        _attn_kernel,
        grid=(b, nb),
        in_specs=[pl.BlockSpec(memory_space=pltpu.SMEM),
                  q_tile,
                  pl.BlockSpec((None, s, N_HEADS * K_PAD), lambda bi, i: (bi, 0, 0)),
                  pl.BlockSpec((None, nb, ATTN_WIDTH, blk), lambda bi, i: (bi, 0, 0, 0)),
                  _const_spec(bias_own.shape), _const_spec(bias_prev.shape)],
        out_specs=q_tile,
        out_shape=jax.ShapeDtypeStruct((b, nb, ATTN_WIDTH, blk), BF16),
        scratch_shapes=[pltpu.VMEM((nb, N_HEADS * K_PAD), F32),
                        pltpu.VMEM((N_HEADS, K_PAD, blk), BF16),
                        pltpu.VMEM((2, N_HEADS, blk, blk), F32),
                        pltpu.VMEM((N_HEADS, 1, blk), F32),
                        pltpu.VMEM((N_HEADS, HEAD_DIM + V_ONES, blk), F32)],
        compiler_params=_params(2),
        name="moba_attention",
    )(rel_bias, qt, k, vt, bias_own, bias_prev)


def _layer_norm(z, g, b):
    mu = jnp.mean(z, axis=-1, keepdims=True)
    zc = z - mu
    var = jnp.mean(zc * zc, axis=-1, keepdims=True)
    return zc * lax.rsqrt(var + LN_EPS) * g + b


def _mixer_kernel(alpha, x_ref, at_ref, wp_ref, wg_ref, wpool_ref, pscale_ref, wba_ref, wbp_ref,
                  wout_ref, g_ref, b_ref, h_ref, pbuf_ref):
    t = pl.program_id(1)
    tile = x_ref.shape[0]
    d = x_ref.shape[1]

    @pl.when(t == 0)
    def _():
        pbuf_ref[0:POOL_HALO, :] = jnp.zeros((POOL_HALO, POOL_WIDTH), F32)

    x = x_ref[...]
    xb = x.astype(BF16)
    p = jnp.dot(xb, wp_ref[...], preferred_element_type=F32)
    gate_logits = jnp.dot(xb, wg_ref[...], preferred_element_type=F32)
    y_attn = jnp.concatenate(
        [lax.dot_general(at_ref[c], wba_ref[...], TN_DIMS, preferred_element_type=F32)
         for c in range(at_ref.shape[0])], axis=0)

    pbuf_ref[POOL_HALO:POOL_HALO + tile, :] = p
    pos = t * tile + lax.broadcasted_iota(jnp.int32, (tile, POOL_GROUP), 0)
    ys = []
    for gi, w in enumerate(POOL_WINDOWS):
        cols = slice(gi * POOL_GROUP, (gi + 1) * POOL_GROUP)
        pg = p[:, cols]
        win = pg
        for back in range(1, w):
            win = win + pbuf_ref[POOL_HALO - back:POOL_HALO - back + tile, cols]
        cnt = jnp.minimum(pos + 1, w).astype(F32)
        diff = win / cnt - pg
        ys.append(jnp.dot(diff.astype(BF16), wpool_ref[gi], preferred_element_type=F32))
    pbuf_ref[0:POOL_HALO, :] = pbuf_ref[tile:tile + POOL_HALO, :]
    y = jnp.concatenate(ys, axis=1) * pscale_ref[...]
    y_pool = jnp.dot(y.astype(BF16), wbp_ref[...], preferred_element_type=F32)

    gates = 1.0 / (1.0 + jnp.exp(-gate_logits))
    mixed = gates[:, :d] * y_attn + gates[:, d:] * y_pool
    z = alpha * x + jnp.dot(mixed.astype(BF16), wout_ref[...], preferred_element_type=F32)
    h_ref[...] = _layer_norm(z, g_ref[...], b_ref[...])


def _mixer(alpha, x, attn_t, wp, wg, wpool, pscale, wba, wbp, wout, ln_g, ln_b):
    b, s, d = x.shape
    bpt = TOKEN_TILE // MOBA_BLOCK
    consts = (wp, wg, wpool, pscale, wba, wbp, wout, ln_g, ln_b)
    return pl.pallas_call(
        functools.partial(_mixer_kernel, alpha),
        grid=(b, s // TOKEN_TILE),
        in_specs=[pl.BlockSpec((None, TOKEN_TILE, d), lambda i, t: (i, t, 0)),
                  pl.BlockSpec((None, bpt, ATTN_WIDTH, MOBA_BLOCK), lambda i, t: (i, t, 0, 0))]
                 + [_const_spec(c.shape) for c in consts],
        out_specs=pl.BlockSpec((None, TOKEN_TILE, d), lambda i, t: (i, t, 0)),
        out_shape=jax.ShapeDtypeStruct((b, s, d), F32),
        scratch_shapes=[pltpu.VMEM((POOL_HALO + TOKEN_TILE, POOL_WIDTH), F32)],
        compiler_params=_params(2),
        name="mixer",
    )(x, attn_t, *consts)


def _ffn_kernel(alpha, h_ref, wa_ref, wu_ref, cw_ref, cb_ref, wo_ref, g_ref, b_ref, o_ref,
                carry_ref):
    t = pl.program_id(1)
    tile = h_ref.shape[0]
    n_chunks = wa_ref.shape[0]
    h = h_ref[...]
    hb = h.astype(BF16)

    @pl.when(t == 0)
    def _():
        carry_ref[...] = jnp.zeros(carry_ref.shape, F32)

    row = lax.broadcasted_iota(jnp.int32, (CONV_HALO, FF_CHUNK), 0)

    def project(c):
        return (jnp.dot(hb, wa_ref[c], preferred_element_type=F32),
                jnp.dot(hb, wu_ref[c], preferred_element_type=F32))

    def shifted(a, prev, back):
        rolled = pltpu.roll(a, back, 0)
        top = rolled[0:CONV_HALO]
        for r in range(back):
            top = jnp.where(row == r, prev[CONV_HALO - back + r:CONV_HALO - back + r + 1, :], top)
        return jnp.concatenate([top, rolled[CONV_HALO:]], axis=0)

    f = None
    nxt = project(0)
    for c in range(n_chunks):
        a, u = nxt
        if c + 1 < n_chunks:
            nxt = project(c + 1)
        prev = carry_ref[c]
        carry_ref[c] = a[tile - CONV_HALO:tile, :]
        cw = cw_ref[c]
        y = cb_ref[c]
        for tap in range(CONV_WIDTH):
            back = CONV_WIDTH - 1 - tap
            src = a if back == 0 else shifted(a, prev, back)
            y = y + src * cw[tap:tap + 1, :]
        act = 0.5 * y * (1.0 + lax.erf(y * math.sqrt(0.5))) * u
        part = jnp.dot(act.astype(BF16), wo_ref[c], preferred_element_type=F32)
        f = part if f is None else f + part
    o_ref[...] = _layer_norm(alpha * h + f, g_ref[...], b_ref[...])


def _conv_ffn(alpha, h, wa, wu, cw, cb, wo, ln_g, ln_b):
    b, s, d = h.shape
    n_chunks = wa.shape[0]
    consts = (wa, wu, cw, cb, wo, ln_g, ln_b)
    return pl.pallas_call(
        functools.partial(_ffn_kernel, alpha),
        grid=(b, s // FFN_TILE),
        in_specs=[pl.BlockSpec((None, FFN_TILE, d), lambda i, t: (i, t, 0))]
                 + [_const_spec(c.shape) for c in consts],
        out_specs=pl.BlockSpec((None, FFN_TILE, d), lambda i, t: (i, t, 0)),
        out_shape=jax.ShapeDtypeStruct((b, s, d), F32),
        scratch_shapes=[pltpu.VMEM((n_chunks, CONV_HALO, FF_CHUNK), F32)],
        compiler_params=_params(2),
        name="conv_ffn",
    )(h, *consts)


def _chunk_cols(w, n_chunks):
    r = w.shape[0]
    return w.reshape(r, n_chunks, -1).transpose(1, 0, 2)


def kernel(x, w_in, rel_bias, w_pool_group, pool_scale, w_branch_attn, w_branch_pool, w_out, ln1_g,
           ln1_b, w_ffn_in, conv_w, conv_b, w_ffn_out, ln2_g, ln2_b):
    b, s, d = x.shape
    depth = w_in.shape[0]
    d_ff = w_ffn_out.shape[1]
    assert s % TOKEN_TILE == 0 and TOKEN_TILE % MOBA_BLOCK == 0 and d_ff % FF_CHUNK == 0
    assert s % FFN_TILE == 0
    assert w_in.shape[2] == 3 * ATTN_WIDTH + POOL_WIDTH + 2 * d
    n_chunks = d_ff // FF_CHUNK
    alpha = (2.0 * depth) ** 0.25
    assert PEN_SPLIT * (s // MOBA_BLOCK) <= K_PAD - HEAD_DIM - 16
    scale = LOG2E / math.sqrt(HEAD_DIM)

    bias_own, bias_prev = _rel_bias_tables(rel_bias)
    h = x
    for l in range(depth):
        w = w_in[l]
        wq_t = (w[:, :ATTN_WIDTH] * scale).T.astype(BF16)
        wk = w[:, ATTN_WIDTH:2 * ATTN_WIDTH].reshape(d, N_HEADS, HEAD_DIM)
        wk_pad = jnp.pad(wk, ((0, 0), (0, 0), (0, K_PAD - HEAD_DIM))).reshape(d, N_HEADS * K_PAD)
        wv_t = w[:, 2 * ATTN_WIDTH:3 * ATTN_WIDTH].T.astype(BF16)
        qt, k, vt = _qkv_proj(h, wq_t, wk_pad.astype(BF16), wv_t)
        attn_t = _moba_attention(rel_bias, qt, k, vt, bias_own, bias_prev)
        h1 = _mixer(alpha, h, attn_t,
                    w[:, 3 * ATTN_WIDTH:3 * ATTN_WIDTH + POOL_WIDTH].astype(BF16),
                    w[:, 3 * ATTN_WIDTH + POOL_WIDTH:].astype(BF16),
                    w_pool_group[l].astype(BF16),
                    pool_scale[l].reshape(1, POOL_WIDTH),
                    w_branch_attn[l].astype(BF16),
                    w_branch_pool[l].astype(BF16),
                    w_out[l].astype(BF16),
                    ln1_g[l].reshape(1, d), ln1_b[l].reshape(1, d))
        wf = w_ffn_in[l].astype(BF16)
        h = _conv_ffn(alpha, h1,
                      _chunk_cols(wf[:, :d_ff], n_chunks),
                      _chunk_cols(wf[:, d_ff:], n_chunks),
                      _chunk_cols(conv_w[l], n_chunks),
                      _chunk_cols(conv_b[l].reshape(1, d_ff), n_chunks),
                      w_ffn_out[l].astype(BF16).reshape(n_chunks, FF_CHUNK, d),
                      ln2_g[l].reshape(1, d), ln2_b[l].reshape(1, d))
    return h
```

```python
import functools
import math

import jax
import jax.numpy as jnp
from jax import lax
from jax.experimental import pallas as pl
from jax.experimental.pallas import tpu as pltpu

N_HEADS = 8
HEAD_DIM = 64
ATTN_WIDTH = N_HEADS * HEAD_DIM
MOBA_BLOCK = 256
MOBA_TOPK = 3
POOL_WINDOWS = (2, 4, 8, 16)
POOL_GROUP = 128
POOL_WIDTH = POOL_GROUP * len(POOL_WINDOWS)
N_BUCKETS = 32
MAX_DISTANCE = 128
CONV_WIDTH = 3
LN_EPS = 1e-5
NEG = -1e30
M_INIT = 0.5 * NEG
LOG2E = math.log2(math.e)
PEN_SPLIT = 3

LANES = 128
K_PAD = LANES
V_ONES = 16
POOL_HALO = max(POOL_WINDOWS)
CONV_HALO = 8
FF_CHUNK = 256
TOKEN_TILE = 512
FFN_TILE = 512
VMEM_LIMIT = 56 * 1024 * 1024

F32 = jnp.float32
BF16 = jnp.bfloat16
NT_DIMS = (((1,), (1,)), ((), ()))
TN_DIMS = (((0,), (0,)), ((), ()))


def _params(n_grid):
    return pltpu.CompilerParams(dimension_semantics=("arbitrary",) * n_grid,
                                vmem_limit_bytes=VMEM_LIMIT)


def _const_spec(shape):
    nd = len(shape)
    return pl.BlockSpec(shape, lambda *_: (0,) * nd)


def _rel_bias_kernel(rb_ref, own_ref, prev_ref):
    h = pl.program_id(0)
    kk = lax.broadcasted_iota(jnp.int32, (MOBA_BLOCK, MOBA_BLOCK), 0)
    qq = lax.broadcasted_iota(jnp.int32, (MOBA_BLOCK, MOBA_BLOCK), 1)
    max_exact = N_BUCKETS // 2

    def table(dist):
        n = jnp.maximum(dist, 0)
        nf = jnp.maximum(n, 1).astype(F32)
        large = max_exact + (jnp.log(nf / max_exact) / math.log(MAX_DISTANCE / max_exact)
                             * (N_BUCKETS - max_exact)).astype(jnp.int32)
        large = jnp.minimum(large, N_BUCKETS - 1)
        bucket = jnp.where(n < max_exact, n, large)
        out = jnp.zeros(dist.shape, F32)
        for b in range(N_BUCKETS):
            out = jnp.where(bucket == b, rb_ref[h, b], out)
        return out

    own_ref[...] = table(qq - kk) * LOG2E
    prev_ref[...] = table(qq - kk + MOBA_BLOCK) * LOG2E


def _rel_bias_tables(rel_bias):
    shape = jax.ShapeDtypeStruct((N_HEADS, MOBA_BLOCK, MOBA_BLOCK), F32)
    spec = pl.BlockSpec((None, MOBA_BLOCK, MOBA_BLOCK), lambda h: (h, 0, 0))
    return pl.pallas_call(
        _rel_bias_kernel,
        grid=(N_HEADS,),
        in_specs=[pl.BlockSpec(memory_space=pltpu.SMEM)],
        out_specs=[spec, spec],
        out_shape=[shape, shape],
        compiler_params=_params(1),
        name="rel_bias_tables",
    )(rel_bias)


def _qkv_kernel(n_blocks, x_ref, wq_ref, wk_ref, wv_ref, qt_ref, k_ref, vt_ref):
    xb = x_ref[...].astype(BF16)
    qt = lax.dot_general(wq_ref[...], xb, NT_DIMS, preferred_element_type=F32)
    vt = lax.dot_general(wv_ref[...], xb, NT_DIMS, preferred_element_type=F32)
    k = jnp.dot(xb, wk_ref[...], preferred_element_type=F32)
    tile = k.shape[0]
    blk_id = (pl.program_id(1) * (tile // MOBA_BLOCK)
              + lax.shift_right_logical(lax.broadcasted_iota(jnp.int32, k.shape, 0),
                                        MOBA_BLOCK.bit_length() - 1))
    pad_lane = (lax.broadcasted_iota(jnp.int32, k.shape, 1) & (K_PAD - 1)) - HEAD_DIM
    hot = pad_lane == blk_id
    for rep in range(1, PEN_SPLIT):
        hot = hot | (pad_lane == blk_id + rep * n_blocks)
    k_ref[...] = jnp.where(hot, 1.0, k).astype(BF16)
    for c in range(qt_ref.shape[0]):
        cols = slice(c * MOBA_BLOCK, (c + 1) * MOBA_BLOCK)
        qt_ref[c] = qt[:, cols].astype(BF16)
        vt_ref[c] = vt[:, cols].astype(BF16)


def _qkv_proj(x, wq_t, wk_pad, wv_t):
    b, s, d = x.shape
    nb = s // MOBA_BLOCK
    bpt = TOKEN_TILE // MOBA_BLOCK
    t_shape = jax.ShapeDtypeStruct((b, nb, ATTN_WIDTH, MOBA_BLOCK), BF16)
    t_spec = pl.BlockSpec((None, bpt, ATTN_WIDTH, MOBA_BLOCK), lambda i, t: (i, t, 0, 0))
    return pl.pallas_call(
        functools.partial(_qkv_kernel, nb),
        grid=(b, s // TOKEN_TILE),
        in_specs=[pl.BlockSpec((None, TOKEN_TILE, d), lambda i, t: (i, t, 0)),
                  _const_spec(wq_t.shape), _const_spec(wk_pad.shape), _const_spec(wv_t.shape)],
        out_specs=[t_spec,
                   pl.BlockSpec((None, TOKEN_TILE, N_HEADS * K_PAD), lambda i, t: (i, t, 0)),
                   t_spec],
        out_shape=[t_shape, jax.ShapeDtypeStruct((b, s, N_HEADS * K_PAD), BF16), t_shape],
        compiler_params=_params(2),
        name="qkv_proj",
    )(x, wq_t, wk_pad, wv_t)


def _split_bf16(x):
    parts = []
    rem = x
    for _ in range(PEN_SPLIT):
        part = rem.astype(BF16)
        parts.append(part)
        rem = rem - part.astype(F32)
    return parts


def _select_blocks(gate, i):
    nb = gate.shape[0]
    bidx = lax.broadcasted_iota(jnp.int32, gate.shape, 0)
    avail = jnp.where(bidx < i, 1, 0)
    chosen = jnp.zeros(gate.shape, jnp.int32)
    for _ in range(MOBA_TOPK):
        live = avail == 1
        top = jnp.max(jnp.where(live, gate, -jnp.inf), axis=0, keepdims=True)
        cand = jnp.where(live, jnp.where(gate == top, bidx, nb), nb)
        pick = jnp.where(bidx == jnp.min(cand, axis=0, keepdims=True), 1, 0)
        chosen = chosen + pick
        avail = avail - pick
    return chosen


def _attn_kernel(rb_ref, qt_ref, k_ref, vt_ref, bown_ref, bprev_ref, o_ref,
                 kmean_ref, qa_ref, s_ref, m_ref, acc_ref):
    i = pl.program_id(1)
    nb = vt_ref.shape[0]
    blk = MOBA_BLOCK

    @pl.when(i == 0)
    def _():
        for j in range(nb):
            kj = k_ref[j * blk:(j + 1) * blk, :].astype(F32)
            kmean_ref[j:j + 1, :] = jnp.sum(kj, axis=0, keepdims=True) * (1.0 / blk)

    def put_scores(h, j, slot):
        start = pl.multiple_of(j * blk, blk)
        s_ref[slot, h] = jnp.dot(k_ref[pl.ds(start, blk), h * K_PAD:(h + 1) * K_PAD], qa_ref[h],
                                 preferred_element_type=F32)

    bidx = lax.broadcasted_iota(jnp.int32, (nb, blk), 0)
    gates = []
    for h in range(N_HEADS):
        q = qt_ref[h * HEAD_DIM:(h + 1) * HEAD_DIM, :]
        qz = jnp.concatenate([q, jnp.zeros((K_PAD - HEAD_DIM, blk), BF16)], axis=0)
        parts = _split_bf16(kmean_ref[:, h * K_PAD:(h + 1) * K_PAD])
        g3 = jnp.dot(jnp.concatenate(parts, axis=0), qz, preferred_element_type=F32)
        gate = g3[0:nb]
        for r in range(1, PEN_SPLIT):
            gate = gate + g3[r * nb:(r + 1) * nb]
        gates.append(gate)
    for h in range(N_HEADS):
        q = qt_ref[h * HEAD_DIM:(h + 1) * HEAD_DIM, :]
        chosen = _select_blocks(gates[h], i)
        far_bias = jnp.where(bidx < i - 1, rb_ref[h, N_BUCKETS - 1] * LOG2E, 0.0)
        pen = jnp.where(bidx == i, 0.0, jnp.where(chosen == 1, far_bias, NEG))
        qa_ref[h] = jnp.concatenate(
            [q] + _split_bf16(pen)
            + [jnp.zeros((K_PAD - HEAD_DIM - PEN_SPLIT * nb, blk), BF16)], axis=0)
        put_scores(h, 0, 0)
        put_scores(h, jnp.minimum(1, i), 1)

    m_ref[...] = jnp.full(m_ref.shape, M_INIT, F32)
    acc_ref[...] = jnp.zeros(acc_ref.shape, F32)

    ones = jnp.ones((V_ONES, blk), BF16)
    causal = (lax.broadcasted_iota(jnp.int32, (blk, blk), 1)
              >= lax.broadcasted_iota(jnp.int32, (blk, blk), 0))

    def absorb(h, j, slot, kind):
        s = s_ref[slot, h]
        if kind == "own":
            s = jnp.where(causal, s + bown_ref[h], NEG)
        elif kind == "prev":
            s = s + bprev_ref[h]
        m_old = m_ref[h]
        m_new = jnp.maximum(m_old, jnp.max(s, axis=0, keepdims=True))
        p = jnp.exp2(s - m_new).astype(BF16)
        va = jnp.concatenate([vt_ref[j, h * HEAD_DIM:(h + 1) * HEAD_DIM, :], ones], axis=0)
        acc_ref[h] = jnp.exp2(m_old - m_new) * acc_ref[h] + jnp.dot(va, p, preferred_element_type=F32)
        m_ref[h] = m_new

    def stage(j, slot, kind, refill=None):
        for h in range(N_HEADS):
            absorb(h, j, slot, kind)
            if refill is not None:
                put_scores(h, refill, slot)

    def pair_body(a, carry):
        j0 = 2 * a
        stage(j0, 0, "far", refill=j0 + 2)
        stage(j0 + 1, 1, "far", refill=j0 + 3)
        return carry

    n_far = jnp.maximum(i - 1, 0)
    lax.fori_loop(0, lax.shift_right_logical(n_far, 1), pair_body, 0)

    @pl.when(i == 0)
    def _():
        stage(i, 0, "own")

    @pl.when((i & 1) == 1)
    def _():
        stage(i - 1, 0, "prev")
        stage(i, 1, "own")

    @pl.when(jnp.logical_and(i >= 2, (i & 1) == 0))
    def _():
        stage(i - 2, 0, "far", refill=i)
        stage(i - 1, 1, "prev")
        stage(i, 0, "own")

    for h in range(N_HEADS):
        acc = acc_ref[h]
        o_ref[h * HEAD_DIM:(h + 1) * HEAD_DIM, :] = (
            acc[0:HEAD_DIM] / acc[HEAD_DIM:HEAD_DIM + 1]).astype(o_ref.dtype)


def _moba_attention(rel_bias, qt, k, vt, bias_own, bias_prev):
    b, nb, _, blk = qt.shape
    s = nb * blk
    q_tile = pl.BlockSpec((None, None, ATTN_WIDTH, blk), lambda bi, i: (bi, i, 0, 0))
    return pl.pallas_call(
        _attn_kernel,
        grid=(b, nb),
        in_specs=[pl.BlockSpec(memory_space=pltpu.SMEM),
                  q_tile,
                  pl.BlockSpec((None, s, N_HEADS * K_PAD), lambda bi, i: (bi, 0, 0)),
                  pl.BlockSpec((None, nb, ATTN_WIDTH, blk), lambda bi, i: (bi, 0, 0, 0)),
                  _const_spec(bias_own.shape), _const_spec(bias_prev.shape)],
        out_specs=q_tile,
        out_shape=jax.ShapeDtypeStruct((b, nb, ATTN_WIDTH, blk), BF16),
        scratch_shapes=[pltpu.VMEM((nb, N_HEADS * K_PAD), F32),
                        pltpu.VMEM((N_HEADS, K_PAD, blk), BF16),
                        pltpu.VMEM((2, N_HEADS, blk, blk), F32),
                        pltpu.VMEM((N_HEADS, 1, blk), F32),
                        pltpu.VMEM((N_HEADS, HEAD_DIM + V_ONES, blk), F32)],
        compiler_params=_params(2),
        name="moba_attention",
    )(rel_bias, qt, k, vt, bias_own, bias_prev)


def _layer_norm(z, g, b):
    mu = jnp.mean(z, axis=-1, keepdims=True)
    zc = z - mu
    var = jnp.mean(zc * zc, axis=-1, keepdims=True)
    return zc * lax.rsqrt(var + LN_EPS) * g + b


def _mixer_kernel(alpha, x_ref, at_ref, wp_ref, wg_ref, wpool_ref, pscale_ref, wba_ref, wbp_ref,
                  wout_ref, g_ref, b_ref, h_ref, pbuf_ref):
    t = pl.program_id(1)
    tile = x_ref.shape[0]
    d = x_ref.shape[1]

    @pl.when(t == 0)
    def _():
        pbuf_ref[0:POOL_HALO, :] = jnp.zeros((POOL_HALO, POOL_WIDTH), F32)

    x = x_ref[...]
    xb = x.astype(BF16)
    p = jnp.dot(xb, wp_ref[...], preferred_element_type=F32)
    gate_logits = jnp.dot(xb, wg_ref[...], preferred_element_type=F32)
    y_attn = jnp.concatenate(
        [lax.dot_general(at_ref[c], wba_ref[...], TN_DIMS, preferred_element_type=F32)
         for c in range(at_ref.shape[0])], axis=0)

    pbuf_ref[POOL_HALO:POOL_HALO + tile, :] = p
    pos = t * tile + lax.broadcasted_iota(jnp.int32, (tile, POOL_GROUP), 0)
    ys = []
    for gi, w in enumerate(POOL_WINDOWS):
        cols = slice(gi * POOL_GROUP, (gi + 1) * POOL_GROUP)
        pg = p[:, cols]
        win = pg
        for back in range(1, w):
            win = win + pbuf_ref[POOL_HALO - back:POOL_HALO - back + tile, cols]
        cnt = jnp.minimum(pos + 1, w).astype(F32)
        diff = win / cnt - pg
        ys.append(jnp.dot(diff.astype(BF16), wpool_ref[gi], preferred_element_type=F32))
    pbuf_ref[0:POOL_HALO, :] = pbuf_ref[tile:tile + POOL_HALO, :]
    y = jnp.concatenate(ys, axis=1) * pscale_ref[...]
    y_pool = jnp.dot(y.astype(BF16), wbp_ref[...], preferred_element_type=F32)

    gates = 1.0 / (1.0 + jnp.exp(-gate_logits))
    mixed = gates[:, :d] * y_attn + gates[:, d:] * y_pool
    z = alpha * x + jnp.dot(mixed.astype(BF16), wout_ref[...], preferred_element_type=F32)
    h_ref[...] = _layer_norm(z, g_ref[...], b_ref[...])


def _mixer(alpha, x, attn_t, wp, wg, wpool, pscale, wba, wbp, wout, ln_g, ln_b):
    b, s, d = x.shape
    bpt = TOKEN_TILE // MOBA_BLOCK
    consts = (wp, wg, wpool, pscale, wba, wbp, wout, ln_g, ln_b)
    return pl.pallas_call(
        functools.partial(_mixer_kernel, alpha),
        grid=(b, s // TOKEN_TILE),
        in_specs=[pl.BlockSpec((None, TOKEN_TILE, d), lambda i, t: (i, t, 0)),
                  pl.BlockSpec((None, bpt, ATTN_WIDTH, MOBA_BLOCK), lambda i, t: (i, t, 0, 0))]
                 + [_const_spec(c.shape) for c in consts],
        out_specs=pl.BlockSpec((None, TOKEN_TILE, d), lambda i, t: (i, t, 0)),
        out_shape=jax.ShapeDtypeStruct((b, s, d), F32),
        scratch_shapes=[pltpu.VMEM((POOL_HALO + TOKEN_TILE, POOL_WIDTH), F32)],
        compiler_params=_params(2),
        name="mixer",
    )(x, attn_t, *consts)


def _ffn_kernel(alpha, tiles_per_seq, n_tiles, h_ref, win_ref, cw_ref, cb_ref, wo_ref, g_ref, b_ref,
                o_ref, carry_ref, z_ref):
    g = pl.program_id(0)
    tile = h_ref.shape[0]
    d_ff = wo_ref.shape[0]
    n_chunks = d_ff // FF_CHUNK

    @pl.when(g == 0)
    def _():
        z_ref[...] = jnp.zeros(z_ref.shape, F32)

    @pl.when(g % tiles_per_seq == 0)
    def _():
        carry_ref[...] = jnp.zeros(carry_ref.shape, F32)

    @pl.when(g == n_tiles)
    def _():
        o_ref[...] = _layer_norm(z_ref[...], g_ref[...], b_ref[...])

    @pl.when(g < n_tiles)
    def _():
        o_ref[...] = _layer_norm(z_ref[...], g_ref[...], b_ref[...])
        _ffn_tile(alpha, tile, n_chunks, d_ff, h_ref, win_ref, cw_ref, cb_ref, wo_ref, carry_ref, z_ref)


def _ffn_tile(alpha, tile, n_chunks, d_ff, h_ref, win_ref, cw_ref, cb_ref, wo_ref, carry_ref, z_ref):
    h = h_ref[...]
    hb = h.astype(BF16)
    row = lax.broadcasted_iota(jnp.int32, (CONV_HALO, FF_CHUNK), 0)

    def cols(c, base=0):
        return slice(base + c * FF_CHUNK, base + (c + 1) * FF_CHUNK)

    def project(c):
        return (jnp.dot(hb, win_ref[:, cols(c)], preferred_element_type=F32),
                jnp.dot(hb, win_ref[:, cols(c, d_ff)], preferred_element_type=F32))

    def shifted(a, prev, back):
        rolled = pltpu.roll(a, back, 0)
        top = rolled[0:CONV_HALO]
        for r in range(back):
            top = jnp.where(row == r, prev[CONV_HALO - back + r:CONV_HALO - back + r + 1, :], top)
        return jnp.concatenate([top, rolled[CONV_HALO:]], axis=0)

    f = None
    nxt = project(0)
    for c in range(n_chunks):
        a, u = nxt
        if c + 1 < n_chunks:
            nxt = project(c + 1)
        prev = carry_ref[c]
        carry_ref[c] = a[tile - CONV_HALO:tile, :]
        cw = cw_ref[:, cols(c)]
        y = cb_ref[:, cols(c)]
        for tap in range(CONV_WIDTH):
            back = CONV_WIDTH - 1 - tap
            src = a if back == 0 else shifted(a, prev, back)
            y = y + src * cw[tap:tap + 1, :]
        act = 0.5 * y * (1.0 + lax.erf(y * math.sqrt(0.5))) * u
        part = jnp.dot(act.astype(BF16), wo_ref[c * FF_CHUNK:(c + 1) * FF_CHUNK, :],
                       preferred_element_type=F32)
        f = part if f is None else f + part
    z_ref[...] = alpha * h + f


def _conv_ffn(alpha, h, win, cw, cb, wo, ln_g, ln_b):
    b, s, d = h.shape
    d_ff = wo.shape[0]
    tiles_per_seq = s // FFN_TILE
    n_tiles = b * tiles_per_seq
    consts = (win, cw, cb, wo, ln_g, ln_b)

    def tile_index(g):
        return (g // tiles_per_seq, g % tiles_per_seq, 0)

    return pl.pallas_call(
        functools.partial(_ffn_kernel, alpha, tiles_per_seq, n_tiles),
        grid=(n_tiles + 1,),
        in_specs=[pl.BlockSpec((None, FFN_TILE, d), lambda g: tile_index(jnp.minimum(g, n_tiles - 1)))]
                 + [_const_spec(c.shape) for c in consts],
        out_specs=pl.BlockSpec((None, FFN_TILE, d), lambda g: tile_index(jnp.maximum(g - 1, 0))),
        out_shape=jax.ShapeDtypeStruct((b, s, d), F32),
        scratch_shapes=[pltpu.VMEM((d_ff // FF_CHUNK, CONV_HALO, FF_CHUNK), F32),
                        pltpu.VMEM((FFN_TILE, d), F32)],
        compiler_params=_params(1),
        name="conv_ffn",
    )(h, *consts)


def kernel(x, w_in, rel_bias, w_pool_group, pool_scale, w_branch_attn, w_branch_pool, w_out, ln1_g,
           ln1_b, w_ffn_in, conv_w, conv_b, w_ffn_out, ln2_g, ln2_b):
    b, s, d = x.shape
    depth = w_in.shape[0]
    d_ff = w_ffn_out.shape[1]
    assert s % TOKEN_TILE == 0 and TOKEN_TILE % MOBA_BLOCK == 0 and d_ff % FF_CHUNK == 0
    assert s % FFN_TILE == 0
    assert w_in.shape[2] == 3 * ATTN_WIDTH + POOL_WIDTH + 2 * d
    alpha = (2.0 * depth) ** 0.25
    assert PEN_SPLIT * (s // MOBA_BLOCK) <= K_PAD - HEAD_DIM - 16
    scale = LOG2E / math.sqrt(HEAD_DIM)

    bias_own, bias_prev = _rel_bias_tables(rel_bias)
    h = x
    for l in range(depth):
        w = w_in[l]
        wq_t = (w[:, :ATTN_WIDTH] * scale).T.astype(BF16)
        wk = w[:, ATTN_WIDTH:2 * ATTN_WIDTH].reshape(d, N_HEADS, HEAD_DIM)
        wk_pad = jnp.pad(wk, ((0, 0), (0, 0), (0, K_PAD - HEAD_DIM))).reshape(d, N_HEADS * K_PAD)
        wv_t = w[:, 2 * ATTN_WIDTH:3 * ATTN_WIDTH].T.astype(BF16)
        qt, k, vt = _qkv_proj(h, wq_t, wk_pad.astype(BF16), wv_t)
        attn_t = _moba_attention(rel_bias, qt, k, vt, bias_own, bias_prev)
        h1 = _mixer(alpha, h, attn_t,
                    w[:, 3 * ATTN_WIDTH:3 * ATTN_WIDTH + POOL_WIDTH].astype(BF16),
                    w[:, 3 * ATTN_WIDTH + POOL_WIDTH:].astype(BF16),
                    w_pool_group[l].astype(BF16),
                    pool_scale[l].reshape(1, POOL_WIDTH),
                    w_branch_attn[l].astype(BF16),
                    w_branch_pool[l].astype(BF16),
                    w_out[l].astype(BF16),
                    ln1_g[l].reshape(1, d), ln1_b[l].reshape(1, d))
        h = _conv_ffn(alpha, h1, w_ffn_in[l].astype(BF16), conv_w[l], conv_b[l].reshape(1, d_ff),
                      w_ffn_out[l].astype(BF16), ln2_g[l].reshape(1, d), ln2_b[l].reshape(1, d))
    return h
```

```python
import functools
import math

import jax
import jax.numpy as jnp
from jax import lax
from jax.experimental import pallas as pl
from jax.experimental.pallas import tpu as pltpu

N_HEADS = 8
HEAD_DIM = 64
ATTN_WIDTH = N_HEADS * HEAD_DIM
MOBA_BLOCK = 256
MOBA_TOPK = 3
POOL_WINDOWS = (2, 4, 8, 16)
POOL_GROUP = 128
POOL_WIDTH = POOL_GROUP * len(POOL_WINDOWS)
N_BUCKETS = 32
MAX_DISTANCE = 128
CONV_WIDTH = 3
LN_EPS = 1e-5
NEG = -1e30
M_INIT = 0.5 * NEG
LOG2E = math.log2(math.e)
PEN_SPLIT = 3

LANES = 128
K_PAD = LANES
V_ONES = 16
POOL_HALO = max(POOL_WINDOWS)
CONV_HALO = 8
FF_CHUNK = 256
TOKEN_TILE = 512
FFN_TILE = 512
VMEM_LIMIT = 56 * 1024 * 1024

F32 = jnp.float32
BF16 = jnp.bfloat16
NT_DIMS = (((1,), (1,)), ((), ()))
TN_DIMS = (((0,), (0,)), ((), ()))


def _params(n_grid):
    return pltpu.CompilerParams(dimension_semantics=("arbitrary",) * n_grid,
                                vmem_limit_bytes=VMEM_LIMIT)


def _const_spec(shape):
    nd = len(shape)
    return pl.BlockSpec(shape, lambda *_: (0,) * nd)


def _rel_bias_kernel(rb_ref, own_ref, prev_ref):
    h = pl.program_id(0)
    kk = lax.broadcasted_iota(jnp.int32, (MOBA_BLOCK, MOBA_BLOCK), 0)
    qq = lax.broadcasted_iota(jnp.int32, (MOBA_BLOCK, MOBA_BLOCK), 1)
    max_exact = N_BUCKETS // 2

    def table(dist):
        n = jnp.maximum(dist, 0)
        nf = jnp.maximum(n, 1).astype(F32)
        large = max_exact + (jnp.log(nf / max_exact) / math.log(MAX_DISTANCE / max_exact)
                             * (N_BUCKETS - max_exact)).astype(jnp.int32)
        large = jnp.minimum(large, N_BUCKETS - 1)
        bucket = jnp.where(n < max_exact, n, large)
        out = jnp.zeros(dist.shape, F32)
        for b in range(N_BUCKETS):
            out = jnp.where(bucket == b, rb_ref[h, b], out)
        return out

    own_ref[...] = table(qq - kk) * LOG2E
    prev_ref[...] = table(qq - kk + MOBA_BLOCK) * LOG2E


def _rel_bias_tables(rel_bias):
    shape = jax.ShapeDtypeStruct((N_HEADS, MOBA_BLOCK, MOBA_BLOCK), F32)
    spec = pl.BlockSpec((None, MOBA_BLOCK, MOBA_BLOCK), lambda h: (h, 0, 0))
    return pl.pallas_call(
        _rel_bias_kernel,
        grid=(N_HEADS,),
        in_specs=[pl.BlockSpec(memory_space=pltpu.SMEM)],
        out_specs=[spec, spec],
        out_shape=[shape, shape],
        compiler_params=_params(1),
        name="rel_bias_tables",
    )(rel_bias)


def _qkv_kernel(n_blocks, x_ref, wq_ref, wk_ref, wv_ref, qt_ref, k_ref, vt_ref):
    xb = x_ref[...].astype(BF16)
    qt = lax.dot_general(wq_ref[...], xb, NT_DIMS, preferred_element_type=F32)
    vt = lax.dot_general(wv_ref[...], xb, NT_DIMS, preferred_element_type=F32)
    k = jnp.dot(xb, wk_ref[...], preferred_element_type=F32)
    tile = k.shape[0]
    blk_id = (pl.program_id(1) * (tile // MOBA_BLOCK)
              + lax.shift_right_logical(lax.broadcasted_iota(jnp.int32, k.shape, 0),
                                        MOBA_BLOCK.bit_length() - 1))
    pad_lane = (lax.broadcasted_iota(jnp.int32, k.shape, 1) & (K_PAD - 1)) - HEAD_DIM
    hot = pad_lane == blk_id
    for rep in range(1, PEN_SPLIT):
        hot = hot | (pad_lane == blk_id + rep * n_blocks)
    k_ref[...] = jnp.where(hot, 1.0, k).astype(BF16)
    for c in range(qt_ref.shape[0]):
        cols = slice(c * MOBA_BLOCK, (c + 1) * MOBA_BLOCK)
        qt_ref[c] = qt[:, cols].astype(BF16)
        vt_ref[c] = vt[:, cols].astype(BF16)


def _qkv_proj(x, wq_t, wk_pad, wv_t):
    b, s, d = x.shape
    nb = s // MOBA_BLOCK
    bpt = TOKEN_TILE // MOBA_BLOCK
    t_shape = jax.ShapeDtypeStruct((b, nb, ATTN_WIDTH, MOBA_BLOCK), BF16)
    t_spec = pl.BlockSpec((None, bpt, ATTN_WIDTH, MOBA_BLOCK), lambda i, t: (i, t, 0, 0))
    return pl.pallas_call(
        functools.partial(_qkv_kernel, nb),
        grid=(b, s // TOKEN_TILE),
        in_specs=[pl.BlockSpec((None, TOKEN_TILE, d), lambda i, t: (i, t, 0)),
                  _const_spec(wq_t.shape), _const_spec(wk_pad.shape), _const_spec(wv_t.shape)],
        out_specs=[t_spec,
                   pl.BlockSpec((None, TOKEN_TILE, N_HEADS * K_PAD), lambda i, t: (i, t, 0)),
                   t_spec],
        out_shape=[t_shape, jax.ShapeDtypeStruct((b, s, N_HEADS * K_PAD), BF16), t_shape],
        compiler_params=_params(2),
        name="qkv_proj",
    )(x, wq_t, wk_pad, wv_t)


def _split_bf16(x):
    parts = []
    rem = x
    for _ in range(PEN_SPLIT):
        part = rem.astype(BF16)
        parts.append(part)
        rem = rem - part.astype(F32)
    return parts


def _select_blocks(gate, i):
    nb = gate.shape[0]
    bidx = lax.broadcasted_iota(jnp.int32, gate.shape, 0)
    avail = jnp.where(bidx < i, 1, 0)
    chosen = jnp.zeros(gate.shape, jnp.int32)
    for _ in range(MOBA_TOPK):
        live = avail == 1
        top = jnp.max(jnp.where(live, gate, -jnp.inf), axis=0, keepdims=True)
        cand = jnp.where(live, jnp.where(gate == top, bidx, nb), nb)
        pick = jnp.where(bidx == jnp.min(cand, axis=0, keepdims=True), 1, 0)
        chosen = chosen + pick
        avail = avail - pick
    return chosen


def _attn_kernel(rb_ref, qt_ref, k_ref, vt_ref, bown_ref, bprev_ref, o_ref,
                 kmean_ref, qa_ref, s_ref, m_ref, acc_ref):
    i = pl.program_id(1)
    nb = vt_ref.shape[0]
    blk = MOBA_BLOCK

    @pl.when(i == 0)
    def _():
        for j in range(nb):
            kj = k_ref[j * blk:(j + 1) * blk, :].astype(F32)
            kmean_ref[j:j + 1, :] = jnp.sum(kj, axis=0, keepdims=True) * (1.0 / blk)

    def put_scores(h, j, slot):
        start = pl.multiple_of(j * blk, blk)
        s_ref[slot, h] = jnp.dot(k_ref[pl.ds(start, blk), h * K_PAD:(h + 1) * K_PAD], qa_ref[h],
                                 preferred_element_type=F32)

    bidx = lax.broadcasted_iota(jnp.int32, (nb, blk), 0)
    gates = []
    for h in range(N_HEADS):
        q = qt_ref[h * HEAD_DIM:(h + 1) * HEAD_DIM, :]
        qz = jnp.concatenate([q, jnp.zeros((K_PAD - HEAD_DIM, blk), BF16)], axis=0)
        parts = _split_bf16(kmean_ref[:, h * K_PAD:(h + 1) * K_PAD])
        g3 = jnp.dot(jnp.concatenate(parts, axis=0), qz, preferred_element_type=F32)
        gate = g3[0:nb]
        for r in range(1, PEN_SPLIT):
            gate = gate + g3[r * nb:(r + 1) * nb]
        gates.append(gate)
    for h in range(N_HEADS):
        q = qt_ref[h * HEAD_DIM:(h + 1) * HEAD_DIM, :]
        chosen = _select_blocks(gates[h], i)
        far_bias = jnp.where(bidx < i - 1, rb_ref[h, N_BUCKETS - 1] * LOG2E, 0.0)
        pen = jnp.where(bidx == i, 0.0, jnp.where(chosen == 1, far_bias, NEG))
        qa_ref[h] = jnp.concatenate(
            [q] + _split_bf16(pen)
            + [jnp.zeros((K_PAD - HEAD_DIM - PEN_SPLIT * nb, blk), BF16)], axis=0)
        put_scores(h, 0, 0)
        put_scores(h, jnp.minimum(1, i), 1)

    m_ref[...] = jnp.full(m_ref.shape, M_INIT, F32)
    acc_ref[...] = jnp.zeros(acc_ref.shape, F32)

    ones = jnp.ones((V_ONES, blk), BF16)
    causal = (lax.broadcasted_iota(jnp.int32, (blk, blk), 1)
              >= lax.broadcasted_iota(jnp.int32, (blk, blk), 0))

    def absorb(h, j, slot, kind):
        s = s_ref[slot, h]
        if kind == "own":
            s = jnp.where(causal, s + bown_ref[h], NEG)
        elif kind == "prev":
            s = s + bprev_ref[h]
        m_old = m_ref[h]
        m_new = jnp.maximum(m_old, jnp.max(s, axis=0, keepdims=True))
        p = jnp.exp2(s - m_new).astype(BF16)
        va = jnp.concatenate([vt_ref[j, h * HEAD_DIM:(h + 1) * HEAD_DIM, :], ones], axis=0)
        acc_ref[h] = jnp.exp2(m_old - m_new) * acc_ref[h] + jnp.dot(va, p, preferred_element_type=F32)
        m_ref[h] = m_new

    def stage(j, slot, kind, refill=None):
        for h in range(N_HEADS):
            absorb(h, j, slot, kind)
            if refill is not None:
                put_scores(h, refill, slot)

    def pair_body(a, carry):
        j0 = 2 * a
        stage(j0, 0, "far", refill=j0 + 2)
        stage(j0 + 1, 1, "far", refill=j0 + 3)
        return carry

    n_far = jnp.maximum(i - 1, 0)
    lax.fori_loop(0, lax.shift_right_logical(n_far, 1), pair_body, 0)

    @pl.when(i == 0)
    def _():
        stage(i, 0, "own")

    @pl.when((i & 1) == 1)
    def _():
        stage(i - 1, 0, "prev")
        stage(i, 1, "own")

    @pl.when(jnp.logical_and(i >= 2, (i & 1) == 0))
    def _():
        stage(i - 2, 0, "far", refill=i)
        stage(i - 1, 1, "prev")
        stage(i, 0, "own")

    for h in range(N_HEADS):
        acc = acc_ref[h]
        o_ref[h * HEAD_DIM:(h + 1) * HEAD_DIM, :] = (
            acc[0:HEAD_DIM] / acc[HEAD_DIM:HEAD_DIM + 1]).astype(o_ref.dtype)


def _moba_attention(rel_bias, qt, k, vt, bias_own, bias_prev):
    b, nb, _, blk = qt.shape
    s = nb * blk
    q_tile = pl.BlockSpec((None, None, ATTN_WIDTH, blk), lambda bi, i: (bi, i, 0, 0))
    return pl.pallas_call(
        _attn_kernel,
        grid=(b, nb),
        in_specs=[pl.BlockSpec(memory_space=pltpu.SMEM),
                  q_tile,
                  pl.BlockSpec((None, s, N_HEADS * K_PAD), lambda bi, i: (bi, 0, 0)),
                  pl.BlockSpec((None, nb, ATTN_WIDTH, blk), lambda bi, i: (bi, 0, 0, 0)),
                  _const_spec(bias_own.shape), _const_spec(bias_prev.shape)],
        out_specs=q_tile,
        out_shape=jax.ShapeDtypeStruct((b, nb, ATTN_WIDTH, blk), BF16),
        scratch_shapes=[pltpu.VMEM((nb, N_HEADS * K_PAD), F32),
                        pltpu.VMEM((N_HEADS, K_PAD, blk), BF16),
                        pltpu.VMEM((2, N_HEADS, blk, blk), F32),
                        pltpu.VMEM((N_HEADS, 1, blk), F32),
                        pltpu.VMEM((N_HEADS, HEAD_DIM + V_ONES, blk), F32)],
        compiler_params=_params(2),
        name="moba_attention",
    )(rel_bias, qt, k, vt, bias_own, bias_prev)


def _layer_norm(z, g, b):
    mu = jnp.mean(z, axis=-1, keepdims=True)
    zc = z - mu
    var = jnp.mean(zc * zc, axis=-1, keepdims=True)
    return zc * lax.rsqrt(var + LN_EPS) * g + b


def _mixer_kernel(alpha, x_ref, at_ref, wp_ref, wg_ref, wpool_ref, pscale_ref, wba_ref, wbp_ref,
                  wout_ref, g_ref, b_ref, h_ref, pbuf_ref):
    t = pl.program_id(1)
    tile = x_ref.shape[0]
    d = x_ref.shape[1]

    @pl.when(t == 0)
    def _():
        pbuf_ref[0:POOL_HALO, :] = jnp.zeros((POOL_HALO, POOL_WIDTH), F32)

    x = x_ref[...]
    xb = x.astype(BF16)
    p = jnp.dot(xb, wp_ref[...], preferred_element_type=F32)
    gate_logits = jnp.dot(xb, wg_ref[...], preferred_element_type=F32)
    y_attn = jnp.concatenate(
        [lax.dot_general(at_ref[c], wba_ref[...], TN_DIMS, preferred_element_type=F32)
         for c in range(at_ref.shape[0])], axis=0)

    pbuf_ref[POOL_HALO:POOL_HALO + tile, :] = p
    pos = t * tile + lax.broadcasted_iota(jnp.int32, (tile, POOL_GROUP), 0)
    ys = []
    for gi, w in enumerate(POOL_WINDOWS):
        cols = slice(gi * POOL_GROUP, (gi + 1) * POOL_GROUP)
        pg = p[:, cols]
        win = pg
        for back in range(1, w):
            win = win + pbuf_ref[POOL_HALO - back:POOL_HALO - back + tile, cols]
        cnt = jnp.minimum(pos + 1, w).astype(F32)
        diff = win / cnt - pg
        ys.append(jnp.dot(diff.astype(BF16), wpool_ref[gi], preferred_element_type=F32))
    pbuf_ref[0:POOL_HALO, :] = pbuf_ref[tile:tile + POOL_HALO, :]
    y = jnp.concatenate(ys, axis=1) * pscale_ref[...]
    y_pool = jnp.dot(y.astype(BF16), wbp_ref[...], preferred_element_type=F32)

    gates = 1.0 / (1.0 + jnp.exp(-gate_logits))
    mixed = gates[:, :d] * y_attn + gates[:, d:] * y_pool
    z = alpha * x + jnp.dot(mixed.astype(BF16), wout_ref[...], preferred_element_type=F32)
    h_ref[...] = _layer_norm(z, g_ref[...], b_ref[...])


def _mixer(alpha, x, attn_t, wp, wg, wpool, pscale, wba, wbp, wout, ln_g, ln_b):
    b, s, d = x.shape
    bpt = TOKEN_TILE // MOBA_BLOCK
    consts = (wp, wg, wpool, pscale, wba, wbp, wout, ln_g, ln_b)
    return pl.pallas_call(
        functools.partial(_mixer_kernel, alpha),
        grid=(b, s // TOKEN_TILE),
        in_specs=[pl.BlockSpec((None, TOKEN_TILE, d), lambda i, t: (i, t, 0)),
                  pl.BlockSpec((None, bpt, ATTN_WIDTH, MOBA_BLOCK), lambda i, t: (i, t, 0, 0))]
                 + [_const_spec(c.shape) for c in consts],
        out_specs=pl.BlockSpec((None, TOKEN_TILE, d), lambda i, t: (i, t, 0)),
        out_shape=jax.ShapeDtypeStruct((b, s, d), F32),
        scratch_shapes=[pltpu.VMEM((POOL_HALO + TOKEN_TILE, POOL_WIDTH), F32)],
        compiler_params=_params(2),
        name="mixer",
    )(x, attn_t, *consts)


def _ffn_kernel(alpha, tiles_per_seq, n_tiles, h_ref, win_ref, cw_ref, cb_ref, wo_ref, g_ref, b_ref,
                o_ref, carry_ref, z_ref, act_ref):
    g = pl.program_id(0)
    tile = h_ref.shape[0]
    d_ff = wo_ref.shape[0]
    n_chunks = d_ff // FF_CHUNK

    @pl.when(g == 0)
    def _():
        z_ref[...] = jnp.zeros(z_ref.shape, F32)

    @pl.when(g % tiles_per_seq == 0)
    def _():
        carry_ref[...] = jnp.zeros(carry_ref.shape, F32)

    @pl.when(g == n_tiles)
    def _():
        o_ref[...] = _layer_norm(z_ref[...], g_ref[...], b_ref[...])

    @pl.when(g < n_tiles)
    def _():
        o_ref[...] = _layer_norm(z_ref[...], g_ref[...], b_ref[...])
        _ffn_tile(alpha, tile, n_chunks, d_ff, h_ref, win_ref, cw_ref, cb_ref, wo_ref, carry_ref, z_ref, act_ref)


def _ffn_tile(alpha, tile, n_chunks, d_ff, h_ref, win_ref, cw_ref, cb_ref, wo_ref, carry_ref, z_ref, act_ref):
    h = h_ref[...]
    hb = h.astype(BF16)
    row = lax.broadcasted_iota(jnp.int32, (CONV_HALO, FF_CHUNK), 0)

    def cols(c, base=0):
        return slice(base + c * FF_CHUNK, base + (c + 1) * FF_CHUNK)

    def project(c):
        return (jnp.dot(hb, win_ref[:, cols(c)], preferred_element_type=F32),
                jnp.dot(hb, win_ref[:, cols(c, d_ff)], preferred_element_type=F32))

    def shifted(a, prev, back):
        rolled = pltpu.roll(a, back, 0)
        top = rolled[0:CONV_HALO]
        for r in range(back):
            top = jnp.where(row == r, prev[CONV_HALO - back + r:CONV_HALO - back + r + 1, :], top)
        return jnp.concatenate([top, rolled[CONV_HALO:]], axis=0)

    nxt = project(0)
    for c in range(n_chunks):
        a, u = nxt
        if c + 1 < n_chunks:
            nxt = project(c + 1)
        prev = carry_ref[c]
        carry_ref[c] = a[tile - CONV_HALO:tile, :]
        cw = cw_ref[:, cols(c)]
        y = cb_ref[:, cols(c)]
        for tap in range(CONV_WIDTH):
            back = CONV_WIDTH - 1 - tap
            src = a if back == 0 else shifted(a, prev, back)
            y = y + src * cw[tap:tap + 1, :]
        act = 0.5 * y * (1.0 + lax.erf(y * math.sqrt(0.5))) * u
        act_ref[:, cols(c)] = act.astype(BF16)
    split = (n_chunks - 1) * FF_CHUNK
    f = (jnp.dot(act_ref[:, :split], wo_ref[:split, :], preferred_element_type=F32)
         + jnp.dot(act_ref[:, split:], wo_ref[split:, :], preferred_element_type=F32))
    z_ref[...] = alpha * h + f


def _conv_ffn(alpha, h, win, cw, cb, wo, ln_g, ln_b):
    b, s, d = h.shape
    d_ff = wo.shape[0]
    tiles_per_seq = s // FFN_TILE
    n_tiles = b * tiles_per_seq
    consts = (win, cw, cb, wo, ln_g, ln_b)

    def tile_index(g):
        return (g // tiles_per_seq, g % tiles_per_seq, 0)

    return pl.pallas_call(
        functools.partial(_ffn_kernel, alpha, tiles_per_seq, n_tiles),
        grid=(n_tiles + 1,),
        in_specs=[pl.BlockSpec((None, FFN_TILE, d), lambda g: tile_index(jnp.minimum(g, n_tiles - 1)))]
                 + [_const_spec(c.shape) for c in consts],
        out_specs=pl.BlockSpec((None, FFN_TILE, d), lambda g: tile_index(jnp.maximum(g - 1, 0))),
        out_shape=jax.ShapeDtypeStruct((b, s, d), F32),
        scratch_shapes=[pltpu.VMEM((d_ff // FF_CHUNK, CONV_HALO, FF_CHUNK), F32),
                        pltpu.VMEM((FFN_TILE, d), F32),
                        pltpu.VMEM((FFN_TILE, d_ff), BF16)],
        compiler_params=_params(1),
        name="conv_ffn",
    )(h, *consts)


def kernel(x, w_in, rel_bias, w_pool_group, pool_scale, w_branch_attn, w_branch_pool, w_out, ln1_g,
           ln1_b, w_ffn_in, conv_w, conv_b, w_ffn_out, ln2_g, ln2_b):
    b, s, d = x.shape
    depth = w_in.shape[0]
    d_ff = w_ffn_out.shape[1]
    assert s % TOKEN_TILE == 0 and TOKEN_TILE % MOBA_BLOCK == 0 and d_ff % FF_CHUNK == 0
    assert s % FFN_TILE == 0
    assert w_in.shape[2] == 3 * ATTN_WIDTH + POOL_WIDTH + 2 * d
    alpha = (2.0 * depth) ** 0.25
    assert PEN_SPLIT * (s // MOBA_BLOCK) <= K_PAD - HEAD_DIM - 16
    scale = LOG2E / math.sqrt(HEAD_DIM)

    bias_own, bias_prev = _rel_bias_tables(rel_bias)
    h = x
    for l in range(depth):
        w = w_in[l]
        wq_t = (w[:, :ATTN_WIDTH] * scale).T.astype(BF16)
        wk = w[:, ATTN_WIDTH:2 * ATTN_WIDTH].reshape(d, N_HEADS, HEAD_DIM)
        wk_pad = jnp.pad(wk, ((0, 0), (0, 0), (0, K_PAD - HEAD_DIM))).reshape(d, N_HEADS * K_PAD)
        wv_t = w[:, 2 * ATTN_WIDTH:3 * ATTN_WIDTH].T.astype(BF16)
        qt, k, vt = _qkv_proj(h, wq_t, wk_pad.astype(BF16), wv_t)
        attn_t = _moba_attention(rel_bias, qt, k, vt, bias_own, bias_prev)
        h1 = _mixer(alpha, h, attn_t,
                    w[:, 3 * ATTN_WIDTH:3 * ATTN_WIDTH + POOL_WIDTH].astype(BF16),
                    w[:, 3 * ATTN_WIDTH + POOL_WIDTH:].astype(BF16),
                    w_pool_group[l].astype(BF16),
                    pool_scale[l].reshape(1, POOL_WIDTH),
                    w_branch_attn[l].astype(BF16),
                    w_branch_pool[l].astype(BF16),
                    w_out[l].astype(BF16),
                    ln1_g[l].reshape(1, d), ln1_b[l].reshape(1, d))
        h = _conv_ffn(alpha, h1, w_ffn_in[l].astype(BF16), conv_w[l], conv_b[l].reshape(1, d_ff),
                      w_ffn_out[l].astype(BF16), ln2_g[l].reshape(1, d), ln2_b[l].reshape(1, d))
    return h
```

```python
import functools
import math

import jax
import jax.numpy as jnp
from jax import lax
from jax.experimental import pallas as pl
from jax.experimental.pallas import tpu as pltpu

N_HEADS = 8
HEAD_DIM = 64
ATTN_WIDTH = N_HEADS * HEAD_DIM
MOBA_BLOCK = 256
MOBA_TOPK = 3
POOL_WINDOWS = (2, 4, 8, 16)
POOL_GROUP = 128
POOL_WIDTH = POOL_GROUP * len(POOL_WINDOWS)
N_BUCKETS = 32
MAX_DISTANCE = 128
CONV_WIDTH = 3
LN_EPS = 1e-5
NEG = -1e30
M_INIT = 0.5 * NEG
LOG2E = math.log2(math.e)
FILL_AHEAD = 2
PEN_SPLIT = 3

LANES = 128
K_PAD = LANES
V_ONES = 16
POOL_HALO = max(POOL_WINDOWS)
CONV_HALO = 8
FF_CHUNK = 256
TOKEN_TILE = 512
FFN_TILE = 512
VMEM_LIMIT = 56 * 1024 * 1024

F32 = jnp.float32
BF16 = jnp.bfloat16
NT_DIMS = (((1,), (1,)), ((), ()))
TN_DIMS = (((0,), (0,)), ((), ()))


def _params(n_grid):
    return pltpu.CompilerParams(dimension_semantics=("arbitrary",) * n_grid,
                                vmem_limit_bytes=VMEM_LIMIT)


def _const_spec(shape):
    nd = len(shape)
    return pl.BlockSpec(shape, lambda *_: (0,) * nd)


def _rel_bias_kernel(rb_ref, own_ref, prev_ref):
    h = pl.program_id(0)
    kk = lax.broadcasted_iota(jnp.int32, (MOBA_BLOCK, MOBA_BLOCK), 0)
    qq = lax.broadcasted_iota(jnp.int32, (MOBA_BLOCK, MOBA_BLOCK), 1)
    max_exact = N_BUCKETS // 2

    def table(dist):
        n = jnp.maximum(dist, 0)
        nf = jnp.maximum(n, 1).astype(F32)
        large = max_exact + (jnp.log(nf / max_exact) / math.log(MAX_DISTANCE / max_exact)
                             * (N_BUCKETS - max_exact)).astype(jnp.int32)
        large = jnp.minimum(large, N_BUCKETS - 1)
        bucket = jnp.where(n < max_exact, n, large)
        out = jnp.zeros(dist.shape, F32)
        for b in range(N_BUCKETS):
            out = jnp.where(bucket == b, rb_ref[h, b], out)
        return out

    own_ref[...] = table(qq - kk) * LOG2E
    prev_ref[...] = table(qq - kk + MOBA_BLOCK) * LOG2E


def _rel_bias_tables(rel_bias):
    shape = jax.ShapeDtypeStruct((N_HEADS, MOBA_BLOCK, MOBA_BLOCK), F32)
    spec = pl.BlockSpec((None, MOBA_BLOCK, MOBA_BLOCK), lambda h: (h, 0, 0))
    return pl.pallas_call(
        _rel_bias_kernel,
        grid=(N_HEADS,),
        in_specs=[pl.BlockSpec(memory_space=pltpu.SMEM)],
        out_specs=[spec, spec],
        out_shape=[shape, shape],
        compiler_params=_params(1),
        name="rel_bias_tables",
    )(rel_bias)


def _qkv_kernel(n_blocks, x_ref, wq_ref, wk_ref, wv_ref, qt_ref, k_ref, vt_ref):
    xb = x_ref[...].astype(BF16)
    qt = lax.dot_general(wq_ref[...], xb, NT_DIMS, preferred_element_type=F32)
    vt = lax.dot_general(wv_ref[...], xb, NT_DIMS, preferred_element_type=F32)
    k = jnp.dot(xb, wk_ref[...], preferred_element_type=F32)
    tile = k.shape[0]
    blk_id = (pl.program_id(1) * (tile // MOBA_BLOCK)
              + lax.shift_right_logical(lax.broadcasted_iota(jnp.int32, k.shape, 0),
                                        MOBA_BLOCK.bit_length() - 1))
    pad_lane = (lax.broadcasted_iota(jnp.int32, k.shape, 1) & (K_PAD - 1)) - HEAD_DIM
    hot = pad_lane == blk_id
    for rep in range(1, PEN_SPLIT):
        hot = hot | (pad_lane == blk_id + rep * n_blocks)
    k_ref[...] = jnp.where(hot, 1.0, k).astype(BF16)
    for c in range(qt_ref.shape[0]):
        cols = slice(c * MOBA_BLOCK, (c + 1) * MOBA_BLOCK)
        qt_ref[c] = qt[:, cols].astype(BF16)
        vt_ref[c] = vt[:, cols].astype(BF16)


def _qkv_proj(x, wq_t, wk_pad, wv_t):
    b, s, d = x.shape
    nb = s // MOBA_BLOCK
    bpt = TOKEN_TILE // MOBA_BLOCK
    t_shape = jax.ShapeDtypeStruct((b, nb, ATTN_WIDTH, MOBA_BLOCK), BF16)
    t_spec = pl.BlockSpec((None, bpt, ATTN_WIDTH, MOBA_BLOCK), lambda i, t: (i, t, 0, 0))
    return pl.pallas_call(
        functools.partial(_qkv_kernel, nb),
        grid=(b, s // TOKEN_TILE),
        in_specs=[pl.BlockSpec((None, TOKEN_TILE, d), lambda i, t: (i, t, 0)),
                  _const_spec(wq_t.shape), _const_spec(wk_pad.shape), _const_spec(wv_t.shape)],
        out_specs=[t_spec,
                   pl.BlockSpec((None, TOKEN_TILE, N_HEADS * K_PAD), lambda i, t: (i, t, 0)),
                   t_spec],
        out_shape=[t_shape, jax.ShapeDtypeStruct((b, s, N_HEADS * K_PAD), BF16), t_shape],
        compiler_params=_params(2),
        name="qkv_proj",
    )(x, wq_t, wk_pad, wv_t)


def _split_bf16(x):
    parts = []
    rem = x
    for _ in range(PEN_SPLIT):
        part = rem.astype(BF16)
        parts.append(part)
        rem = rem - part.astype(F32)
    return parts


def _select_blocks(gate, i):
    nb = gate.shape[0]
    bidx = lax.broadcasted_iota(jnp.int32, gate.shape, 0)
    avail = jnp.where(bidx < i, 1, 0)
    chosen = jnp.zeros(gate.shape, jnp.int32)
    for _ in range(MOBA_TOPK):
        live = avail == 1
        top = jnp.max(jnp.where(live, gate, -jnp.inf), axis=0, keepdims=True)
        cand = jnp.where(live, jnp.where(gate == top, bidx, nb), nb)
        pick = jnp.where(bidx == jnp.min(cand, axis=0, keepdims=True), 1, 0)
        chosen = chosen + pick
        avail = avail - pick
    return chosen


def _attn_kernel(rb_ref, qt_ref, k_ref, vt_ref, bown_ref, bprev_ref, o_ref,
                 kmean_ref, qa_ref, s_ref, m_ref, acc_ref):
    i = pl.program_id(1)
    nb = vt_ref.shape[0]
    blk = MOBA_BLOCK

    @pl.when(i == 0)
    def _():
        for j in range(nb):
            kj = k_ref[j * blk:(j + 1) * blk, :].astype(F32)
            kmean_ref[j:j + 1, :] = jnp.sum(kj, axis=0, keepdims=True) * (1.0 / blk)

    def put_scores(h, j, slot):
        start = pl.multiple_of(j * blk, blk)
        s_ref[slot, h] = jnp.dot(k_ref[pl.ds(start, blk), h * K_PAD:(h + 1) * K_PAD], qa_ref[h],
                                 preferred_element_type=F32)

    bidx = lax.broadcasted_iota(jnp.int32, (nb, blk), 0)
    gates = []
    for h in range(N_HEADS):
        q = qt_ref[h * HEAD_DIM:(h + 1) * HEAD_DIM, :]
        qz = jnp.concatenate([q, jnp.zeros((K_PAD - HEAD_DIM, blk), BF16)], axis=0)
        parts = _split_bf16(kmean_ref[:, h * K_PAD:(h + 1) * K_PAD])
        g3 = jnp.dot(jnp.concatenate(parts, axis=0), qz, preferred_element_type=F32)
        gate = g3[0:nb]
        for r in range(1, PEN_SPLIT):
            gate = gate + g3[r * nb:(r + 1) * nb]
        gates.append(gate)
    for h in range(N_HEADS):
        q = qt_ref[h * HEAD_DIM:(h + 1) * HEAD_DIM, :]
        chosen = _select_blocks(gates[h], i)
        far_bias = jnp.where(bidx < i - 1, rb_ref[h, N_BUCKETS - 1] * LOG2E, 0.0)
        pen = jnp.where(bidx == i, 0.0, jnp.where(chosen == 1, far_bias, NEG))
        qa_ref[h] = jnp.concatenate(
            [q] + _split_bf16(pen)
            + [jnp.zeros((K_PAD - HEAD_DIM - PEN_SPLIT * nb, blk), BF16)], axis=0)
        put_scores(h, 0, 0)
        put_scores(h, jnp.minimum(1, i), 1)

    m_ref[...] = jnp.full(m_ref.shape, M_INIT, F32)
    acc_ref[...] = jnp.zeros(acc_ref.shape, F32)

    ones = jnp.ones((V_ONES, blk), BF16)
    causal = (lax.broadcasted_iota(jnp.int32, (blk, blk), 1)
              >= lax.broadcasted_iota(jnp.int32, (blk, blk), 0))

    def absorb(h, j, slot, kind):
        s = s_ref[slot, h]
        if kind == "own":
            s = jnp.where(causal, s + bown_ref[h], NEG)
        elif kind == "prev":
            s = s + bprev_ref[h]
        m_old = m_ref[h]
        m_new = jnp.maximum(m_old, jnp.max(s, axis=0, keepdims=True))
        p = jnp.exp2(s - m_new).astype(BF16)
        va = jnp.concatenate([vt_ref[j, h * HEAD_DIM:(h + 1) * HEAD_DIM, :], ones], axis=0)
        acc_ref[h] = jnp.exp2(m_old - m_new) * acc_ref[h] + jnp.dot(va, p, preferred_element_type=F32)
        m_ref[h] = m_new

    def stage(j, slot, kind, fill=None):
        if fill is not None:
            for h in range(FILL_AHEAD):
                put_scores(h, fill[0], fill[1])
        for h in range(N_HEADS):
            if fill is not None and h + FILL_AHEAD < N_HEADS:
                put_scores(h + FILL_AHEAD, fill[0], fill[1])
            absorb(h, j, slot, kind)

    def triple_body(a, carry):
        j0 = 3 * a
        stage(j0, 0, "far", fill=(j0 + 2, 2))
        stage(j0 + 1, 1, "far", fill=(j0 + 3, 0))
        stage(j0 + 2, 2, "far", fill=(j0 + 4, 1))
        return carry

    n_far = jnp.maximum(i - 1, 0)
    n_triples = lax.div(n_far, 3)
    rest = n_far - 3 * n_triples
    lax.fori_loop(0, n_triples, triple_body, 0)

    @pl.when(i == 0)
    def _():
        stage(i, 0, "own")

    @pl.when(jnp.logical_and(i >= 1, rest == 0))
    def _():
        stage(i - 1, 0, "prev")
        stage(i, 1, "own")

    @pl.when(rest == 1)
    def _():
        stage(i - 2, 0, "far", fill=(i, 2))
        stage(i - 1, 1, "prev")
        stage(i, 2, "own")

    @pl.when(rest == 2)
    def _():
        stage(i - 3, 0, "far", fill=(i - 1, 2))
        stage(i - 2, 1, "far", fill=(i, 0))
        stage(i - 1, 2, "prev")
        stage(i, 0, "own")

    for h in range(N_HEADS):
        acc = acc_ref[h]
        o_ref[h * HEAD_DIM:(h + 1) * HEAD_DIM, :] = (
            acc[0:HEAD_DIM] / acc[HEAD_DIM:HEAD_DIM + 1]).astype(o_ref.dtype)


def _moba_attention(rel_bias, qt, k, vt, bias_own, bias_prev):
    b, nb, _, blk = qt.shape
    s = nb * blk
    q_tile = pl.BlockSpec((None, None, ATTN_WIDTH, blk), lambda bi, i: (bi, i, 0, 0))
    return pl.pallas_call(
        _attn_kernel,
        grid=(b, nb),
        in_specs=[pl.BlockSpec(memory_space=pltpu.SMEM),
                  q_tile,
                  pl.BlockSpec((None, s, N_HEADS * K_PAD), lambda bi, i: (bi, 0, 0)),
                  pl.BlockSpec((None, nb, ATTN_WIDTH, blk), lambda bi, i: (bi, 0, 0, 0)),
                  _const_spec(bias_own.shape), _const_spec(bias_prev.shape)],
        out_specs=q_tile,
        out_shape=jax.ShapeDtypeStruct((b, nb, ATTN_WIDTH, blk), BF16),
        scratch_shapes=[pltpu.VMEM((nb, N_HEADS * K_PAD), F32),
                        pltpu.VMEM((N_HEADS, K_PAD, blk), BF16),
                        pltpu.VMEM((3, N_HEADS, blk, blk), F32),
                        pltpu.VMEM((N_HEADS, 1, blk), F32),
                        pltpu.VMEM((N_HEADS, HEAD_DIM + V_ONES, blk), F32)],
        compiler_params=_params(2),
        name="moba_attention",
    )(rel_bias, qt, k, vt, bias_own, bias_prev)


def _layer_norm(z, g, b):
    mu = jnp.mean(z, axis=-1, keepdims=True)
    zc = z - mu
    var = jnp.mean(zc * zc, axis=-1, keepdims=True)
    return zc * lax.rsqrt(var + LN_EPS) * g + b


def _mixer_kernel(alpha, x_ref, at_ref, wp_ref, wg_ref, wpool_ref, pscale_ref, wba_ref, wbp_ref,
                  wout_ref, g_ref, b_ref, h_ref, pbuf_ref):
    t = pl.program_id(1)
    tile = x_ref.shape[0]
    d = x_ref.shape[1]

    @pl.when(t == 0)
    def _():
        pbuf_ref[0:POOL_HALO, :] = jnp.zeros((POOL_HALO, POOL_WIDTH), F32)

    x = x_ref[...]
    xb = x.astype(BF16)
    p = jnp.dot(xb, wp_ref[...], preferred_element_type=F32)
    gate_logits = jnp.dot(xb, wg_ref[...], preferred_element_type=F32)
    y_attn = jnp.concatenate(
        [lax.dot_general(at_ref[c], wba_ref[...], TN_DIMS, preferred_element_type=F32)
         for c in range(at_ref.shape[0])], axis=0)

    pbuf_ref[POOL_HALO:POOL_HALO + tile, :] = p
    pos = t * tile + lax.broadcasted_iota(jnp.int32, (tile, POOL_GROUP), 0)
    ys = []
    for gi, w in enumerate(POOL_WINDOWS):
        cols = slice(gi * POOL_GROUP, (gi + 1) * POOL_GROUP)
        pg = p[:, cols]
        win = pg
        for back in range(1, w):
            win = win + pbuf_ref[POOL_HALO - back:POOL_HALO - back + tile, cols]
        cnt = jnp.minimum(pos + 1, w).astype(F32)
        diff = win / cnt - pg
        ys.append(jnp.dot(diff.astype(BF16), wpool_ref[gi], preferred_element_type=F32))
    pbuf_ref[0:POOL_HALO, :] = pbuf_ref[tile:tile + POOL_HALO, :]
    y = jnp.concatenate(ys, axis=1) * pscale_ref[...]
    y_pool = jnp.dot(y.astype(BF16), wbp_ref[...], preferred_element_type=F32)

    gates = 1.0 / (1.0 + jnp.exp(-gate_logits))
    mixed = gates[:, :d] * y_attn + gates[:, d:] * y_pool
    z = alpha * x + jnp.dot(mixed.astype(BF16), wout_ref[...], preferred_element_type=F32)
    h_ref[...] = _layer_norm(z, g_ref[...], b_ref[...])


def _mixer(alpha, x, attn_t, wp, wg, wpool, pscale, wba, wbp, wout, ln_g, ln_b):
    b, s, d = x.shape
    bpt = TOKEN_TILE // MOBA_BLOCK
    consts = (wp, wg, wpool, pscale, wba, wbp, wout, ln_g, ln_b)
    return pl.pallas_call(
        functools.partial(_mixer_kernel, alpha),
        grid=(b, s // TOKEN_TILE),
        in_specs=[pl.BlockSpec((None, TOKEN_TILE, d), lambda i, t: (i, t, 0)),
                  pl.BlockSpec((None, bpt, ATTN_WIDTH, MOBA_BLOCK), lambda i, t: (i, t, 0, 0))]
                 + [_const_spec(c.shape) for c in consts],
        out_specs=pl.BlockSpec((None, TOKEN_TILE, d), lambda i, t: (i, t, 0)),
        out_shape=jax.ShapeDtypeStruct((b, s, d), F32),
        scratch_shapes=[pltpu.VMEM((POOL_HALO + TOKEN_TILE, POOL_WIDTH), F32)],
        compiler_params=_params(2),
        name="mixer",
    )(x, attn_t, *consts)


def _ffn_kernel(alpha, tiles_per_seq, n_tiles, h_ref, win_ref, cw_ref, cb_ref, wo_ref, g_ref, b_ref,
                o_ref, carry_ref, z_ref, act_ref):
    g = pl.program_id(0)
    tile = h_ref.shape[0]
    d_ff = wo_ref.shape[0]
    n_chunks = d_ff // FF_CHUNK

    @pl.when(g == 0)
    def _():
        z_ref[...] = jnp.zeros(z_ref.shape, F32)

    @pl.when(g % tiles_per_seq == 0)
    def _():
        carry_ref[...] = jnp.zeros(carry_ref.shape, F32)

    @pl.when(g == n_tiles)
    def _():
        o_ref[...] = _layer_norm(z_ref[...], g_ref[...], b_ref[...])

    @pl.when(g < n_tiles)
    def _():
        o_ref[...] = _layer_norm(z_ref[...], g_ref[...], b_ref[...])
        _ffn_tile(alpha, tile, n_chunks, d_ff, h_ref, win_ref, cw_ref, cb_ref, wo_ref, carry_ref, z_ref, act_ref)


def _ffn_tile(alpha, tile, n_chunks, d_ff, h_ref, win_ref, cw_ref, cb_ref, wo_ref, carry_ref, z_ref, act_ref):
    h = h_ref[...]
    hb = h.astype(BF16)
    row = lax.broadcasted_iota(jnp.int32, (CONV_HALO, FF_CHUNK), 0)

    def cols(c, base=0):
        return slice(base + c * FF_CHUNK, base + (c + 1) * FF_CHUNK)

    def project(c):
        return (jnp.dot(hb, win_ref[:, cols(c)], preferred_element_type=F32),
                jnp.dot(hb, win_ref[:, cols(c, d_ff)], preferred_element_type=F32))

    def shifted(a, prev, back):
        rolled = pltpu.roll(a, back, 0)
        top = rolled[0:CONV_HALO]
        for r in range(back):
            top = jnp.where(row == r, prev[CONV_HALO - back + r:CONV_HALO - back + r + 1, :], top)
        return jnp.concatenate([top, rolled[CONV_HALO:]], axis=0)

    nxt = project(0)
    for c in range(n_chunks):
        a, u = nxt
        if c + 1 < n_chunks:
            nxt = project(c + 1)
        prev = carry_ref[c]
        carry_ref[c] = a[tile - CONV_HALO:tile, :]
        cw = cw_ref[:, cols(c)]
        y = cb_ref[:, cols(c)]
        for tap in range(CONV_WIDTH):
            back = CONV_WIDTH - 1 - tap
            src = a if back == 0 else shifted(a, prev, back)
            y = y + src * cw[tap:tap + 1, :]
        act = 0.5 * y * (1.0 + lax.erf(y * math.sqrt(0.5))) * u
        act_ref[:, cols(c)] = act.astype(BF16)
    split = (n_chunks - 1) * FF_CHUNK
    f = (jnp.dot(act_ref[:, :split], wo_ref[:split, :], preferred_element_type=F32)
         + jnp.dot(act_ref[:, split:], wo_ref[split:, :], preferred_element_type=F32))
    z_ref[...] = alpha * h + f


def _conv_ffn(alpha, h, win, cw, cb, wo, ln_g, ln_b):
    b, s, d = h.shape
    d_ff = wo.shape[0]
    tiles_per_seq = s // FFN_TILE
    n_tiles = b * tiles_per_seq
    consts = (win, cw, cb, wo, ln_g, ln_b)

    def tile_index(g):
        return (g // tiles_per_seq, g % tiles_per_seq, 0)

    return pl.pallas_call(
        functools.partial(_ffn_kernel, alpha, tiles_per_seq, n_tiles),
        grid=(n_tiles + 1,),
        in_specs=[pl.BlockSpec((None, FFN_TILE, d), lambda g: tile_index(jnp.minimum(g, n_tiles - 1)))]
                 + [_const_spec(c.shape) for c in consts],
        out_specs=pl.BlockSpec((None, FFN_TILE, d), lambda g: tile_index(jnp.maximum(g - 1, 0))),
        out_shape=jax.ShapeDtypeStruct((b, s, d), F32),
        scratch_shapes=[pltpu.VMEM((d_ff // FF_CHUNK, CONV_HALO, FF_CHUNK), F32),
                        pltpu.VMEM((FFN_TILE, d), F32),
                        pltpu.VMEM((FFN_TILE, d_ff), BF16)],
        compiler_params=_params(1),
        name="conv_ffn",
    )(h, *consts)


def kernel(x, w_in, rel_bias, w_pool_group, pool_scale, w_branch_attn, w_branch_pool, w_out, ln1_g,
           ln1_b, w_ffn_in, conv_w, conv_b, w_ffn_out, ln2_g, ln2_b):
    b, s, d = x.shape
    depth = w_in.shape[0]
    d_ff = w_ffn_out.shape[1]
    assert s % TOKEN_TILE == 0 and TOKEN_TILE % MOBA_BLOCK == 0 and d_ff % FF_CHUNK == 0
    assert s % FFN_TILE == 0
    assert w_in.shape[2] == 3 * ATTN_WIDTH + POOL_WIDTH + 2 * d
    alpha = (2.0 * depth) ** 0.25
    assert PEN_SPLIT * (s // MOBA_BLOCK) <= K_PAD - HEAD_DIM - 16
    scale = LOG2E / math.sqrt(HEAD_DIM)

    bias_own, bias_prev = _rel_bias_tables(rel_bias)
    h = x
    for l in range(depth):
        w = w_in[l]
        wq_t = (w[:, :ATTN_WIDTH] * scale).T.astype(BF16)
        wk = w[:, ATTN_WIDTH:2 * ATTN_WIDTH].reshape(d, N_HEADS, HEAD_DIM)
        wk_pad = jnp.pad(wk, ((0, 0), (0, 0), (0, K_PAD - HEAD_DIM))).reshape(d, N_HEADS * K_PAD)
        wv_t = w[:, 2 * ATTN_WIDTH:3 * ATTN_WIDTH].T.astype(BF16)
        qt, k, vt = _qkv_proj(h, wq_t, wk_pad.astype(BF16), wv_t)
        attn_t = _moba_attention(rel_bias, qt, k, vt, bias_own, bias_prev)
        h1 = _mixer(alpha, h, attn_t,
                    w[:, 3 * ATTN_WIDTH:3 * ATTN_WIDTH + POOL_WIDTH].astype(BF16),
                    w[:, 3 * ATTN_WIDTH + POOL_WIDTH:].astype(BF16),
                    w_pool_group[l].astype(BF16),
                    pool_scale[l].reshape(1, POOL_WIDTH),
                    w_branch_attn[l].astype(BF16),
                    w_branch_pool[l].astype(BF16),
                    w_out[l].astype(BF16),
                    ln1_g[l].reshape(1, d), ln1_b[l].reshape(1, d))
        h = _conv_ffn(alpha, h1, w_ffn_in[l].astype(BF16), conv_w[l], conv_b[l].reshape(1, d_ff),
                      w_ffn_out[l].astype(BF16), ln2_g[l].reshape(1, d), ln2_b[l].reshape(1, d))
    return h
```

```python
import functools
import math

import jax
import jax.numpy as jnp
from jax import lax
from jax.experimental import pallas as pl
from jax.experimental.pallas import tpu as pltpu

N_HEADS = 8
HEAD_DIM = 64
ATTN_WIDTH = N_HEADS * HEAD_DIM
MOBA_BLOCK = 256
MOBA_TOPK = 3
POOL_WINDOWS = (2, 4, 8, 16)
POOL_GROUP = 128
POOL_WIDTH = POOL_GROUP * len(POOL_WINDOWS)
N_BUCKETS = 32
MAX_DISTANCE = 128
CONV_WIDTH = 3
LN_EPS = 1e-5
NEG = -1e30
M_INIT = 0.5 * NEG
LOG2E = math.log2(math.e)
FILL_AHEAD = 2
PEN_SPLIT = 3

LANES = 128
K_PAD = LANES
V_ONES = 16
POOL_HALO = max(POOL_WINDOWS)
CONV_HALO = 8
FF_CHUNK = 256
TOKEN_TILE = 1024
QKV_TILE = 1024
FFN_TILE = 1024
VMEM_LIMIT = 56 * 1024 * 1024

F32 = jnp.float32
BF16 = jnp.bfloat16
NT_DIMS = (((1,), (1,)), ((), ()))
TN_DIMS = (((0,), (0,)), ((), ()))


def _params(n_grid):
    return pltpu.CompilerParams(dimension_semantics=("arbitrary",) * n_grid,
                                vmem_limit_bytes=VMEM_LIMIT)


def _const_spec(shape):
    nd = len(shape)
    return pl.BlockSpec(shape, lambda *_: (0,) * nd)


def _rel_bias_kernel(rb_ref, own_ref, prev_ref):
    h = pl.program_id(0)
    kk = lax.broadcasted_iota(jnp.int32, (MOBA_BLOCK, MOBA_BLOCK), 0)
    qq = lax.broadcasted_iota(jnp.int32, (MOBA_BLOCK, MOBA_BLOCK), 1)
    max_exact = N_BUCKETS // 2

    def table(dist):
        n = jnp.maximum(dist, 0)
        nf = jnp.maximum(n, 1).astype(F32)
        large = max_exact + (jnp.log(nf / max_exact) / math.log(MAX_DISTANCE / max_exact)
                             * (N_BUCKETS - max_exact)).astype(jnp.int32)
        large = jnp.minimum(large, N_BUCKETS - 1)
        bucket = jnp.where(n < max_exact, n, large)
        out = jnp.zeros(dist.shape, F32)
        for b in range(N_BUCKETS):
            out = jnp.where(bucket == b, rb_ref[h, b], out)
        return out

    own_ref[...] = table(qq - kk) * LOG2E
    prev_ref[...] = table(qq - kk + MOBA_BLOCK) * LOG2E


def _rel_bias_tables(rel_bias):
    shape = jax.ShapeDtypeStruct((N_HEADS, MOBA_BLOCK, MOBA_BLOCK), F32)
    spec = pl.BlockSpec((None, MOBA_BLOCK, MOBA_BLOCK), lambda h: (h, 0, 0))
    return pl.pallas_call(
        _rel_bias_kernel,
        grid=(N_HEADS,),
        in_specs=[pl.BlockSpec(memory_space=pltpu.SMEM)],
        out_specs=[spec, spec],
        out_shape=[shape, shape],
        compiler_params=_params(1),
        name="rel_bias_tables",
    )(rel_bias)


def _qkv_kernel(n_blocks, x_ref, wq_ref, wk_ref, wv_ref, qt_ref, k_ref, vt_ref):
    xb = x_ref[...].astype(BF16)
    qt = lax.dot_general(wq_ref[...], xb, NT_DIMS, preferred_element_type=F32)
    vt = lax.dot_general(wv_ref[...], xb, NT_DIMS, preferred_element_type=F32)
    k = jnp.dot(xb, wk_ref[...], preferred_element_type=F32)
    tile = k.shape[0]
    blk_id = (pl.program_id(1) * (tile // MOBA_BLOCK)
              + lax.shift_right_logical(lax.broadcasted_iota(jnp.int32, k.shape, 0),
                                        MOBA_BLOCK.bit_length() - 1))
    pad_lane = (lax.broadcasted_iota(jnp.int32, k.shape, 1) & (K_PAD - 1)) - HEAD_DIM
    hot = pad_lane == blk_id
    for rep in range(1, PEN_SPLIT):
        hot = hot | (pad_lane == blk_id + rep * n_blocks)
    k_ref[...] = jnp.where(hot, 1.0, k).astype(BF16)
    for c in range(qt_ref.shape[0]):
        cols = slice(c * MOBA_BLOCK, (c + 1) * MOBA_BLOCK)
        qt_ref[c] = qt[:, cols].astype(BF16)
        vt_ref[c] = vt[:, cols].astype(BF16)


def _qkv_proj(x, wq_t, wk_pad, wv_t):
    b, s, d = x.shape
    nb = s // MOBA_BLOCK
    bpt = QKV_TILE // MOBA_BLOCK
    t_shape = jax.ShapeDtypeStruct((b, nb, ATTN_WIDTH, MOBA_BLOCK), BF16)
    t_spec = pl.BlockSpec((None, bpt, ATTN_WIDTH, MOBA_BLOCK), lambda i, t: (i, t, 0, 0))
    return pl.pallas_call(
        functools.partial(_qkv_kernel, nb),
        grid=(b, s // QKV_TILE),
        in_specs=[pl.BlockSpec((None, QKV_TILE, d), lambda i, t: (i, t, 0)),
                  _const_spec(wq_t.shape), _const_spec(wk_pad.shape), _const_spec(wv_t.shape)],
        out_specs=[t_spec,
                   pl.BlockSpec((None, QKV_TILE, N_HEADS * K_PAD), lambda i, t: (i, t, 0)),
                   t_spec],
        out_shape=[t_shape, jax.ShapeDtypeStruct((b, s, N_HEADS * K_PAD), BF16), t_shape],
        compiler_params=_params(2),
        name="qkv_proj",
    )(x, wq_t, wk_pad, wv_t)


def _split_bf16(x):
    parts = []
    rem = x
    for _ in range(PEN_SPLIT):
        part = rem.astype(BF16)
        parts.append(part)
        rem = rem - part.astype(F32)
    return parts


def _select_blocks(gate, i):
    nb = gate.shape[0]
    bidx = lax.broadcasted_iota(jnp.int32, gate.shape, 0)
    avail = jnp.where(bidx < i, 1, 0)
    chosen = jnp.zeros(gate.shape, jnp.int32)
    for _ in range(MOBA_TOPK):
        live = avail == 1
        top = jnp.max(jnp.where(live, gate, -jnp.inf), axis=0, keepdims=True)
        cand = jnp.where(live, jnp.where(gate == top, bidx, nb), nb)
        pick = jnp.where(bidx == jnp.min(cand, axis=0, keepdims=True), 1, 0)
        chosen = chosen + pick
        avail = avail - pick
    return chosen


def _attn_kernel(rb_ref, qt_ref, k_ref, vt_ref, bown_ref, bprev_ref, o_ref,
                 kmean_ref, qa_ref, s_ref, m_ref, acc_ref):
    i = pl.program_id(1)
    nb = vt_ref.shape[0]
    blk = MOBA_BLOCK

    @pl.when(i == 0)
    def _():
        for j in range(nb):
            kj = k_ref[j * blk:(j + 1) * blk, :].astype(F32)
            kmean_ref[j:j + 1, :] = jnp.sum(kj, axis=0, keepdims=True) * (1.0 / blk)

    def put_scores(h, j, slot):
        start = pl.multiple_of(j * blk, blk)
        s_ref[slot, h] = jnp.dot(k_ref[pl.ds(start, blk), h * K_PAD:(h + 1) * K_PAD], qa_ref[h],
                                 preferred_element_type=F32)

    bidx = lax.broadcasted_iota(jnp.int32, (nb, blk), 0)
    gates = []
    for h in range(N_HEADS):
        q = qt_ref[h * HEAD_DIM:(h + 1) * HEAD_DIM, :]
        qz = jnp.concatenate([q, jnp.zeros((K_PAD - HEAD_DIM, blk), BF16)], axis=0)
        parts = _split_bf16(kmean_ref[:, h * K_PAD:(h + 1) * K_PAD])
        g3 = jnp.dot(jnp.concatenate(parts, axis=0), qz, preferred_element_type=F32)
        gate = g3[0:nb]
        for r in range(1, PEN_SPLIT):
            gate = gate + g3[r * nb:(r + 1) * nb]
        gates.append(gate)
    for h in range(N_HEADS):
        q = qt_ref[h * HEAD_DIM:(h + 1) * HEAD_DIM, :]
        chosen = _select_blocks(gates[h], i)
        far_bias = jnp.where(bidx < i - 1, rb_ref[h, N_BUCKETS - 1] * LOG2E, 0.0)
        pen = jnp.where(bidx == i, 0.0, jnp.where(chosen == 1, far_bias, NEG))
        qa_ref[h] = jnp.concatenate(
            [q] + _split_bf16(pen)
            + [jnp.zeros((K_PAD - HEAD_DIM - PEN_SPLIT * nb, blk), BF16)], axis=0)
        put_scores(h, 0, 0)
        put_scores(h, jnp.minimum(1, i), 1)

    m_ref[...] = jnp.full(m_ref.shape, M_INIT, F32)
    acc_ref[...] = jnp.zeros(acc_ref.shape, F32)

    ones = jnp.ones((V_ONES, blk), BF16)
    causal = (lax.broadcasted_iota(jnp.int32, (blk, blk), 1)
              >= lax.broadcasted_iota(jnp.int32, (blk, blk), 0))

    def absorb(h, j, slot, kind):
        s = s_ref[slot, h]
        if kind == "own":
            s = jnp.where(causal, s + bown_ref[h], NEG)
        elif kind == "prev":
            s = s + bprev_ref[h]
        m_old = m_ref[h]
        m_new = jnp.maximum(m_old, jnp.max(s, axis=0, keepdims=True))
        p = jnp.exp2(s - m_new).astype(BF16)
        va = jnp.concatenate([vt_ref[j, h * HEAD_DIM:(h + 1) * HEAD_DIM, :], ones], axis=0)
        acc_ref[h] = jnp.exp2(m_old - m_new) * acc_ref[h] + jnp.dot(va, p, preferred_element_type=F32)
        m_ref[h] = m_new

    def stage(j, slot, kind, fill=None):
        if fill is not None:
            for h in range(FILL_AHEAD):
                put_scores(h, fill[0], fill[1])
        for h in range(N_HEADS):
            if fill is not None and h + FILL_AHEAD < N_HEADS:
                put_scores(h + FILL_AHEAD, fill[0], fill[1])
            absorb(h, j, slot, kind)

    def triple_body(a, carry):
        j0 = 3 * a
        stage(j0, 0, "far", fill=(j0 + 2, 2))
        stage(j0 + 1, 1, "far", fill=(j0 + 3, 0))
        stage(j0 + 2, 2, "far", fill=(j0 + 4, 1))
        return carry

    n_far = jnp.maximum(i - 1, 0)
    n_triples = lax.div(n_far, 3)
    rest = n_far - 3 * n_triples
    lax.fori_loop(0, n_triples, triple_body, 0)

    @pl.when(i == 0)
    def _():
        stage(i, 0, "own")

    @pl.when(jnp.logical_and(i >= 1, rest == 0))
    def _():
        stage(i - 1, 0, "prev")
        stage(i, 1, "own")

    @pl.when(rest == 1)
    def _():
        stage(i - 2, 0, "far", fill=(i, 2))
        stage(i - 1, 1, "prev")
        stage(i, 2, "own")

    @pl.when(rest == 2)
    def _():
        stage(i - 3, 0, "far", fill=(i - 1, 2))
        stage(i - 2, 1, "far", fill=(i, 0))
        stage(i - 1, 2, "prev")
        stage(i, 0, "own")

    for h in range(N_HEADS):
        acc = acc_ref[h]
        o_ref[h * HEAD_DIM:(h + 1) * HEAD_DIM, :] = (
            acc[0:HEAD_DIM] / acc[HEAD_DIM:HEAD_DIM + 1]).astype(o_ref.dtype)


def _moba_attention(rel_bias, qt, k, vt, bias_own, bias_prev):
    b, nb, _, blk = qt.shape
    s = nb * blk
    q_tile = pl.BlockSpec((None, None, ATTN_WIDTH, blk), lambda bi, i: (bi, i, 0, 0))
    return pl.pallas_call(
        _attn_kernel,
        grid=(b, nb),
        in_specs=[pl.BlockSpec(memory_space=pltpu.SMEM),
                  q_tile,
                  pl.BlockSpec((None, s, N_HEADS * K_PAD), lambda bi, i: (bi, 0, 0)),
                  pl.BlockSpec((None, nb, ATTN_WIDTH, blk), lambda bi, i: (bi, 0, 0, 0)),
                  _const_spec(bias_own.shape), _const_spec(bias_prev.shape)],
        out_specs=q_tile,
        out_shape=jax.ShapeDtypeStruct((b, nb, ATTN_WIDTH, blk), BF16),
        scratch_shapes=[pltpu.VMEM((nb, N_HEADS * K_PAD), F32),
                        pltpu.VMEM((N_HEADS, K_PAD, blk), BF16),
                        pltpu.VMEM((3, N_HEADS, blk, blk), F32),
                        pltpu.VMEM((N_HEADS, 1, blk), F32),
                        pltpu.VMEM((N_HEADS, HEAD_DIM + V_ONES, blk), F32)],
        compiler_params=_params(2),
        name="moba_attention",
    )(rel_bias, qt, k, vt, bias_own, bias_prev)


def _layer_norm(z, g, b):
    mu = jnp.mean(z, axis=-1, keepdims=True)
    zc = z - mu
    var = jnp.mean(zc * zc, axis=-1, keepdims=True)
    return zc * lax.rsqrt(var + LN_EPS) * g + b


def _mixer_kernel(alpha, x_ref, at_ref, wp_ref, wg_ref, wpool_ref, pscale_ref, wba_ref, wbp_ref,
                  wout_ref, g_ref, b_ref, h_ref, pbuf_ref):
    t = pl.program_id(1)
    tile = x_ref.shape[0]
    d = x_ref.shape[1]

    @pl.when(t == 0)
    def _():
        pbuf_ref[0:POOL_HALO, :] = jnp.zeros((POOL_HALO, POOL_WIDTH), F32)

    x = x_ref[...]
    xb = x.astype(BF16)
    p = jnp.dot(xb, wp_ref[...], preferred_element_type=F32)
    gate_logits = jnp.dot(xb, wg_ref[...], preferred_element_type=F32)
    y_attn = jnp.concatenate(
        [lax.dot_general(at_ref[c], wba_ref[...], TN_DIMS, preferred_element_type=F32)
         for c in range(at_ref.shape[0])], axis=0)

    pbuf_ref[POOL_HALO:POOL_HALO + tile, :] = p
    pos = t * tile + lax.broadcasted_iota(jnp.int32, (tile, POOL_GROUP), 0)
    ys = []
    for gi, w in enumerate(POOL_WINDOWS):
        cols = slice(gi * POOL_GROUP, (gi + 1) * POOL_GROUP)
        pg = p[:, cols]
        win = pg
        for back in range(1, w):
            win = win + pbuf_ref[POOL_HALO - back:POOL_HALO - back + tile, cols]
        cnt = jnp.minimum(pos + 1, w).astype(F32)
        diff = win / cnt - pg
        ys.append(jnp.dot(diff.astype(BF16), wpool_ref[gi], preferred_element_type=F32))
    pbuf_ref[0:POOL_HALO, :] = pbuf_ref[tile:tile + POOL_HALO, :]
    y = jnp.concatenate(ys, axis=1) * pscale_ref[...]
    y_pool = jnp.dot(y.astype(BF16), wbp_ref[...], preferred_element_type=F32)

    gates = 1.0 / (1.0 + jnp.exp(-gate_logits))
    mixed = (gates[:, :d] * y_attn + gates[:, d:] * y_pool).astype(BF16)
    for rows in (slice(0, tile // 2), slice(tile // 2, tile)):
        z = alpha * x[rows] + jnp.dot(mixed[rows], wout_ref[...], preferred_element_type=F32)
        h_ref[rows, :] = _layer_norm(z, g_ref[...], b_ref[...])


def _mixer(alpha, x, attn_t, wp, wg, wpool, pscale, wba, wbp, wout, ln_g, ln_b):
    b, s, d = x.shape
    bpt = TOKEN_TILE // MOBA_BLOCK
    consts = (wp, wg, wpool, pscale, wba, wbp, wout, ln_g, ln_b)
    return pl.pallas_call(
        functools.partial(_mixer_kernel, alpha),
        grid=(b, s // TOKEN_TILE),
        in_specs=[pl.BlockSpec((None, TOKEN_TILE, d), lambda i, t: (i, t, 0)),
                  pl.BlockSpec((None, bpt, ATTN_WIDTH, MOBA_BLOCK), lambda i, t: (i, t, 0, 0))]
                 + [_const_spec(c.shape) for c in consts],
        out_specs=pl.BlockSpec((None, TOKEN_TILE, d), lambda i, t: (i, t, 0)),
        out_shape=jax.ShapeDtypeStruct((b, s, d), F32),
        scratch_shapes=[pltpu.VMEM((POOL_HALO + TOKEN_TILE, POOL_WIDTH), F32)],
        compiler_params=_params(2),
        name="mixer",
    )(x, attn_t, *consts)


def _ffn_kernel(alpha, tiles_per_seq, n_tiles, h_ref, win_ref, cw_ref, cb_ref, wo_ref, g_ref, b_ref,
                o_ref, carry_ref, z_ref, act_ref):
    g = pl.program_id(0)
    tile = h_ref.shape[0]
    d_ff = wo_ref.shape[0]
    n_chunks = d_ff // FF_CHUNK

    @pl.when(g == 0)
    def _():
        z_ref[...] = jnp.zeros(z_ref.shape, F32)

    @pl.when(g % tiles_per_seq == 0)
    def _():
        carry_ref[...] = jnp.zeros(carry_ref.shape, F32)

    @pl.when(g == n_tiles)
    def _():
        o_ref[...] = _layer_norm(z_ref[...], g_ref[...], b_ref[...])

    @pl.when(g < n_tiles)
    def _():
        o_ref[...] = _layer_norm(z_ref[...], g_ref[...], b_ref[...])
        _ffn_tile(alpha, tile, n_chunks, d_ff, h_ref, win_ref, cw_ref, cb_ref, wo_ref, carry_ref, z_ref, act_ref)


def _ffn_tile(alpha, tile, n_chunks, d_ff, h_ref, win_ref, cw_ref, cb_ref, wo_ref, carry_ref, z_ref, act_ref):
    h = h_ref[...]
    hb = h.astype(BF16)
    row = lax.broadcasted_iota(jnp.int32, (CONV_HALO, FF_CHUNK), 0)

    def cols(c, base=0):
        return slice(base + c * FF_CHUNK, base + (c + 1) * FF_CHUNK)

    def project(c):
        return (jnp.dot(hb, win_ref[:, cols(c)], preferred_element_type=F32),
                jnp.dot(hb, win_ref[:, cols(c, d_ff)], preferred_element_type=F32))

    def shifted(a, prev, back):
        rolled = pltpu.roll(a, back, 0)
        top = rolled[0:CONV_HALO]
        for r in range(back):
            top = jnp.where(row == r, prev[CONV_HALO - back + r:CONV_HALO - back + r + 1, :], top)
        return jnp.concatenate([top, rolled[CONV_HALO:]], axis=0)

    nxt = project(0)
    for c in range(n_chunks):
        a, u = nxt
        if c + 1 < n_chunks:
            nxt = project(c + 1)
        prev = carry_ref[c]
        carry_ref[c] = a[tile - CONV_HALO:tile, :]
        cw = cw_ref[:, cols(c)]
        y = cb_ref[:, cols(c)]
        for tap in range(CONV_WIDTH):
            back = CONV_WIDTH - 1 - tap
            src = a if back == 0 else shifted(a, prev, back)
            y = y + src * cw[tap:tap + 1, :]
        act = 0.5 * y * (1.0 + lax.erf(y * math.sqrt(0.5))) * u
        act_ref[:, cols(c)] = act.astype(BF16)
    split = (n_chunks - 1) * FF_CHUNK
    f = (jnp.dot(act_ref[:, :split], wo_ref[:split, :], preferred_element_type=F32)
         + jnp.dot(act_ref[:, split:], wo_ref[split:, :], preferred_element_type=F32))
    z_ref[...] = alpha * h + f


def _conv_ffn(alpha, h, win, cw, cb, wo, ln_g, ln_b):
    b, s, d = h.shape
    d_ff = wo.shape[0]
    tiles_per_seq = s // FFN_TILE
    n_tiles = b * tiles_per_seq
    consts = (win, cw, cb, wo, ln_g, ln_b)

    def tile_index(g):
        return (g // tiles_per_seq, g % tiles_per_seq, 0)

    return pl.pallas_call(
        functools.partial(_ffn_kernel, alpha, tiles_per_seq, n_tiles),
        grid=(n_tiles + 1,),
        in_specs=[pl.BlockSpec((None, FFN_TILE, d), lambda g: tile_index(jnp.minimum(g, n_tiles - 1)))]
                 + [_const_spec(c.shape) for c in consts],
        out_specs=pl.BlockSpec((None, FFN_TILE, d), lambda g: tile_index(jnp.maximum(g - 1, 0))),
        out_shape=jax.ShapeDtypeStruct((b, s, d), F32),
        scratch_shapes=[pltpu.VMEM((d_ff // FF_CHUNK, CONV_HALO, FF_CHUNK), F32),
                        pltpu.VMEM((FFN_TILE, d), F32),
                        pltpu.VMEM((FFN_TILE, d_ff), BF16)],
        compiler_params=_params(1),
        name="conv_ffn",
    )(h, *consts)


def kernel(x, w_in, rel_bias, w_pool_group, pool_scale, w_branch_attn, w_branch_pool, w_out, ln1_g,
           ln1_b, w_ffn_in, conv_w, conv_b, w_ffn_out, ln2_g, ln2_b):
    b, s, d = x.shape
    depth = w_in.shape[0]
    d_ff = w_ffn_out.shape[1]
    assert s % TOKEN_TILE == 0 and TOKEN_TILE % MOBA_BLOCK == 0 and d_ff % FF_CHUNK == 0
    assert s % FFN_TILE == 0 and s % QKV_TILE == 0 and QKV_TILE % MOBA_BLOCK == 0
    assert w_in.shape[2] == 3 * ATTN_WIDTH + POOL_WIDTH + 2 * d
    alpha = (2.0 * depth) ** 0.25
    assert PEN_SPLIT * (s // MOBA_BLOCK) <= K_PAD - HEAD_DIM - 16
    scale = LOG2E / math.sqrt(HEAD_DIM)

    bias_own, bias_prev = _rel_bias_tables(rel_bias)
    h = x
    for l in range(depth):
        w = w_in[l]
        wq_t = (w[:, :ATTN_WIDTH] * scale).T.astype(BF16)
        wk = w[:, ATTN_WIDTH:2 * ATTN_WIDTH].reshape(d, N_HEADS, HEAD_DIM)
        wk_pad = jnp.pad(wk, ((0, 0), (0, 0), (0, K_PAD - HEAD_DIM))).reshape(d, N_HEADS * K_PAD)
        wv_t = w[:, 2 * ATTN_WIDTH:3 * ATTN_WIDTH].T.astype(BF16)
        qt, k, vt = _qkv_proj(h, wq_t, wk_pad.astype(BF16), wv_t)
        attn_t = _moba_attention(rel_bias, qt, k, vt, bias_own, bias_prev)
        h1 = _mixer(alpha, h, attn_t,
                    w[:, 3 * ATTN_WIDTH:3 * ATTN_WIDTH + POOL_WIDTH].astype(BF16),
                    w[:, 3 * ATTN_WIDTH + POOL_WIDTH:].astype(BF16),
                    w_pool_group[l].astype(BF16),
                    pool_scale[l].reshape(1, POOL_WIDTH),
                    w_branch_attn[l].astype(BF16),
                    w_branch_pool[l].astype(BF16),
                    w_out[l].astype(BF16),
                    ln1_g[l].reshape(1, d), ln1_b[l].reshape(1, d))
        h = _conv_ffn(alpha, h1, w_ffn_in[l].astype(BF16), conv_w[l], conv_b[l].reshape(1, d_ff),
                      w_ffn_out[l].astype(BF16), ln2_g[l].reshape(1, d), ln2_b[l].reshape(1, d))
    return h
```

```python
import functools
import math

import jax
import jax.numpy as jnp
from jax import lax
from jax.experimental import pallas as pl
from jax.experimental.pallas import tpu as pltpu

N_HEADS = 8
HEAD_DIM = 64
ATTN_WIDTH = N_HEADS * HEAD_DIM
MOBA_BLOCK = 256
MOBA_TOPK = 3
POOL_WINDOWS = (2, 4, 8, 16)
POOL_GROUP = 128
POOL_WIDTH = POOL_GROUP * len(POOL_WINDOWS)
N_BUCKETS = 32
MAX_DISTANCE = 128
CONV_WIDTH = 3
LN_EPS = 1e-5
NEG = -1e30
M_INIT = 0.5 * NEG
LOG2E = math.log2(math.e)
FILL_AHEAD = 2
PEN_SPLIT = 3

LANES = 128
K_PAD = LANES
V_ONES = 16
POOL_HALO = max(POOL_WINDOWS)
CONV_HALO = 8
FF_CHUNK = 256
TOKEN_TILE = 1024
QKV_TILE = 1024
FFN_TILE = 1024
Q_BLOCKS_PER_STEP = 4
VMEM_LIMIT = 56 * 1024 * 1024

F32 = jnp.float32
BF16 = jnp.bfloat16
NT_DIMS = (((1,), (1,)), ((), ()))
TN_DIMS = (((0,), (0,)), ((), ()))


def _params(n_grid):
    return pltpu.CompilerParams(dimension_semantics=("arbitrary",) * n_grid,
                                vmem_limit_bytes=VMEM_LIMIT)


def _const_spec(shape):
    nd = len(shape)
    return pl.BlockSpec(shape, lambda *_: (0,) * nd)


def _rel_bias_kernel(rb_ref, own_ref, prev_ref):
    h = pl.program_id(0)
    kk = lax.broadcasted_iota(jnp.int32, (MOBA_BLOCK, MOBA_BLOCK), 0)
    qq = lax.broadcasted_iota(jnp.int32, (MOBA_BLOCK, MOBA_BLOCK), 1)
    max_exact = N_BUCKETS // 2

    def table(dist):
        n = jnp.maximum(dist, 0)
        nf = jnp.maximum(n, 1).astype(F32)
        large = max_exact + (jnp.log(nf / max_exact) / math.log(MAX_DISTANCE / max_exact)
                             * (N_BUCKETS - max_exact)).astype(jnp.int32)
        large = jnp.minimum(large, N_BUCKETS - 1)
        bucket = jnp.where(n < max_exact, n, large)
        out = jnp.zeros(dist.shape, F32)
        for b in range(N_BUCKETS):
            out = jnp.where(bucket == b, rb_ref[h, b], out)
        return out

    own_ref[...] = jnp.where(qq >= kk, table(qq - kk) * LOG2E, NEG)
    prev_ref[...] = table(qq - kk + MOBA_BLOCK) * LOG2E


def _rel_bias_tables(rel_bias):
    shape = jax.ShapeDtypeStruct((N_HEADS, MOBA_BLOCK, MOBA_BLOCK), F32)
    spec = pl.BlockSpec((None, MOBA_BLOCK, MOBA_BLOCK), lambda h: (h, 0, 0))
    return pl.pallas_call(
        _rel_bias_kernel,
        grid=(N_HEADS,),
        in_specs=[pl.BlockSpec(memory_space=pltpu.SMEM)],
        out_specs=[spec, spec],
        out_shape=[shape, shape],
        compiler_params=_params(1),
        name="rel_bias_tables",
    )(rel_bias)


def _qkv_kernel(n_blocks, x_ref, wq_ref, wk_ref, wv_ref, qt_ref, k_ref, vt_ref):
    xb = x_ref[...].astype(BF16)
    qt = lax.dot_general(wq_ref[...], xb, NT_DIMS, preferred_element_type=F32)
    vt = lax.dot_general(wv_ref[...], xb, NT_DIMS, preferred_element_type=F32)
    k = jnp.dot(xb, wk_ref[...], preferred_element_type=F32)
    tile = k.shape[0]
    blk_id = (pl.program_id(1) * (tile // MOBA_BLOCK)
              + lax.shift_right_logical(lax.broadcasted_iota(jnp.int32, k.shape, 0),
                                        MOBA_BLOCK.bit_length() - 1))
    pad_lane = (lax.broadcasted_iota(jnp.int32, k.shape, 1) & (K_PAD - 1)) - HEAD_DIM
    hot = pad_lane == blk_id
    for rep in range(1, PEN_SPLIT):
        hot = hot | (pad_lane == blk_id + rep * n_blocks)
    k_ref[...] = jnp.where(hot, 1.0, k).astype(BF16)
    for c in range(qt_ref.shape[0]):
        cols = slice(c * MOBA_BLOCK, (c + 1) * MOBA_BLOCK)
        qt_ref[c] = qt[:, cols].astype(BF16)
        vt_ref[c] = vt[:, cols].astype(BF16)


def _qkv_proj(x, wq_t, wk_pad, wv_t):
    b, s, d = x.shape
    nb = s // MOBA_BLOCK
    bpt = QKV_TILE // MOBA_BLOCK
    t_shape = jax.ShapeDtypeStruct((b, nb, ATTN_WIDTH, MOBA_BLOCK), BF16)
    t_spec = pl.BlockSpec((None, bpt, ATTN_WIDTH, MOBA_BLOCK), lambda i, t: (i, t, 0, 0))
    return pl.pallas_call(
        functools.partial(_qkv_kernel, nb),
        grid=(b, s // QKV_TILE),
        in_specs=[pl.BlockSpec((None, QKV_TILE, d), lambda i, t: (i, t, 0)),
                  _const_spec(wq_t.shape), _const_spec(wk_pad.shape), _const_spec(wv_t.shape)],
        out_specs=[t_spec,
                   pl.BlockSpec((None, QKV_TILE, N_HEADS * K_PAD), lambda i, t: (i, t, 0)),
                   t_spec],
        out_shape=[t_shape, jax.ShapeDtypeStruct((b, s, N_HEADS * K_PAD), BF16), t_shape],
        compiler_params=_params(2),
        name="qkv_proj",
    )(x, wq_t, wk_pad, wv_t)


def _split_bf16(x):
    parts = []
    rem = x
    for _ in range(PEN_SPLIT):
        part = rem.astype(BF16)
        parts.append(part)
        rem = rem - part.astype(F32)
    return parts


def _select_blocks(gate, i):
    nb = gate.shape[0]
    bidx = lax.broadcasted_iota(jnp.int32, gate.shape, 0)
    avail = jnp.where(bidx < i, 1, 0)
    chosen = jnp.zeros(gate.shape, jnp.int32)
    for _ in range(MOBA_TOPK):
        live = avail == 1
        top = jnp.max(jnp.where(live, gate, -jnp.inf), axis=0, keepdims=True)
        cand = jnp.where(live, jnp.where(gate == top, bidx, nb), nb)
        pick = jnp.where(bidx == jnp.min(cand, axis=0, keepdims=True), 1, 0)
        chosen = chosen + pick
        avail = avail - pick
    return chosen


def _attn_kernel(rb_ref, qt_ref, k_ref, vt_ref, bown_ref, bprev_ref, o_ref, *scratch):
    per_step = qt_ref.shape[0]

    def one_block(sub, carry):
        _attn_block(pl.program_id(1) * per_step + sub, rb_ref, qt_ref.at[sub], k_ref, vt_ref,
                    bown_ref, bprev_ref, o_ref.at[sub], *scratch)
        return carry

    lax.fori_loop(0, per_step, one_block, 0)


def _attn_block(i, rb_ref, qt_ref, k_ref, vt_ref, bown_ref, bprev_ref, o_ref,
                kmean_ref, qa_ref, s_ref, m_ref, acc_ref):
    nb = vt_ref.shape[0]
    blk = MOBA_BLOCK

    @pl.when(i == 0)
    def _():
        for j in range(nb):
            kj = k_ref[j * blk:(j + 1) * blk, :].astype(F32)
            kmean_ref[j:j + 1, :] = jnp.sum(kj, axis=0, keepdims=True) * (1.0 / blk)

    def put_scores(h, j, slot):
        start = pl.multiple_of(j * blk, blk)
        s_ref[slot, h] = jnp.dot(k_ref[pl.ds(start, blk), h * K_PAD:(h + 1) * K_PAD], qa_ref[h],
                                 preferred_element_type=F32)

    bidx = lax.broadcasted_iota(jnp.int32, (nb, blk), 0)
    gates = []
    for h in range(N_HEADS):
        q = qt_ref[h * HEAD_DIM:(h + 1) * HEAD_DIM, :]
        qz = jnp.concatenate([q, jnp.zeros((K_PAD - HEAD_DIM, blk), BF16)], axis=0)
        parts = _split_bf16(kmean_ref[:, h * K_PAD:(h + 1) * K_PAD])
        g3 = jnp.dot(jnp.concatenate(parts, axis=0), qz, preferred_element_type=F32)
        gate = g3[0:nb]
        for r in range(1, PEN_SPLIT):
            gate = gate + g3[r * nb:(r + 1) * nb]
        gates.append(gate)
    for h in range(N_HEADS):
        q = qt_ref[h * HEAD_DIM:(h + 1) * HEAD_DIM, :]
        chosen = _select_blocks(gates[h], i)
        far_bias = jnp.where(bidx < i - 1, rb_ref[h, N_BUCKETS - 1] * LOG2E, 0.0)
        pen = jnp.where(bidx == i, 0.0, jnp.where(chosen == 1, far_bias, NEG))
        qa_ref[h] = jnp.concatenate(
            [q] + _split_bf16(pen)
            + [jnp.zeros((K_PAD - HEAD_DIM - PEN_SPLIT * nb, blk), BF16)], axis=0)
        put_scores(h, 0, 0)
        put_scores(h, jnp.minimum(1, i), 1)

    m_ref[...] = jnp.full(m_ref.shape, M_INIT, F32)
    acc_ref[...] = jnp.zeros(acc_ref.shape, F32)

    ones = jnp.ones((V_ONES, blk), BF16)

    def absorb(h, j, slot, kind):
        s = s_ref[slot, h]
        if kind == "own":
            s = s + bown_ref[h]
        elif kind == "prev":
            s = s + bprev_ref[h]
        m_old = m_ref[h]
        m_new = jnp.maximum(m_old, jnp.max(s, axis=0, keepdims=True))
        p = jnp.exp2(s - m_new).astype(BF16)
        va = jnp.concatenate([vt_ref[j, h * HEAD_DIM:(h + 1) * HEAD_DIM, :], ones], axis=0)
        acc_ref[h] = jnp.exp2(m_old - m_new) * acc_ref[h] + jnp.dot(va, p, preferred_element_type=F32)
        m_ref[h] = m_new

    def stage(j, slot, kind, fill=None):
        if fill is not None:
            for h in range(FILL_AHEAD):
                put_scores(h, fill[0], fill[1])
        for h in range(N_HEADS):
            if fill is not None and h + FILL_AHEAD < N_HEADS:
                put_scores(h + FILL_AHEAD, fill[0], fill[1])
            absorb(h, j, slot, kind)

    def triple_body(a, carry):
        j0 = 3 * a
        stage(j0, 0, "far", fill=(j0 + 2, 2))
        stage(j0 + 1, 1, "far", fill=(j0 + 3, 0))
        stage(j0 + 2, 2, "far", fill=(j0 + 4, 1))
        return carry

    n_far = jnp.maximum(i - 1, 0)
    n_triples = lax.div(n_far, 3)
    rest = n_far - 3 * n_triples
    lax.fori_loop(0, n_triples, triple_body, 0)

    @pl.when(i == 0)
    def _():
        stage(i, 0, "own")

    @pl.when(jnp.logical_and(i >= 1, rest == 0))
    def _():
        stage(i - 1, 0, "prev")
        stage(i, 1, "own")

    @pl.when(rest == 1)
    def _():
        stage(i - 2, 0, "far", fill=(i, 2))
        stage(i - 1, 1, "prev")
        stage(i, 2, "own")

    @pl.when(rest == 2)
    def _():
        stage(i - 3, 0, "far", fill=(i - 1, 2))
        stage(i - 2, 1, "far", fill=(i, 0))
        stage(i - 1, 2, "prev")
        stage(i, 0, "own")

    for h in range(N_HEADS):
        acc = acc_ref[h]
        o_ref[h * HEAD_DIM:(h + 1) * HEAD_DIM, :] = (
            acc[0:HEAD_DIM] / acc[HEAD_DIM:HEAD_DIM + 1]).astype(o_ref.dtype)


def _moba_attention(rel_bias, qt, k, vt, bias_own, bias_prev):
    b, nb, _, blk = qt.shape
    s = nb * blk
    q_tile = pl.BlockSpec((None, Q_BLOCKS_PER_STEP, ATTN_WIDTH, blk), lambda bi, t: (bi, t, 0, 0))
    return pl.pallas_call(
        _attn_kernel,
        grid=(b, nb // Q_BLOCKS_PER_STEP),
        in_specs=[pl.BlockSpec(memory_space=pltpu.SMEM),
                  q_tile,
                  pl.BlockSpec((None, s, N_HEADS * K_PAD), lambda bi, i: (bi, 0, 0)),
                  pl.BlockSpec((None, nb, ATTN_WIDTH, blk), lambda bi, i: (bi, 0, 0, 0)),
                  _const_spec(bias_own.shape), _const_spec(bias_prev.shape)],
        out_specs=q_tile,
        out_shape=jax.ShapeDtypeStruct((b, nb, ATTN_WIDTH, blk), BF16),
        scratch_shapes=[pltpu.VMEM((nb, N_HEADS * K_PAD), F32),
                        pltpu.VMEM((N_HEADS, K_PAD, blk), BF16),
                        pltpu.VMEM((3, N_HEADS, blk, blk), F32),
                        pltpu.VMEM((N_HEADS, 1, blk), F32),
                        pltpu.VMEM((N_HEADS, HEAD_DIM + V_ONES, blk), F32)],
        compiler_params=_params(2),
        name="moba_attention",
    )(rel_bias, qt, k, vt, bias_own, bias_prev)


def _layer_norm(z, g, b):
    mu = jnp.mean(z, axis=-1, keepdims=True)
    zc = z - mu
    var = jnp.mean(zc * zc, axis=-1, keepdims=True)
    return zc * lax.rsqrt(var + LN_EPS) * g + b


def _mixer_kernel(alpha, x_ref, at_ref, wp_ref, wg_ref, wpool_ref, pscale_ref, wba_ref, wbp_ref,
                  wout_ref, g_ref, b_ref, h_ref, pbuf_ref):
    t = pl.program_id(1)
    tile = x_ref.shape[0]
    d = x_ref.shape[1]

    @pl.when(t == 0)
    def _():
        pbuf_ref[0:POOL_HALO, :] = jnp.zeros((POOL_HALO, POOL_WIDTH), F32)

    x = x_ref[...]
    xb = x.astype(BF16)
    p = jnp.dot(xb, wp_ref[...], preferred_element_type=F32)
    gate_logits = jnp.dot(xb, wg_ref[...], preferred_element_type=F32)
    y_attn = jnp.concatenate(
        [lax.dot_general(at_ref[c], wba_ref[...], TN_DIMS, preferred_element_type=F32)
         for c in range(at_ref.shape[0])], axis=0)

    pbuf_ref[POOL_HALO:POOL_HALO + tile, :] = p
    pos = t * tile + lax.broadcasted_iota(jnp.int32, (tile, POOL_GROUP), 0)
    ys = []
    for gi, w in enumerate(POOL_WINDOWS):
        cols = slice(gi * POOL_GROUP, (gi + 1) * POOL_GROUP)
        pg = p[:, cols]
        win = pg
        for back in range(1, w):
            win = win + pbuf_ref[POOL_HALO - back:POOL_HALO - back + tile, cols]
        cnt = jnp.minimum(pos + 1, w).astype(F32)
        diff = win / cnt - pg
        ys.append(jnp.dot(diff.astype(BF16), wpool_ref[gi], preferred_element_type=F32))
    pbuf_ref[0:POOL_HALO, :] = pbuf_ref[tile:tile + POOL_HALO, :]
    y = jnp.concatenate(ys, axis=1) * pscale_ref[...]
    y_pool = jnp.dot(y.astype(BF16), wbp_ref[...], preferred_element_type=F32)

    gates = 1.0 / (1.0 + jnp.exp(-gate_logits))
    mixed = (gates[:, :d] * y_attn + gates[:, d:] * y_pool).astype(BF16)
    for rows in (slice(0, tile // 2), slice(tile // 2, tile)):
        z = alpha * x[rows] + jnp.dot(mixed[rows], wout_ref[...], preferred_element_type=F32)
        h_ref[rows, :] = _layer_norm(z, g_ref[...], b_ref[...])


def _mixer(alpha, x, attn_t, wp, wg, wpool, pscale, wba, wbp, wout, ln_g, ln_b):
    b, s, d = x.shape
    bpt = TOKEN_TILE // MOBA_BLOCK
    consts = (wp, wg, wpool, pscale, wba, wbp, wout, ln_g, ln_b)
    return pl.pallas_call(
        functools.partial(_mixer_kernel, alpha),
        grid=(b, s // TOKEN_TILE),
        in_specs=[pl.BlockSpec((None, TOKEN_TILE, d), lambda i, t: (i, t, 0)),
                  pl.BlockSpec((None, bpt, ATTN_WIDTH, MOBA_BLOCK), lambda i, t: (i, t, 0, 0))]
                 + [_const_spec(c.shape) for c in consts],
        out_specs=pl.BlockSpec((None, TOKEN_TILE, d), lambda i, t: (i, t, 0)),
        out_shape=jax.ShapeDtypeStruct((b, s, d), F32),
        scratch_shapes=[pltpu.VMEM((POOL_HALO + TOKEN_TILE, POOL_WIDTH), F32)],
        compiler_params=_params(2),
        name="mixer",
    )(x, attn_t, *consts)


def _ffn_kernel(alpha, tiles_per_seq, n_tiles, h_ref, win_ref, cw_ref, cb_ref, wo_ref, g_ref, b_ref,
                o_ref, carry_ref, z_ref, act_ref):
    g = pl.program_id(0)
    tile = h_ref.shape[0]
    d_ff = wo_ref.shape[0]
    n_chunks = d_ff // FF_CHUNK

    @pl.when(g == 0)
    def _():
        z_ref[...] = jnp.zeros(z_ref.shape, F32)

    @pl.when(g % tiles_per_seq == 0)
    def _():
        carry_ref[...] = jnp.zeros(carry_ref.shape, F32)

    @pl.when(g == n_tiles)
    def _():
        o_ref[...] = _layer_norm(z_ref[...], g_ref[...], b_ref[...])

    @pl.when(g < n_tiles)
    def _():
        o_ref[...] = _layer_norm(z_ref[...], g_ref[...], b_ref[...])
        _ffn_tile(alpha, tile, n_chunks, d_ff, h_ref, win_ref, cw_ref, cb_ref, wo_ref, carry_ref, z_ref, act_ref)


def _ffn_tile(alpha, tile, n_chunks, d_ff, h_ref, win_ref, cw_ref, cb_ref, wo_ref, carry_ref, z_ref, act_ref):
    h = h_ref[...]
    hb = h.astype(BF16)
    row = lax.broadcasted_iota(jnp.int32, (CONV_HALO, FF_CHUNK), 0)

    def cols(c, base=0):
        return slice(base + c * FF_CHUNK, base + (c + 1) * FF_CHUNK)

    def project(c):
        return (jnp.dot(hb, win_ref[:, cols(c)], preferred_element_type=F32),
                jnp.dot(hb, win_ref[:, cols(c, d_ff)], preferred_element_type=F32))

    def shifted(a, prev, back):
        rolled = pltpu.roll(a, back, 0)
        top = rolled[0:CONV_HALO]
        for r in range(back):
            top = jnp.where(row == r, prev[CONV_HALO - back + r:CONV_HALO - back + r + 1, :], top)
        return jnp.concatenate([top, rolled[CONV_HALO:]], axis=0)

    nxt = project(0)
    for c in range(n_chunks):
        a, u = nxt
        if c + 1 < n_chunks:
            nxt = project(c + 1)
        prev = carry_ref[c]
        carry_ref[c] = a[tile - CONV_HALO:tile, :]
        cw = cw_ref[:, cols(c)]
        y = cb_ref[:, cols(c)]
        for tap in range(CONV_WIDTH):
            back = CONV_WIDTH - 1 - tap
            src = a if back == 0 else shifted(a, prev, back)
            y = y + src * cw[tap:tap + 1, :]
        act = 0.5 * y * (1.0 + lax.erf(y * math.sqrt(0.5))) * u
        act_ref[:, cols(c)] = act.astype(BF16)
    split = (n_chunks - 1) * FF_CHUNK
    f = (jnp.dot(act_ref[:, :split], wo_ref[:split, :], preferred_element_type=F32)
         + jnp.dot(act_ref[:, split:], wo_ref[split:, :], preferred_element_type=F32))
    z_ref[...] = alpha * h + f


def _conv_ffn(alpha, h, win, cw, cb, wo, ln_g, ln_b):
    b, s, d = h.shape
    d_ff = wo.shape[0]
    tiles_per_seq = s // FFN_TILE
    n_tiles = b * tiles_per_seq
    consts = (win, cw, cb, wo, ln_g, ln_b)

    def tile_index(g):
        return (g // tiles_per_seq, g % tiles_per_seq, 0)

    return pl.pallas_call(
        functools.partial(_ffn_kernel, alpha, tiles_per_seq, n_tiles),
        grid=(n_tiles + 1,),
        in_specs=[pl.BlockSpec((None, FFN_TILE, d), lambda g: tile_index(jnp.minimum(g, n_tiles - 1)))]
                 + [_const_spec(c.shape) for c in consts],
        out_specs=pl.BlockSpec((None, FFN_TILE, d), lambda g: tile_index(jnp.maximum(g - 1, 0))),
        out_shape=jax.ShapeDtypeStruct((b, s, d), F32),
        scratch_shapes=[pltpu.VMEM((d_ff // FF_CHUNK, CONV_HALO, FF_CHUNK), F32),
                        pltpu.VMEM((FFN_TILE, d), F32),
                        pltpu.VMEM((FFN_TILE, d_ff), BF16)],
        compiler_params=_params(1),
        name="conv_ffn",
    )(h, *consts)


def kernel(x, w_in, rel_bias, w_pool_group, pool_scale, w_branch_attn, w_branch_pool, w_out, ln1_g,
           ln1_b, w_ffn_in, conv_w, conv_b, w_ffn_out, ln2_g, ln2_b):
    b, s, d = x.shape
    depth = w_in.shape[0]
    d_ff = w_ffn_out.shape[1]
    assert s % TOKEN_TILE == 0 and TOKEN_TILE % MOBA_BLOCK == 0 and d_ff % FF_CHUNK == 0
    assert s % FFN_TILE == 0 and s % QKV_TILE == 0 and QKV_TILE % MOBA_BLOCK == 0
    assert s % (MOBA_BLOCK * Q_BLOCKS_PER_STEP) == 0
    assert w_in.shape[2] == 3 * ATTN_WIDTH + POOL_WIDTH + 2 * d
    alpha = (2.0 * depth) ** 0.25
    assert PEN_SPLIT * (s // MOBA_BLOCK) <= K_PAD - HEAD_DIM - 16
    scale = LOG2E / math.sqrt(HEAD_DIM)

    bias_own, bias_prev = _rel_bias_tables(rel_bias)
    h = x
    for l in range(depth):
        w = w_in[l]
        wq_t = (w[:, :ATTN_WIDTH] * scale).T.astype(BF16)
        wk = w[:, ATTN_WIDTH:2 * ATTN_WIDTH].reshape(d, N_HEADS, HEAD_DIM)
        wk_pad = jnp.pad(wk, ((0, 0), (0, 0), (0, K_PAD - HEAD_DIM))).reshape(d, N_HEADS * K_PAD)
        wv_t = w[:, 2 * ATTN_WIDTH:3 * ATTN_WIDTH].T.astype(BF16)
        qt, k, vt = _qkv_proj(h, wq_t, wk_pad.astype(BF16), wv_t)
        attn_t = _moba_attention(rel_bias, qt, k, vt, bias_own, bias_prev)
        h1 = _mixer(alpha, h, attn_t,
                    w[:, 3 * ATTN_WIDTH:3 * ATTN_WIDTH + POOL_WIDTH].astype(BF16),
                    w[:, 3 * ATTN_WIDTH + POOL_WIDTH:].astype(BF16),
                    w_pool_group[l].astype(BF16),
                    pool_scale[l].reshape(1, POOL_WIDTH),
                    w_branch_attn[l].astype(BF16),
                    w_branch_pool[l].astype(BF16),
                    w_out[l].astype(BF16),
                    ln1_g[l].reshape(1, d), ln1_b[l].reshape(1, d))
        h = _conv_ffn(alpha, h1, w_ffn_in[l].astype(BF16), conv_w[l], conv_b[l].reshape(1, d_ff),
                      w_ffn_out[l].astype(BF16), ln2_g[l].reshape(1, d), ln2_b[l].reshape(1, d))
    return h
```

```python
import functools
import math

import jax
import jax.numpy as jnp
from jax import lax
from jax.experimental import pallas as pl
from jax.experimental.pallas import tpu as pltpu

N_HEADS = 8
HEAD_DIM = 64
ATTN_WIDTH = N_HEADS * HEAD_DIM
MOBA_BLOCK = 256
MOBA_TOPK = 3
POOL_WINDOWS = (2, 4, 8, 16)
POOL_GROUP = 128
POOL_WIDTH = POOL_GROUP * len(POOL_WINDOWS)
N_BUCKETS = 32
MAX_DISTANCE = 128
CONV_WIDTH = 3
LN_EPS = 1e-5
NEG = -1e30
M_INIT = 0.5 * NEG
LOG2E = math.log2(math.e)
FILL_AHEAD = 2
PEN_SPLIT = 3

LANES = 128
K_PAD = LANES
BF16_SUBLANES = 16
MXU_TILE = 256
V_ONES = BF16_SUBLANES
SCORE_SLOTS = 3
POOL_HALO = max(POOL_WINDOWS)
CONV_HALO = 8
FF_CHUNK = MXU_TILE
TOKEN_TILE = 1024
QKV_TILE = 1024
FFN_TILE = 1024
Q_BLOCKS_PER_STEP = 4
VMEM_LIMIT = 56 * 1024 * 1024

F32 = jnp.float32
BF16 = jnp.bfloat16
NT_DIMS = (((1,), (1,)), ((), ()))
TN_DIMS = (((0,), (0,)), ((), ()))


def _params(n_grid):
    return pltpu.CompilerParams(dimension_semantics=("arbitrary",) * n_grid,
                                vmem_limit_bytes=VMEM_LIMIT)


def _const_spec(shape):
    nd = len(shape)
    return pl.BlockSpec(shape, lambda *_: (0,) * nd)


def _rel_bias_kernel(rb_ref, own_ref, prev_ref):
    h = pl.program_id(0)
    kk = lax.broadcasted_iota(jnp.int32, (MOBA_BLOCK, MOBA_BLOCK), 0)
    qq = lax.broadcasted_iota(jnp.int32, (MOBA_BLOCK, MOBA_BLOCK), 1)
    max_exact = N_BUCKETS // 2

    def table(dist):
        n = jnp.maximum(dist, 0)
        nf = jnp.maximum(n, 1).astype(F32)
        large = max_exact + (jnp.log(nf / max_exact) / math.log(MAX_DISTANCE / max_exact)
                             * (N_BUCKETS - max_exact)).astype(jnp.int32)
        large = jnp.minimum(large, N_BUCKETS - 1)
        bucket = jnp.where(n < max_exact, n, large)
        out = jnp.zeros(dist.shape, F32)
        for b in range(N_BUCKETS):
            out = jnp.where(bucket == b, rb_ref[h, b], out)
        return out

    own_ref[...] = jnp.where(qq >= kk, table(qq - kk) * LOG2E, NEG)
    prev_ref[...] = table(qq - kk + MOBA_BLOCK) * LOG2E


def _rel_bias_tables(rel_bias):
    shape = jax.ShapeDtypeStruct((N_HEADS, MOBA_BLOCK, MOBA_BLOCK), F32)
    spec = pl.BlockSpec((None, MOBA_BLOCK, MOBA_BLOCK), lambda h: (h, 0, 0))
    return pl.pallas_call(
        _rel_bias_kernel,
        grid=(N_HEADS,),
        in_specs=[pl.BlockSpec(memory_space=pltpu.SMEM)],
        out_specs=[spec, spec],
        out_shape=[shape, shape],
        compiler_params=_params(1),
        name="rel_bias_tables",
    )(rel_bias)


def _qkv_kernel(n_blocks, x_ref, wq_ref, wk_ref, wv_ref, qt_ref, k_ref, vt_ref):
    xb = x_ref[...].astype(BF16)
    qt = lax.dot_general(wq_ref[...], xb, NT_DIMS, preferred_element_type=F32)
    vt = lax.dot_general(wv_ref[...], xb, NT_DIMS, preferred_element_type=F32)
    k = jnp.dot(xb, wk_ref[...], preferred_element_type=F32)
    tile = k.shape[0]
    blk_id = (pl.program_id(1) * (tile // MOBA_BLOCK)
              + lax.shift_right_logical(lax.broadcasted_iota(jnp.int32, k.shape, 0),
                                        MOBA_BLOCK.bit_length() - 1))
    pad_lane = (lax.broadcasted_iota(jnp.int32, k.shape, 1) & (K_PAD - 1)) - HEAD_DIM
    hot = pad_lane == blk_id
    for rep in range(1, PEN_SPLIT):
        hot = hot | (pad_lane == blk_id + rep * n_blocks)
    k_ref[...] = jnp.where(hot, 1.0, k).astype(BF16)
    for c in range(qt_ref.shape[0]):
        cols = slice(c * MOBA_BLOCK, (c + 1) * MOBA_BLOCK)
        qt_ref[c] = qt[:, cols].astype(BF16)
        vt_ref[c] = vt[:, cols].astype(BF16)


def _qkv_proj(x, wq_t, wk_pad, wv_t):
    b, s, d = x.shape
    nb = s // MOBA_BLOCK
    bpt = QKV_TILE // MOBA_BLOCK
    t_shape = jax.ShapeDtypeStruct((b, nb, ATTN_WIDTH, MOBA_BLOCK), BF16)
    t_spec = pl.BlockSpec((None, bpt, ATTN_WIDTH, MOBA_BLOCK), lambda i, t: (i, t, 0, 0))
    return pl.pallas_call(
        functools.partial(_qkv_kernel, nb),
        grid=(b, s // QKV_TILE),
        in_specs=[pl.BlockSpec((None, QKV_TILE, d), lambda i, t: (i, t, 0)),
                  _const_spec(wq_t.shape), _const_spec(wk_pad.shape), _const_spec(wv_t.shape)],
        out_specs=[t_spec,
                   pl.BlockSpec((None, QKV_TILE, N_HEADS * K_PAD), lambda i, t: (i, t, 0)),
                   t_spec],
        out_shape=[t_shape, jax.ShapeDtypeStruct((b, s, N_HEADS * K_PAD), BF16), t_shape],
        compiler_params=_params(2),
        name="qkv_proj",
    )(x, wq_t, wk_pad, wv_t)


def _split_bf16(x):
    parts = []
    rem = x
    for _ in range(PEN_SPLIT):
        part = rem.astype(BF16)
        parts.append(part)
        rem = rem - part.astype(F32)
    return parts


def _select_blocks(gate, i):
    nb = gate.shape[0]
    bidx = lax.broadcasted_iota(jnp.int32, gate.shape, 0)
    avail = jnp.where(bidx < i, 1, 0)
    chosen = jnp.zeros(gate.shape, jnp.int32)
    for _ in range(MOBA_TOPK):
        live = avail == 1
        top = jnp.max(jnp.where(live, gate, -jnp.inf), axis=0, keepdims=True)
        cand = jnp.where(live, jnp.where(gate == top, bidx, nb), nb)
        pick = jnp.where(bidx == jnp.min(cand, axis=0, keepdims=True), 1, 0)
        chosen = chosen + pick
        avail = avail - pick
    return chosen


def _attn_kernel(rb_ref, qt_ref, k_ref, vt_ref, bown_ref, bprev_ref, o_ref, *scratch):
    per_step = qt_ref.shape[0]

    def one_block(sub, carry):
        _attn_block(pl.program_id(1) * per_step + sub, rb_ref, qt_ref.at[sub], k_ref, vt_ref,
                    bown_ref, bprev_ref, o_ref.at[sub], *scratch)
        return carry

    lax.fori_loop(0, per_step, one_block, 0)


def _attn_block(i, rb_ref, qt_ref, k_ref, vt_ref, bown_ref, bprev_ref, o_ref,
                kmean_ref, qa_ref, s_ref, m_ref, acc_ref):
    nb = vt_ref.shape[0]
    blk = MOBA_BLOCK

    @pl.when(i == 0)
    def _():
        for j in range(nb):
            kj = k_ref[j * blk:(j + 1) * blk, :].astype(F32)
            kmean_ref[j:j + 1, :] = jnp.sum(kj, axis=0, keepdims=True) * (1.0 / blk)

    def put_scores(h, j, slot):
        start = pl.multiple_of(j * blk, blk)
        s_ref[slot, h] = jnp.dot(k_ref[pl.ds(start, blk), h * K_PAD:(h + 1) * K_PAD], qa_ref[h],
                                 preferred_element_type=F32)

    bidx = lax.broadcasted_iota(jnp.int32, (nb, blk), 0)
    gates = []
    for h in range(N_HEADS):
        q = qt_ref[h * HEAD_DIM:(h + 1) * HEAD_DIM, :]
        qz = jnp.concatenate([q, jnp.zeros((K_PAD - HEAD_DIM, blk), BF16)], axis=0)
        parts = _split_bf16(kmean_ref[:, h * K_PAD:(h + 1) * K_PAD])
        g3 = jnp.dot(jnp.concatenate(parts, axis=0), qz, preferred_element_type=F32)
        gate = g3[0:nb]
        for r in range(1, PEN_SPLIT):
            gate = gate + g3[r * nb:(r + 1) * nb]
        gates.append(gate)
    for h in range(N_HEADS):
        q = qt_ref[h * HEAD_DIM:(h + 1) * HEAD_DIM, :]
        chosen = _select_blocks(gates[h], i)
        far_bias = jnp.where(bidx < i - 1, rb_ref[h, N_BUCKETS - 1] * LOG2E, 0.0)
        pen = jnp.where(bidx == i, 0.0, jnp.where(chosen == 1, far_bias, NEG))
        qa_ref[h] = jnp.concatenate(
            [q] + _split_bf16(pen)
            + [jnp.zeros((K_PAD - HEAD_DIM - PEN_SPLIT * nb, blk), BF16)], axis=0)
        put_scores(h, 0, 0)
        put_scores(h, jnp.minimum(1, i), 1)

    m_ref[...] = jnp.full(m_ref.shape, M_INIT, F32)
    acc_ref[...] = jnp.zeros(acc_ref.shape, F32)

    ones = jnp.ones((V_ONES, blk), BF16)

    def absorb(h, j, slot, kind):
        s = s_ref[slot, h]
        if kind == "own":
            s = s + bown_ref[h]
        elif kind == "prev":
            s = s + bprev_ref[h]
        m_old = m_ref[h]
        m_new = jnp.maximum(m_old, jnp.max(s, axis=0, keepdims=True))
        p = jnp.exp2(s - m_new).astype(BF16)
        va = jnp.concatenate([vt_ref[j, h * HEAD_DIM:(h + 1) * HEAD_DIM, :], ones], axis=0)
        acc_ref[h] = jnp.exp2(m_old - m_new) * acc_ref[h] + jnp.dot(va, p, preferred_element_type=F32)
        m_ref[h] = m_new

    def stage(j, slot, kind, fill=None):
        if fill is not None:
            for h in range(FILL_AHEAD):
                put_scores(h, fill[0], fill[1])
        for h in range(N_HEADS):
            if fill is not None and h + FILL_AHEAD < N_HEADS:
                put_scores(h + FILL_AHEAD, fill[0], fill[1])
            absorb(h, j, slot, kind)

    def triple_body(a, carry):
        j0 = 3 * a
        stage(j0, 0, "far", fill=(j0 + 2, 2))
        stage(j0 + 1, 1, "far", fill=(j0 + 3, 0))
        stage(j0 + 2, 2, "far", fill=(j0 + 4, 1))
        return carry

    n_far = jnp.maximum(i - 1, 0)
    n_triples = lax.div(n_far, 3)
    rest = n_far - 3 * n_triples
    lax.fori_loop(0, n_triples, triple_body, 0)

    @pl.when(i == 0)
    def _():
        stage(i, 0, "own")

    @pl.when(jnp.logical_and(i >= 1, rest == 0))
    def _():
        stage(i - 1, 0, "prev")
        stage(i, 1, "own")

    @pl.when(rest == 1)
    def _():
        stage(i - 2, 0, "far", fill=(i, 2))
        stage(i - 1, 1, "prev")
        stage(i, 2, "own")

    @pl.when(rest == 2)
    def _():
        stage(i - 3, 0, "far", fill=(i - 1, 2))
        stage(i - 2, 1, "far", fill=(i, 0))
        stage(i - 1, 2, "prev")
        stage(i, 0, "own")

    for h in range(N_HEADS):
        acc = acc_ref[h]
        o_ref[h * HEAD_DIM:(h + 1) * HEAD_DIM, :] = (
            acc[0:HEAD_DIM] / acc[HEAD_DIM:HEAD_DIM + 1]).astype(o_ref.dtype)


def _moba_attention(rel_bias, qt, k, vt, bias_own, bias_prev):
    b, nb, _, blk = qt.shape
    s = nb * blk
    q_tile = pl.BlockSpec((None, Q_BLOCKS_PER_STEP, ATTN_WIDTH, blk), lambda bi, t: (bi, t, 0, 0))
    return pl.pallas_call(
        _attn_kernel,
        grid=(b, nb // Q_BLOCKS_PER_STEP),
        in_specs=[pl.BlockSpec(memory_space=pltpu.SMEM),
                  q_tile,
                  pl.BlockSpec((None, s, N_HEADS * K_PAD), lambda bi, i: (bi, 0, 0)),
                  pl.BlockSpec((None, nb, ATTN_WIDTH, blk), lambda bi, i: (bi, 0, 0, 0)),
                  _const_spec(bias_own.shape), _const_spec(bias_prev.shape)],
        out_specs=q_tile,
        out_shape=jax.ShapeDtypeStruct((b, nb, ATTN_WIDTH, blk), BF16),
        scratch_shapes=[pltpu.VMEM((nb, N_HEADS * K_PAD), F32),
                        pltpu.VMEM((N_HEADS, K_PAD, blk), BF16),
                        pltpu.VMEM((SCORE_SLOTS, N_HEADS, blk, blk), F32),
                        pltpu.VMEM((N_HEADS, 1, blk), F32),
                        pltpu.VMEM((N_HEADS, HEAD_DIM + V_ONES, blk), F32)],
        compiler_params=_params(2),
        name="moba_attention",
    )(rel_bias, qt, k, vt, bias_own, bias_prev)


def _layer_norm(z, g, b):
    mu = jnp.mean(z, axis=-1, keepdims=True)
    zc = z - mu
    var = jnp.mean(zc * zc, axis=-1, keepdims=True)
    return zc * lax.rsqrt(var + LN_EPS) * g + b


def _mixer_kernel(alpha, x_ref, at_ref, wp_ref, wg_ref, wpool_ref, pscale_ref, wba_ref, wbp_ref,
                  wout_ref, g_ref, b_ref, h_ref, pbuf_ref):
    t = pl.program_id(1)
    tile = x_ref.shape[0]
    d = x_ref.shape[1]

    @pl.when(t == 0)
    def _():
        pbuf_ref[0:POOL_HALO, :] = jnp.zeros((POOL_HALO, POOL_WIDTH), F32)

    x = x_ref[...]
    xb = x.astype(BF16)
    p = jnp.dot(xb, wp_ref[...], preferred_element_type=F32)
    gate_logits = jnp.dot(xb, wg_ref[...], preferred_element_type=F32)
    y_attn = jnp.concatenate(
        [lax.dot_general(at_ref[c], wba_ref[...], TN_DIMS, preferred_element_type=F32)
         for c in range(at_ref.shape[0])], axis=0)

    pbuf_ref[POOL_HALO:POOL_HALO + tile, :] = p
    pos = t * tile + lax.broadcasted_iota(jnp.int32, (tile, POOL_GROUP), 0)
    ys = []
    for gi, w in enumerate(POOL_WINDOWS):
        cols = slice(gi * POOL_GROUP, (gi + 1) * POOL_GROUP)
        pg = p[:, cols]
        win = pg
        for back in range(1, w):
            win = win + pbuf_ref[POOL_HALO - back:POOL_HALO - back + tile, cols]
        cnt = jnp.minimum(pos + 1, w).astype(F32)
        diff = win / cnt - pg
        ys.append(jnp.dot(diff.astype(BF16), wpool_ref[gi], preferred_element_type=F32))
    pbuf_ref[0:POOL_HALO, :] = pbuf_ref[tile:tile + POOL_HALO, :]
    y = jnp.concatenate(ys, axis=1) * pscale_ref[...]
    y_pool = jnp.dot(y.astype(BF16), wbp_ref[...], preferred_element_type=F32)

    gates = 1.0 / (1.0 + jnp.exp(-gate_logits))
    mixed = (gates[:, :d] * y_attn + gates[:, d:] * y_pool).astype(BF16)
    for rows in (slice(0, tile // 2), slice(tile // 2, tile)):
        z = alpha * x[rows] + jnp.dot(mixed[rows], wout_ref[...], preferred_element_type=F32)
        h_ref[rows, :] = _layer_norm(z, g_ref[...], b_ref[...])


def _mixer(alpha, x, attn_t, wp, wg, wpool, pscale, wba, wbp, wout, ln_g, ln_b):
    b, s, d = x.shape
    bpt = TOKEN_TILE // MOBA_BLOCK
    consts = (wp, wg, wpool, pscale, wba, wbp, wout, ln_g, ln_b)
    return pl.pallas_call(
        functools.partial(_mixer_kernel, alpha),
        grid=(b, s // TOKEN_TILE),
        in_specs=[pl.BlockSpec((None, TOKEN_TILE, d), lambda i, t: (i, t, 0)),
                  pl.BlockSpec((None, bpt, ATTN_WIDTH, MOBA_BLOCK), lambda i, t: (i, t, 0, 0))]
                 + [_const_spec(c.shape) for c in consts],
        out_specs=pl.BlockSpec((None, TOKEN_TILE, d), lambda i, t: (i, t, 0)),
        out_shape=jax.ShapeDtypeStruct((b, s, d), F32),
        scratch_shapes=[pltpu.VMEM((POOL_HALO + TOKEN_TILE, POOL_WIDTH), F32)],
        compiler_params=_params(2),
        name="mixer",
    )(x, attn_t, *consts)


def _ffn_kernel(alpha, tiles_per_seq, n_tiles, h_ref, win_ref, cw_ref, cb_ref, wo_ref, g_ref, b_ref,
                o_ref, carry_ref, z_ref, act_ref):
    g = pl.program_id(0)
    tile = h_ref.shape[0]
    d_ff = wo_ref.shape[0]
    n_chunks = d_ff // FF_CHUNK

    @pl.when(g == 0)
    def _():
        z_ref[...] = jnp.zeros(z_ref.shape, F32)

    @pl.when(g % tiles_per_seq == 0)
    def _():
        carry_ref[...] = jnp.zeros(carry_ref.shape, F32)

    @pl.when(g == n_tiles)
    def _():
        o_ref[...] = _layer_norm(z_ref[...], g_ref[...], b_ref[...])

    @pl.when(g < n_tiles)
    def _():
        o_ref[...] = _layer_norm(z_ref[...], g_ref[...], b_ref[...])
        _ffn_tile(alpha, tile, n_chunks, d_ff, h_ref, win_ref, cw_ref, cb_ref, wo_ref, carry_ref, z_ref, act_ref)


def _ffn_tile(alpha, tile, n_chunks, d_ff, h_ref, win_ref, cw_ref, cb_ref, wo_ref, carry_ref, z_ref, act_ref):
    h = h_ref[...]
    hb = h.astype(BF16)
    row = lax.broadcasted_iota(jnp.int32, (CONV_HALO, FF_CHUNK), 0)

    def cols(c, base=0):
        return slice(base + c * FF_CHUNK, base + (c + 1) * FF_CHUNK)

    def project(c):
        return (jnp.dot(hb, win_ref[:, cols(c)], preferred_element_type=F32),
                jnp.dot(hb, win_ref[:, cols(c, d_ff)], preferred_element_type=F32))

    def shifted(a, prev, back):
        rolled = pltpu.roll(a, back, 0)
        top = rolled[0:CONV_HALO]
        for r in range(back):
            top = jnp.where(row == r, prev[CONV_HALO - back + r:CONV_HALO - back + r + 1, :], top)
        return jnp.concatenate([top, rolled[CONV_HALO:]], axis=0)

    nxt = project(0)
    for c in range(n_chunks):
        a, u = nxt
        if c + 1 < n_chunks:
            nxt = project(c + 1)
        prev = carry_ref[c]
        carry_ref[c] = a[tile - CONV_HALO:tile, :]
        cw = cw_ref[:, cols(c)]
        y = cb_ref[:, cols(c)]
        for tap in range(CONV_WIDTH):
            back = CONV_WIDTH - 1 - tap
            src = a if back == 0 else shifted(a, prev, back)
            y = y + src * cw[tap:tap + 1, :]
        act = 0.5 * y * (1.0 + lax.erf(y * math.sqrt(0.5))) * u
        act_ref[:, cols(c)] = act.astype(BF16)
    split = (n_chunks - 1) * FF_CHUNK
    f = (jnp.dot(act_ref[:, :split], wo_ref[:split, :], preferred_element_type=F32)
         + jnp.dot(act_ref[:, split:], wo_ref[split:, :], preferred_element_type=F32))
    z_ref[...] = alpha * h + f


def _conv_ffn(alpha, h, win, cw, cb, wo, ln_g, ln_b):
    b, s, d = h.shape
    d_ff = wo.shape[0]
    tiles_per_seq = s // FFN_TILE
    n_tiles = b * tiles_per_seq
    consts = (win, cw, cb, wo, ln_g, ln_b)

    def tile_index(g):
        return (g // tiles_per_seq, g % tiles_per_seq, 0)

    return pl.pallas_call(
        functools.partial(_ffn_kernel, alpha, tiles_per_seq, n_tiles),
        grid=(n_tiles + 1,),
        in_specs=[pl.BlockSpec((None, FFN_TILE, d), lambda g: tile_index(jnp.minimum(g, n_tiles - 1)))]
                 + [_const_spec(c.shape) for c in consts],
        out_specs=pl.BlockSpec((None, FFN_TILE, d), lambda g: tile_index(jnp.maximum(g - 1, 0))),
        out_shape=jax.ShapeDtypeStruct((b, s, d), F32),
        scratch_shapes=[pltpu.VMEM((d_ff // FF_CHUNK, CONV_HALO, FF_CHUNK), F32),
                        pltpu.VMEM((FFN_TILE, d), F32),
                        pltpu.VMEM((FFN_TILE, d_ff), BF16)],
        compiler_params=_params(1),
        name="conv_ffn",
    )(h, *consts)


def kernel(x, w_in, rel_bias, w_pool_group, pool_scale, w_branch_attn, w_branch_pool, w_out, ln1_g,
           ln1_b, w_ffn_in, conv_w, conv_b, w_ffn_out, ln2_g, ln2_b):
    b, s, d = x.shape
    depth = w_in.shape[0]
    d_ff = w_ffn_out.shape[1]
    assert s % TOKEN_TILE == 0 and TOKEN_TILE % MOBA_BLOCK == 0 and d_ff % FF_CHUNK == 0
    assert s % FFN_TILE == 0 and s % QKV_TILE == 0 and QKV_TILE % MOBA_BLOCK == 0
    assert s % (MOBA_BLOCK * Q_BLOCKS_PER_STEP) == 0
    assert w_in.shape[2] == 3 * ATTN_WIDTH + POOL_WIDTH + 2 * d
    alpha = (2.0 * depth) ** 0.25
    assert (K_PAD - HEAD_DIM - PEN_SPLIT * (s // MOBA_BLOCK)) % BF16_SUBLANES == 0
    assert K_PAD - HEAD_DIM - PEN_SPLIT * (s // MOBA_BLOCK) > 0
    scale = LOG2E / math.sqrt(HEAD_DIM)

    bias_own, bias_prev = _rel_bias_tables(rel_bias)
    h = x
    for l in range(depth):
        w = w_in[l]
        wq_t = (w[:, :ATTN_WIDTH] * scale).T.astype(BF16)
        wk = w[:, ATTN_WIDTH:2 * ATTN_WIDTH].reshape(d, N_HEADS, HEAD_DIM)
        wk_pad = jnp.pad(wk, ((0, 0), (0, 0), (0, K_PAD - HEAD_DIM))).reshape(d, N_HEADS * K_PAD)
        wv_t = w[:, 2 * ATTN_WIDTH:3 * ATTN_WIDTH].T.astype(BF16)
        qt, k, vt = _qkv_proj(h, wq_t, wk_pad.astype(BF16), wv_t)
        attn_t = _moba_attention(rel_bias, qt, k, vt, bias_own, bias_prev)
        h1 = _mixer(alpha, h, attn_t,
                    w[:, 3 * ATTN_WIDTH:3 * ATTN_WIDTH + POOL_WIDTH].astype(BF16),
                    w[:, 3 * ATTN_WIDTH + POOL_WIDTH:].astype(BF16),
                    w_pool_group[l].astype(BF16),
                    pool_scale[l].reshape(1, POOL_WIDTH),
                    w_branch_attn[l].astype(BF16),
                    w_branch_pool[l].astype(BF16),
                    w_out[l].astype(BF16),
                    ln1_g[l].reshape(1, d), ln1_b[l].reshape(1, d))
        h = _conv_ffn(alpha, h1, w_ffn_in[l].astype(BF16), conv_w[l], conv_b[l].reshape(1, d_ff),
                      w_ffn_out[l].astype(BF16), ln2_g[l].reshape(1, d), ln2_b[l].reshape(1, d))
    return h
```

```python
import functools
import math

import jax
import jax.numpy as jnp
from jax import lax
from jax.experimental import pallas as pl
from jax.experimental.pallas import tpu as pltpu

N_HEADS = 8
HEAD_DIM = 64
ATTN_WIDTH = N_HEADS * HEAD_DIM
MOBA_BLOCK = 256
MOBA_TOPK = 3
POOL_WINDOWS = (2, 4, 8, 16)
POOL_GROUP = 128
POOL_WIDTH = POOL_GROUP * len(POOL_WINDOWS)
N_BUCKETS = 32
MAX_DISTANCE = 128
CONV_WIDTH = 3
LN_EPS = 1e-5
NEG = -1e30
M_INIT = 0.5 * NEG
LOG2E = math.log2(math.e)
FILL_AHEAD = 2
PEN_SPLIT = 3

LANES = 128
K_PAD = LANES
BF16_SUBLANES = 16
MXU_TILE = 256
V_ONES = BF16_SUBLANES
SCORE_SLOTS = 3
POOL_HALO = max(POOL_WINDOWS)
CONV_HALO = 8
FF_CHUNK = MXU_TILE
TOKEN_TILE = 1024
QKV_TILE = 1024
FFN_TILE = 1024
Q_BLOCKS_PER_STEP = 4
VMEM_LIMIT = 56 * 1024 * 1024

F32 = jnp.float32
BF16 = jnp.bfloat16
NT_DIMS = (((1,), (1,)), ((), ()))
TN_DIMS = (((0,), (0,)), ((), ()))


def _params(n_grid):
    return pltpu.CompilerParams(dimension_semantics=("arbitrary",) * n_grid,
                                vmem_limit_bytes=VMEM_LIMIT)


def _const_spec(shape):
    nd = len(shape)
    return pl.BlockSpec(shape, lambda *_: (0,) * nd)


def _rel_bias_kernel(rb_ref, own_ref, prev_ref):
    h = pl.program_id(0)
    kk = lax.broadcasted_iota(jnp.int32, (MOBA_BLOCK, MOBA_BLOCK), 0)
    qq = lax.broadcasted_iota(jnp.int32, (MOBA_BLOCK, MOBA_BLOCK), 1)
    max_exact = N_BUCKETS // 2

    def table(dist):
        n = jnp.maximum(dist, 0)
        nf = jnp.maximum(n, 1).astype(F32)
        large = max_exact + (jnp.log(nf / max_exact) / math.log(MAX_DISTANCE / max_exact)
                             * (N_BUCKETS - max_exact)).astype(jnp.int32)
        large = jnp.minimum(large, N_BUCKETS - 1)
        bucket = jnp.where(n < max_exact, n, large)
        out = jnp.zeros(dist.shape, F32)
        for b in range(N_BUCKETS):
            out = jnp.where(bucket == b, rb_ref[h, b], out)
        return out

    own_ref[...] = jnp.where(qq >= kk, table(qq - kk) * LOG2E, NEG)
    prev_ref[...] = table(qq - kk + MOBA_BLOCK) * LOG2E


def _rel_bias_tables(rel_bias):
    shape = jax.ShapeDtypeStruct((N_HEADS, MOBA_BLOCK, MOBA_BLOCK), F32)
    spec = pl.BlockSpec((None, MOBA_BLOCK, MOBA_BLOCK), lambda h: (h, 0, 0))
    return pl.pallas_call(
        _rel_bias_kernel,
        grid=(N_HEADS,),
        in_specs=[pl.BlockSpec(memory_space=pltpu.SMEM)],
        out_specs=[spec, spec],
        out_shape=[shape, shape],
        compiler_params=_params(1),
        name="rel_bias_tables",
    )(rel_bias)


def _qkv_kernel(x_ref, wq_ref, wk_ref, wv_ref, qt_ref, k_ref, vt_ref):
    xb = x_ref[...].astype(BF16)
    qt = lax.dot_general(wq_ref[...], xb, NT_DIMS, preferred_element_type=F32)
    vt = lax.dot_general(wv_ref[...], xb, NT_DIMS, preferred_element_type=F32)
    k_ref[...] = jnp.dot(xb, wk_ref[...], preferred_element_type=F32).astype(BF16)
    for c in range(qt_ref.shape[0]):
        cols = slice(c * MOBA_BLOCK, (c + 1) * MOBA_BLOCK)
        qt_ref[c] = qt[:, cols].astype(BF16)
        vt_ref[c] = vt[:, cols].astype(BF16)


def _qkv_proj(x, wq_t, wk, wv_t):
    b, s, d = x.shape
    nb = s // MOBA_BLOCK
    bpt = QKV_TILE // MOBA_BLOCK
    t_shape = jax.ShapeDtypeStruct((b, nb, ATTN_WIDTH, MOBA_BLOCK), BF16)
    t_spec = pl.BlockSpec((None, bpt, ATTN_WIDTH, MOBA_BLOCK), lambda i, t: (i, t, 0, 0))
    return pl.pallas_call(
        _qkv_kernel,
        grid=(b, s // QKV_TILE),
        in_specs=[pl.BlockSpec((None, QKV_TILE, d), lambda i, t: (i, t, 0)),
                  _const_spec(wq_t.shape), _const_spec(wk.shape), _const_spec(wv_t.shape)],
        out_specs=[t_spec,
                   pl.BlockSpec((None, QKV_TILE, ATTN_WIDTH), lambda i, t: (i, t, 0)),
                   t_spec],
        out_shape=[t_shape, jax.ShapeDtypeStruct((b, s, ATTN_WIDTH), BF16), t_shape],
        compiler_params=_params(2),
        name="qkv_proj",
    )(x, wq_t, wk, wv_t)


def _split_bf16(x):
    parts = []
    rem = x
    for _ in range(PEN_SPLIT):
        part = rem.astype(BF16)
        parts.append(part)
        rem = rem - part.astype(F32)
    return parts


def _select_blocks(gate, i):
    nb = gate.shape[0]
    bidx = lax.broadcasted_iota(jnp.int32, gate.shape, 0)
    avail = jnp.where(bidx < i, 1, 0)
    chosen = jnp.zeros(gate.shape, jnp.int32)
    for _ in range(MOBA_TOPK):
        live = avail == 1
        top = jnp.max(jnp.where(live, gate, -jnp.inf), axis=0, keepdims=True)
        cand = jnp.where(live, jnp.where(gate == top, bidx, nb), nb)
        pick = jnp.where(bidx == jnp.min(cand, axis=0, keepdims=True), 1, 0)
        chosen = chosen + pick
        avail = avail - pick
    return chosen


def _attn_kernel(rb_ref, qt_ref, k_ref, vt_ref, bown_ref, bprev_ref, o_ref, *scratch):
    per_step = qt_ref.shape[0]

    def one_block(sub, carry):
        _attn_block(pl.program_id(1) * per_step + sub, rb_ref, qt_ref.at[sub], k_ref, vt_ref,
                    bown_ref, bprev_ref, o_ref.at[sub], *scratch)
        return carry

    lax.fori_loop(0, per_step, one_block, 0)


def _attn_block(i, rb_ref, qt_ref, k_ref, vt_ref, bown_ref, bprev_ref, o_ref,
                kmean_ref, kpad_ref, qa_ref, s_ref, m_ref, acc_ref):
    nb = vt_ref.shape[0]
    blk = MOBA_BLOCK

    @pl.when(i == 0)
    def _():
        pad_lane = lax.broadcasted_iota(jnp.int32, (blk, K_PAD - HEAD_DIM), 1)
        for j in range(nb):
            rows = slice(j * blk, (j + 1) * blk)
            kj = k_ref[rows, :]
            kmean_ref[j:j + 1, :] = jnp.sum(kj.astype(F32), axis=0, keepdims=True) * (1.0 / blk)
            hot = pad_lane == j
            for rep in range(1, PEN_SPLIT):
                hot = hot | (pad_lane == j + rep * nb)
            one_hot = jnp.where(hot, 1.0, 0.0).astype(BF16)
            for h in range(N_HEADS):
                kpad_ref[rows, h * K_PAD:(h + 1) * K_PAD] = jnp.concatenate(
                    [kj[:, h * HEAD_DIM:(h + 1) * HEAD_DIM], one_hot], axis=1)

    def put_scores(h, j, slot):
        start = pl.multiple_of(j * blk, blk)
        s_ref[slot, h] = jnp.dot(kpad_ref[pl.ds(start, blk), h * K_PAD:(h + 1) * K_PAD], qa_ref[h],
                                 preferred_element_type=F32)

    bidx = lax.broadcasted_iota(jnp.int32, (nb, blk), 0)
    gates = []
    for h in range(N_HEADS):
        q = qt_ref[h * HEAD_DIM:(h + 1) * HEAD_DIM, :]
        parts = _split_bf16(kmean_ref[:, h * HEAD_DIM:(h + 1) * HEAD_DIM])
        g3 = jnp.dot(jnp.concatenate(parts, axis=0), q, preferred_element_type=F32)
        gate = g3[0:nb]
        for r in range(1, PEN_SPLIT):
            gate = gate + g3[r * nb:(r + 1) * nb]
        gates.append(gate)
    for h in range(N_HEADS):
        q = qt_ref[h * HEAD_DIM:(h + 1) * HEAD_DIM, :]
        chosen = _select_blocks(gates[h], i)
        far_bias = jnp.where(bidx < i - 1, rb_ref[h, N_BUCKETS - 1] * LOG2E, 0.0)
        pen = jnp.where(bidx == i, 0.0, jnp.where(chosen == 1, far_bias, NEG))
        qa_ref[h] = jnp.concatenate(
            [q] + _split_bf16(pen)
            + [jnp.zeros((K_PAD - HEAD_DIM - PEN_SPLIT * nb, blk), BF16)], axis=0)
        put_scores(h, 0, 0)
        put_scores(h, jnp.minimum(1, i), 1)

    m_ref[...] = jnp.full(m_ref.shape, M_INIT, F32)
    acc_ref[...] = jnp.zeros(acc_ref.shape, F32)

    ones = jnp.ones((V_ONES, blk), BF16)

    def absorb(h, j, slot, kind):
        s = s_ref[slot, h]
        if kind == "own":
            s = s + bown_ref[h]
        elif kind == "prev":
            s = s + bprev_ref[h]
        m_old = m_ref[h]
        m_new = jnp.maximum(m_old, jnp.max(s, axis=0, keepdims=True))
        p = jnp.exp2(s - m_new).astype(BF16)
        va = jnp.concatenate([vt_ref[j, h * HEAD_DIM:(h + 1) * HEAD_DIM, :], ones], axis=0)
        acc_ref[h] = jnp.exp2(m_old - m_new) * acc_ref[h] + jnp.dot(va, p, preferred_element_type=F32)
        m_ref[h] = m_new

    def stage(j, slot, kind, fill=None):
        if fill is not None:
            for h in range(FILL_AHEAD):
                put_scores(h, fill[0], fill[1])
        for h in range(N_HEADS):
            if fill is not None and h + FILL_AHEAD < N_HEADS:
                put_scores(h + FILL_AHEAD, fill[0], fill[1])
            absorb(h, j, slot, kind)

    def triple_body(a, carry):
        j0 = 3 * a
        stage(j0, 0, "far", fill=(j0 + 2, 2))
        stage(j0 + 1, 1, "far", fill=(j0 + 3, 0))
        stage(j0 + 2, 2, "far", fill=(j0 + 4, 1))
        return carry

    n_far = jnp.maximum(i - 1, 0)
    n_triples = lax.div(n_far, 3)
    rest = n_far - 3 * n_triples
    lax.fori_loop(0, n_triples, triple_body, 0)

    @pl.when(i == 0)
    def _():
        stage(i, 0, "own")

    @pl.when(jnp.logical_and(i >= 1, rest == 0))
    def _():
        stage(i - 1, 0, "prev")
        stage(i, 1, "own")

    @pl.when(rest == 1)
    def _():
        stage(i - 2, 0, "far", fill=(i, 2))
        stage(i - 1, 1, "prev")
        stage(i, 2, "own")

    @pl.when(rest == 2)
    def _():
        stage(i - 3, 0, "far", fill=(i - 1, 2))
        stage(i - 2, 1, "far", fill=(i, 0))
        stage(i - 1, 2, "prev")
        stage(i, 0, "own")

    for h in range(N_HEADS):
        acc = acc_ref[h]
        o_ref[h * HEAD_DIM:(h + 1) * HEAD_DIM, :] = (
            acc[0:HEAD_DIM] / acc[HEAD_DIM:HEAD_DIM + 1]).astype(o_ref.dtype)


def _moba_attention(rel_bias, qt, k, vt, bias_own, bias_prev):
    b, nb, _, blk = qt.shape
    s = nb * blk
    q_tile = pl.BlockSpec((None, Q_BLOCKS_PER_STEP, ATTN_WIDTH, blk), lambda bi, t: (bi, t, 0, 0))
    return pl.pallas_call(
        _attn_kernel,
        grid=(b, nb // Q_BLOCKS_PER_STEP),
        in_specs=[pl.BlockSpec(memory_space=pltpu.SMEM),
                  q_tile,
                  pl.BlockSpec((None, s, ATTN_WIDTH), lambda bi, i: (bi, 0, 0)),
                  pl.BlockSpec((None, nb, ATTN_WIDTH, blk), lambda bi, i: (bi, 0, 0, 0)),
                  _const_spec(bias_own.shape), _const_spec(bias_prev.shape)],
        out_specs=q_tile,
        out_shape=jax.ShapeDtypeStruct((b, nb, ATTN_WIDTH, blk), BF16),
        scratch_shapes=[pltpu.VMEM((nb, ATTN_WIDTH), F32),
                        pltpu.VMEM((s, N_HEADS * K_PAD), BF16),
                        pltpu.VMEM((N_HEADS, K_PAD, blk), BF16),
                        pltpu.VMEM((SCORE_SLOTS, N_HEADS, blk, blk), F32),
                        pltpu.VMEM((N_HEADS, 1, blk), F32),
                        pltpu.VMEM((N_HEADS, HEAD_DIM + V_ONES, blk), F32)],
        compiler_params=_params(2),
        name="moba_attention",
    )(rel_bias, qt, k, vt, bias_own, bias_prev)


def _layer_norm(z, g, b):
    mu = jnp.mean(z, axis=-1, keepdims=True)
    zc = z - mu
    var = jnp.mean(zc * zc, axis=-1, keepdims=True)
    return zc * lax.rsqrt(var + LN_EPS) * g + b


def _mixer_kernel(alpha, x_ref, at_ref, wp_ref, wg_ref, wpool_ref, pscale_ref, wba_ref, wbp_ref,
                  wout_ref, g_ref, b_ref, h_ref, pbuf_ref):
    t = pl.program_id(1)
    tile = x_ref.shape[0]
    d = x_ref.shape[1]

    @pl.when(t == 0)
    def _():
        pbuf_ref[0:POOL_HALO, :] = jnp.zeros((POOL_HALO, POOL_WIDTH), F32)

    x = x_ref[...]
    xb = x.astype(BF16)
    p = jnp.dot(xb, wp_ref[...], preferred_element_type=F32)
    gate_logits = jnp.dot(xb, wg_ref[...], preferred_element_type=F32)
    y_attn = jnp.concatenate(
        [lax.dot_general(at_ref[c], wba_ref[...], TN_DIMS, preferred_element_type=F32)
         for c in range(at_ref.shape[0])], axis=0)

    pbuf_ref[POOL_HALO:POOL_HALO + tile, :] = p
    pos = t * tile + lax.broadcasted_iota(jnp.int32, (tile, POOL_GROUP), 0)
    ys = []
    for gi, w in enumerate(POOL_WINDOWS):
        cols = slice(gi * POOL_GROUP, (gi + 1) * POOL_GROUP)
        pg = p[:, cols]
        win = pg
        for back in range(1, w):
            win = win + pbuf_ref[POOL_HALO - back:POOL_HALO - back + tile, cols]
        cnt = jnp.minimum(pos + 1, w).astype(F32)
        diff = win / cnt - pg
        ys.append(jnp.dot(diff.astype(BF16), wpool_ref[gi], preferred_element_type=F32))
    pbuf_ref[0:POOL_HALO, :] = pbuf_ref[tile:tile + POOL_HALO, :]
    y = jnp.concatenate(ys, axis=1) * pscale_ref[...]
    y_pool = jnp.dot(y.astype(BF16), wbp_ref[...], preferred_element_type=F32)

    gates = 1.0 / (1.0 + jnp.exp(-gate_logits))
    mixed = (gates[:, :d] * y_attn + gates[:, d:] * y_pool).astype(BF16)
    for rows in (slice(0, tile // 2), slice(tile // 2, tile)):
        z = alpha * x[rows] + jnp.dot(mixed[rows], wout_ref[...], preferred_element_type=F32)
        h_ref[rows, :] = _layer_norm(z, g_ref[...], b_ref[...])


def _mixer(alpha, x, attn_t, wp, wg, wpool, pscale, wba, wbp, wout, ln_g, ln_b):
    b, s, d = x.shape
    bpt = TOKEN_TILE // MOBA_BLOCK
    consts = (wp, wg, wpool, pscale, wba, wbp, wout, ln_g, ln_b)
    return pl.pallas_call(
        functools.partial(_mixer_kernel, alpha),
        grid=(b, s // TOKEN_TILE),
        in_specs=[pl.BlockSpec((None, TOKEN_TILE, d), lambda i, t: (i, t, 0)),
                  pl.BlockSpec((None, bpt, ATTN_WIDTH, MOBA_BLOCK), lambda i, t: (i, t, 0, 0))]
                 + [_const_spec(c.shape) for c in consts],
        out_specs=pl.BlockSpec((None, TOKEN_TILE, d), lambda i, t: (i, t, 0)),
        out_shape=jax.ShapeDtypeStruct((b, s, d), F32),
        scratch_shapes=[pltpu.VMEM((POOL_HALO + TOKEN_TILE, POOL_WIDTH), F32)],
        compiler_params=_params(2),
        name="mixer",
    )(x, attn_t, *consts)


def _ffn_kernel(alpha, tiles_per_seq, n_tiles, h_ref, win_ref, cw_ref, cb_ref, wo_ref, g_ref, b_ref,
                o_ref, carry_ref, z_ref, act_ref):
    g = pl.program_id(0)
    tile = h_ref.shape[0]
    d_ff = wo_ref.shape[0]
    n_chunks = d_ff // FF_CHUNK

    @pl.when(g == 0)
    def _():
        z_ref[...] = jnp.zeros(z_ref.shape, F32)

    @pl.when(g % tiles_per_seq == 0)
    def _():
        carry_ref[...] = jnp.zeros(carry_ref.shape, F32)

    @pl.when(g == n_tiles)
    def _():
        o_ref[...] = _layer_norm(z_ref[...], g_ref[...], b_ref[...])

    @pl.when(g < n_tiles)
    def _():
        o_ref[...] = _layer_norm(z_ref[...], g_ref[...], b_ref[...])
        _ffn_tile(alpha, tile, n_chunks, d_ff, h_ref, win_ref, cw_ref, cb_ref, wo_ref, carry_ref, z_ref, act_ref)


def _ffn_tile(alpha, tile, n_chunks, d_ff, h_ref, win_ref, cw_ref, cb_ref, wo_ref, carry_ref, z_ref, act_ref):
    h = h_ref[...]
    hb = h.astype(BF16)
    row = lax.broadcasted_iota(jnp.int32, (CONV_HALO, FF_CHUNK), 0)

    def cols(c, base=0):
        return slice(base + c * FF_CHUNK, base + (c + 1) * FF_CHUNK)

    def project(c):
        return (jnp.dot(hb, win_ref[:, cols(c)], preferred_element_type=F32),
                jnp.dot(hb, win_ref[:, cols(c, d_ff)], preferred_element_type=F32))

    def shifted(a, prev, back):
        rolled = pltpu.roll(a, back, 0)
        top = rolled[0:CONV_HALO]
        for r in range(back):
            top = jnp.where(row == r, prev[CONV_HALO - back + r:CONV_HALO - back + r + 1, :], top)
        return jnp.concatenate([top, rolled[CONV_HALO:]], axis=0)

    nxt = project(0)
    for c in range(n_chunks):
        a, u = nxt
        if c + 1 < n_chunks:
            nxt = project(c + 1)
        prev = carry_ref[c]
        carry_ref[c] = a[tile - CONV_HALO:tile, :]
        cw = cw_ref[:, cols(c)]
        y = cb_ref[:, cols(c)]
        for tap in range(CONV_WIDTH):
            back = CONV_WIDTH - 1 - tap
            src = a if back == 0 else shifted(a, prev, back)
            y = y + src * cw[tap:tap + 1, :]
        act = 0.5 * y * (1.0 + lax.erf(y * math.sqrt(0.5))) * u
        act_ref[:, cols(c)] = act.astype(BF16)
    split = (n_chunks - 1) * FF_CHUNK
    f = (jnp.dot(act_ref[:, :split], wo_ref[:split, :], preferred_element_type=F32)
         + jnp.dot(act_ref[:, split:], wo_ref[split:, :], preferred_element_type=F32))
    z_ref[...] = alpha * h + f


def _conv_ffn(alpha, h, win, cw, cb, wo, ln_g, ln_b):
    b, s, d = h.shape
    d_ff = wo.shape[0]
    tiles_per_seq = s // FFN_TILE
    n_tiles = b * tiles_per_seq
    consts = (win, cw, cb, wo, ln_g, ln_b)

    def tile_index(g):
        return (g // tiles_per_seq, g % tiles_per_seq, 0)

    return pl.pallas_call(
        functools.partial(_ffn_kernel, alpha, tiles_per_seq, n_tiles),
        grid=(n_tiles + 1,),
        in_specs=[pl.BlockSpec((None, FFN_TILE, d), lambda g: tile_index(jnp.minimum(g, n_tiles - 1)))]
                 + [_const_spec(c.shape) for c in consts],
        out_specs=pl.BlockSpec((None, FFN_TILE, d), lambda g: tile_index(jnp.maximum(g - 1, 0))),
        out_shape=jax.ShapeDtypeStruct((b, s, d), F32),
        scratch_shapes=[pltpu.VMEM((d_ff // FF_CHUNK, CONV_HALO, FF_CHUNK), F32),
                        pltpu.VMEM((FFN_TILE, d), F32),
                        pltpu.VMEM((FFN_TILE, d_ff), BF16)],
        compiler_params=_params(1),
        name="conv_ffn",
    )(h, *consts)


def kernel(x, w_in, rel_bias, w_pool_group, pool_scale, w_branch_attn, w_branch_pool, w_out, ln1_g,
           ln1_b, w_ffn_in, conv_w, conv_b, w_ffn_out, ln2_g, ln2_b):
    b, s, d = x.shape
    depth = w_in.shape[0]
    d_ff = w_ffn_out.shape[1]
    assert s % TOKEN_TILE == 0 and TOKEN_TILE % MOBA_BLOCK == 0 and d_ff % FF_CHUNK == 0
    assert s % FFN_TILE == 0 and s % QKV_TILE == 0 and QKV_TILE % MOBA_BLOCK == 0
    assert s % (MOBA_BLOCK * Q_BLOCKS_PER_STEP) == 0
    assert w_in.shape[2] == 3 * ATTN_WIDTH + POOL_WIDTH + 2 * d
    alpha = (2.0 * depth) ** 0.25
    assert (K_PAD - HEAD_DIM - PEN_SPLIT * (s // MOBA_BLOCK)) % BF16_SUBLANES == 0
    assert K_PAD - HEAD_DIM - PEN_SPLIT * (s // MOBA_BLOCK) > 0
    scale = LOG2E / math.sqrt(HEAD_DIM)

    bias_own, bias_prev = _rel_bias_tables(rel_bias)
    h = x
    for l in range(depth):
        w = w_in[l]
        wq_t = (w[:, :ATTN_WIDTH] * scale).T.astype(BF16)
        wk = w[:, ATTN_WIDTH:2 * ATTN_WIDTH].astype(BF16)
        wv_t = w[:, 2 * ATTN_WIDTH:3 * ATTN_WIDTH].T.astype(BF16)
        qt, k, vt = _qkv_proj(h, wq_t, wk, wv_t)
        attn_t = _moba_attention(rel_bias, qt, k, vt, bias_own, bias_prev)
        h1 = _mixer(alpha, h, attn_t,
                    w[:, 3 * ATTN_WIDTH:3 * ATTN_WIDTH + POOL_WIDTH].astype(BF16),
                    w[:, 3 * ATTN_WIDTH + POOL_WIDTH:].astype(BF16),
                    w_pool_group[l].astype(BF16),
                    pool_scale[l].reshape(1, POOL_WIDTH),
                    w_branch_attn[l].astype(BF16),
                    w_branch_pool[l].astype(BF16),
                    w_out[l].astype(BF16),
                    ln1_g[l].reshape(1, d), ln1_b[l].reshape(1, d))
        h = _conv_ffn(alpha, h1, w_ffn_in[l].astype(BF16), conv_w[l], conv_b[l].reshape(1, d_ff),
                      w_ffn_out[l].astype(BF16), ln2_g[l].reshape(1, d), ln2_b[l].reshape(1, d))
    return h
```

```python
import functools
import math

import jax
import jax.numpy as jnp
from jax import lax
from jax.experimental import pallas as pl
from jax.experimental.pallas import tpu as pltpu

N_HEADS = 8
HEAD_DIM = 64
ATTN_WIDTH = N_HEADS * HEAD_DIM
MOBA_BLOCK = 256
MOBA_TOPK = 3
POOL_WINDOWS = (2, 4, 8, 16)
POOL_GROUP = 128
POOL_WIDTH = POOL_GROUP * len(POOL_WINDOWS)
N_BUCKETS = 32
MAX_DISTANCE = 128
CONV_WIDTH = 3
LN_EPS = 1e-5
NEG = -1e30
M_INIT = 0.5 * NEG
LOG2E = math.log2(math.e)
FILL_AHEAD = 2
PEN_SPLIT = 3

LANES = 128
K_PAD = LANES
BF16_SUBLANES = 16
MXU_TILE = 256
V_ONES = BF16_SUBLANES
SCORE_SLOTS = 3
POOL_HALO = max(POOL_WINDOWS)
CONV_HALO = 8
FF_CHUNK = MXU_TILE
TOKEN_TILE = 1024
QKV_TILE = 1024
FFN_TILE = 1024
Q_BLOCKS_PER_STEP = 4
VMEM_LIMIT = 56 * 1024 * 1024

F32 = jnp.float32
BF16 = jnp.bfloat16
NT_DIMS = (((1,), (1,)), ((), ()))
TN_DIMS = (((0,), (0,)), ((), ()))


def _params(n_grid):
    return pltpu.CompilerParams(dimension_semantics=("arbitrary",) * n_grid,
                                vmem_limit_bytes=VMEM_LIMIT)


def _const_spec(shape):
    nd = len(shape)
    return pl.BlockSpec(shape, lambda *_: (0,) * nd)


def _rel_bias_kernel(rb_ref, own_ref, prev_ref):
    h = pl.program_id(0)
    kk = lax.broadcasted_iota(jnp.int32, (MOBA_BLOCK, MOBA_BLOCK), 0)
    qq = lax.broadcasted_iota(jnp.int32, (MOBA_BLOCK, MOBA_BLOCK), 1)
    max_exact = N_BUCKETS // 2

    def table(dist):
        n = jnp.maximum(dist, 0)
        nf = jnp.maximum(n, 1).astype(F32)
        large = max_exact + (jnp.log(nf / max_exact) / math.log(MAX_DISTANCE / max_exact)
                             * (N_BUCKETS - max_exact)).astype(jnp.int32)
        large = jnp.minimum(large, N_BUCKETS - 1)
        bucket = jnp.where(n < max_exact, n, large)
        out = jnp.zeros(dist.shape, F32)
        for b in range(N_BUCKETS):
            out = jnp.where(bucket == b, rb_ref[h, b], out)
        return out

    own_ref[...] = jnp.where(qq >= kk, table(qq - kk) * LOG2E, NEG)
    prev_ref[...] = table(qq - kk + MOBA_BLOCK) * LOG2E


def _rel_bias_tables(rel_bias):
    shape = jax.ShapeDtypeStruct((N_HEADS, MOBA_BLOCK, MOBA_BLOCK), F32)
    spec = pl.BlockSpec((None, MOBA_BLOCK, MOBA_BLOCK), lambda h: (h, 0, 0))
    return pl.pallas_call(
        _rel_bias_kernel,
        grid=(N_HEADS,),
        in_specs=[pl.BlockSpec(memory_space=pltpu.SMEM)],
        out_specs=[spec, spec],
        out_shape=[shape, shape],
        compiler_params=_params(1),
        name="rel_bias_tables",
    )(rel_bias)


def _qkv_kernel(x_ref, wq_ref, wk_ref, wv_ref, qt_ref, k_ref, vt_ref):
    xb = x_ref[...].astype(BF16)
    qt = lax.dot_general(wq_ref[...], xb, NT_DIMS, preferred_element_type=F32)
    vt = lax.dot_general(wv_ref[...], xb, NT_DIMS, preferred_element_type=F32)
    k_ref[...] = jnp.dot(xb, wk_ref[...], preferred_element_type=F32).astype(BF16)
    for c in range(qt_ref.shape[0]):
        cols = slice(c * MOBA_BLOCK, (c + 1) * MOBA_BLOCK)
        qt_ref[c] = qt[:, cols].astype(BF16)
        vt_ref[c] = vt[:, cols].astype(BF16)


def _qkv_proj(x, wq_t, wk, wv_t):
    b, s, d = x.shape
    nb = s // MOBA_BLOCK
    bpt = QKV_TILE // MOBA_BLOCK
    t_shape = jax.ShapeDtypeStruct((b, nb, ATTN_WIDTH, MOBA_BLOCK), BF16)
    t_spec = pl.BlockSpec((None, bpt, ATTN_WIDTH, MOBA_BLOCK), lambda i, t: (i, t, 0, 0))
    return pl.pallas_call(
        _qkv_kernel,
        grid=(b, s // QKV_TILE),
        in_specs=[pl.BlockSpec((None, QKV_TILE, d), lambda i, t: (i, t, 0)),
                  _const_spec(wq_t.shape), _const_spec(wk.shape), _const_spec(wv_t.shape)],
        out_specs=[t_spec,
                   pl.BlockSpec((None, QKV_TILE, ATTN_WIDTH), lambda i, t: (i, t, 0)),
                   t_spec],
        out_shape=[t_shape, jax.ShapeDtypeStruct((b, s, ATTN_WIDTH), BF16), t_shape],
        compiler_params=_params(2),
        name="qkv_proj",
    )(x, wq_t, wk, wv_t)


def _split_bf16(x):
    parts = []
    rem = x
    for _ in range(PEN_SPLIT):
        part = rem.astype(BF16)
        parts.append(part)
        rem = rem - part.astype(F32)
    return parts


def _select_blocks(gate, i):
    nb = gate.shape[0]
    bidx = lax.broadcasted_iota(jnp.int32, gate.shape, 0)
    avail = jnp.where(bidx < i, 1, 0)
    chosen = jnp.zeros(gate.shape, jnp.int32)
    for _ in range(MOBA_TOPK):
        live = avail == 1
        top = jnp.max(jnp.where(live, gate, -jnp.inf), axis=0, keepdims=True)
        cand = jnp.where(live, jnp.where(gate == top, bidx, nb), nb)
        pick = jnp.where(bidx == jnp.min(cand, axis=0, keepdims=True), 1, 0)
        chosen = chosen + pick
        avail = avail - pick
    return chosen


N0, N1 = SCORE_SLOTS, SCORE_SLOTS + 1


def _attn_kernel(rb_ref, qt_ref, qnext_ref, k_ref, vt_ref, bown_ref, bprev_ref, o_ref,
                 kmean_ref, kpad_ref, qa_ref, s_ref, m_ref, acc_ref):
    i = pl.program_id(1)
    nb = vt_ref.shape[0]
    blk = MOBA_BLOCK
    cur = i & 1
    nxt = 1 - cur
    bidx = lax.broadcasted_iota(jnp.int32, (nb, blk), 0)

    def put_scores(h, j, slot, half):
        start = pl.multiple_of(j * blk, blk)
        s_ref[slot, h] = jnp.dot(kpad_ref[pl.ds(start, blk), h * K_PAD:(h + 1) * K_PAD],
                                 qa_ref[half, h], preferred_element_type=F32)

    def write_operands(q_ref, tile, half):
        gates = []
        for h in range(N_HEADS):
            q = q_ref[h * HEAD_DIM:(h + 1) * HEAD_DIM, :]
            parts = _split_bf16(kmean_ref[:, h * HEAD_DIM:(h + 1) * HEAD_DIM])
            g3 = jnp.dot(jnp.concatenate(parts, axis=0), q, preferred_element_type=F32)
            gate = g3[0:nb]
            for r in range(1, PEN_SPLIT):
                gate = gate + g3[r * nb:(r + 1) * nb]
            gates.append(gate)
        for h in range(N_HEADS):
            chosen = _select_blocks(gates[h], tile)
            far_bias = jnp.where(bidx < tile - 1, rb_ref[h, N_BUCKETS - 1] * LOG2E, 0.0)
            pen = jnp.where(bidx == tile, 0.0, jnp.where(chosen == 1, far_bias, NEG))
            qa_ref[half, h] = jnp.concatenate(
                [q_ref[h * HEAD_DIM:(h + 1) * HEAD_DIM, :]] + _split_bf16(pen)
                + [jnp.zeros((K_PAD - HEAD_DIM - PEN_SPLIT * nb, blk), BF16)], axis=0)

    @pl.when(i == 0)
    def _():
        pad_lane = lax.broadcasted_iota(jnp.int32, (blk, K_PAD - HEAD_DIM), 1)
        for j in range(nb):
            rows = slice(j * blk, (j + 1) * blk)
            kj = k_ref[rows, :]
            kmean_ref[j:j + 1, :] = jnp.sum(kj.astype(F32), axis=0, keepdims=True) * (1.0 / blk)
            hot = pad_lane == j
            for rep in range(1, PEN_SPLIT):
                hot = hot | (pad_lane == j + rep * nb)
            one_hot = jnp.where(hot, 1.0, 0.0).astype(BF16)
            for h in range(N_HEADS):
                kpad_ref[rows, h * K_PAD:(h + 1) * K_PAD] = jnp.concatenate(
                    [kj[:, h * HEAD_DIM:(h + 1) * HEAD_DIM], one_hot], axis=1)
        write_operands(qt_ref, 0, 0)
        for h in range(N_HEADS):
            put_scores(h, 0, N0, 0)

    write_operands(qnext_ref, i + 1, nxt)

    m_ref[...] = jnp.full(m_ref.shape, M_INIT, F32)
    acc_ref[...] = jnp.zeros(acc_ref.shape, F32)

    ones = jnp.ones((V_ONES, blk), BF16)

    def absorb(h, j, slot, kind):
        s = s_ref[slot, h]
        if kind == "own":
            s = s + bown_ref[h]
        elif kind == "prev":
            s = s + bprev_ref[h]
        m_old = m_ref[h]
        m_new = jnp.maximum(m_old, jnp.max(s, axis=0, keepdims=True))
        p = jnp.exp2(s - m_new).astype(BF16)
        va = jnp.concatenate([vt_ref[j, h * HEAD_DIM:(h + 1) * HEAD_DIM, :], ones], axis=0)
        acc_ref[h] = jnp.exp2(m_old - m_new) * acc_ref[h] + jnp.dot(va, p, preferred_element_type=F32)
        m_ref[h] = m_new

    def stage(j, slot, kind, fill=None):
        if fill is not None:
            for h in range(FILL_AHEAD):
                put_scores(h, *fill)
        for h in range(N_HEADS):
            if fill is not None and h + FILL_AHEAD < N_HEADS:
                put_scores(h + FILL_AHEAD, *fill)
            absorb(h, j, slot, kind)

    next0, next1 = (0, N0, nxt), (1, N1, nxt)

    @pl.when(i == 0)
    def _():
        stage(0, N0, "own")
        for h in range(N_HEADS):
            put_scores(h, *next0)
            put_scores(h, *next1)

    @pl.when(i == 1)
    def _():
        stage(0, N0, "prev")
        stage(1, N1, "own", fill=next0)
        for h in range(N_HEADS):
            put_scores(h, *next1)

    @pl.when(i == 2)
    def _():
        stage(0, N0, "far", fill=(2, 2, cur))
        stage(1, N1, "prev", fill=next0)
        stage(2, 2, "own", fill=next1)

    @pl.when(i >= 3)
    def _():
        stage(0, N0, "far", fill=(2, 2, cur))
        stage(1, N1, "far", fill=(3, 0, cur))

    def triple_body(a, carry):
        j0 = 2 + 3 * a
        stage(j0, 2, "far", fill=(j0 + 2, 1, cur))
        stage(j0 + 1, 0, "far", fill=(j0 + 3, 2, cur))
        stage(j0 + 2, 1, "far", fill=(j0 + 4, 0, cur))
        return carry

    n_triples = lax.div(jnp.maximum(i - 3, 0), 3)
    lax.fori_loop(0, n_triples, triple_body, 0)
    rest = jnp.where(i >= 3, i - 1 - 3 * n_triples, 0)

    @pl.when(rest == 2)
    def _():
        stage(i - 1, 2, "prev", fill=next0)
        stage(i, 0, "own", fill=next1)

    @pl.when(rest == 3)
    def _():
        stage(i - 2, 2, "far", fill=(i, 1, cur))
        stage(i - 1, 0, "prev", fill=next0)
        stage(i, 1, "own", fill=next1)

    @pl.when(rest == 4)
    def _():
        stage(i - 3, 2, "far", fill=(i - 1, 1, cur))
        stage(i - 2, 0, "far", fill=(i, 2, cur))
        stage(i - 1, 1, "prev", fill=next0)
        stage(i, 2, "own", fill=next1)

    for h in range(N_HEADS):
        acc = acc_ref[h]
        o_ref[h * HEAD_DIM:(h + 1) * HEAD_DIM, :] = (
            acc[0:HEAD_DIM] / acc[HEAD_DIM:HEAD_DIM + 1]).astype(o_ref.dtype)


def _moba_attention(rel_bias, qt, k, vt, bias_own, bias_prev):
    b, nb, _, blk = qt.shape
    s = nb * blk
    q_tile = pl.BlockSpec((None, None, ATTN_WIDTH, blk), lambda bi, i: (bi, i, 0, 0))
    return pl.pallas_call(
        _attn_kernel,
        grid=(b, nb),
        in_specs=[pl.BlockSpec(memory_space=pltpu.SMEM),
                  q_tile,
                  pl.BlockSpec((None, None, ATTN_WIDTH, blk),
                               lambda bi, i: (bi, jnp.minimum(i + 1, nb - 1), 0, 0)),
                  pl.BlockSpec((None, s, ATTN_WIDTH), lambda bi, i: (bi, 0, 0)),
                  pl.BlockSpec((None, nb, ATTN_WIDTH, blk), lambda bi, i: (bi, 0, 0, 0)),
                  _const_spec(bias_own.shape), _const_spec(bias_prev.shape)],
        out_specs=q_tile,
        out_shape=jax.ShapeDtypeStruct((b, nb, ATTN_WIDTH, blk), BF16),
        scratch_shapes=[pltpu.VMEM((nb, ATTN_WIDTH), F32),
                        pltpu.VMEM((s, N_HEADS * K_PAD), BF16),
                        pltpu.VMEM((2, N_HEADS, K_PAD, blk), BF16),
                        pltpu.VMEM((SCORE_SLOTS + 2, N_HEADS, blk, blk), F32),
                        pltpu.VMEM((N_HEADS, 1, blk), F32),
                        pltpu.VMEM((N_HEADS, HEAD_DIM + V_ONES, blk), F32)],
        compiler_params=_params(2),
        name="moba_attention",
    )(rel_bias, qt, qt, k, vt, bias_own, bias_prev)


def _layer_norm(z, g, b):
    mu = jnp.mean(z, axis=-1, keepdims=True)
    zc = z - mu
    var = jnp.mean(zc * zc, axis=-1, keepdims=True)
    return zc * lax.rsqrt(var + LN_EPS) * g + b


def _mixer_kernel(alpha, x_ref, at_ref, wp_ref, wg_ref, wpool_ref, pscale_ref, wba_ref, wbp_ref,
                  wout_ref, g_ref, b_ref, h_ref, pbuf_ref):
    t = pl.program_id(1)
    tile = x_ref.shape[0]
    d = x_ref.shape[1]

    @pl.when(t == 0)
    def _():
        pbuf_ref[0:POOL_HALO, :] = jnp.zeros((POOL_HALO, POOL_WIDTH), F32)

    x = x_ref[...]
    xb = x.astype(BF16)
    p = jnp.dot(xb, wp_ref[...], preferred_element_type=F32)
    gate_logits = jnp.dot(xb, wg_ref[...], preferred_element_type=F32)
    y_attn = jnp.concatenate(
        [lax.dot_general(at_ref[c], wba_ref[...], TN_DIMS, preferred_element_type=F32)
         for c in range(at_ref.shape[0])], axis=0)

    pbuf_ref[POOL_HALO:POOL_HALO + tile, :] = p
    pos = t * tile + lax.broadcasted_iota(jnp.int32, (tile, POOL_GROUP), 0)
    ys = []
    for gi, w in enumerate(POOL_WINDOWS):
        cols = slice(gi * POOL_GROUP, (gi + 1) * POOL_GROUP)
        pg = p[:, cols]
        win = pg
        for back in range(1, w):
            win = win + pbuf_ref[POOL_HALO - back:POOL_HALO - back + tile, cols]
        cnt = jnp.minimum(pos + 1, w).astype(F32)
        diff = win / cnt - pg
        ys.append(jnp.dot(diff.astype(BF16), wpool_ref[gi], preferred_element_type=F32))
    pbuf_ref[0:POOL_HALO, :] = pbuf_ref[tile:tile + POOL_HALO, :]
    y = jnp.concatenate(ys, axis=1) * pscale_ref[...]
    y_pool = jnp.dot(y.astype(BF16), wbp_ref[...], preferred_element_type=F32)

    gates = 1.0 / (1.0 + jnp.exp(-gate_logits))
    mixed = (gates[:, :d] * y_attn + gates[:, d:] * y_pool).astype(BF16)
    for rows in (slice(0, tile // 2), slice(tile // 2, tile)):
        z = alpha * x[rows] + jnp.dot(mixed[rows], wout_ref[...], preferred_element_type=F32)
        h_ref[rows, :] = _layer_norm(z, g_ref[...], b_ref[...])


def _mixer(alpha, x, attn_t, wp, wg, wpool, pscale, wba, wbp, wout, ln_g, ln_b):
    b, s, d = x.shape
    bpt = TOKEN_TILE // MOBA_BLOCK
    consts = (wp, wg, wpool, pscale, wba, wbp, wout, ln_g, ln_b)
    return pl.pallas_call(
        functools.partial(_mixer_kernel, alpha),
        grid=(b, s // TOKEN_TILE),
        in_specs=[pl.BlockSpec((None, TOKEN_TILE, d), lambda i, t: (i, t, 0)),
                  pl.BlockSpec((None, bpt, ATTN_WIDTH, MOBA_BLOCK), lambda i, t: (i, t, 0, 0))]
                 + [_const_spec(c.shape) for c in consts],
        out_specs=pl.BlockSpec((None, TOKEN_TILE, d), lambda i, t: (i, t, 0)),
        out_shape=jax.ShapeDtypeStruct((b, s, d), F32),
        scratch_shapes=[pltpu.VMEM((POOL_HALO + TOKEN_TILE, POOL_WIDTH), F32)],
        compiler_params=_params(2),
        name="mixer",
    )(x, attn_t, *consts)


def _ffn_kernel(alpha, tiles_per_seq, n_tiles, h_ref, win_ref, cw_ref, cb_ref, wo_ref, g_ref, b_ref,
                o_ref, carry_ref, z_ref, act_ref):
    g = pl.program_id(0)
    tile = h_ref.shape[0]
    d_ff = wo_ref.shape[0]
    n_chunks = d_ff // FF_CHUNK

    @pl.when(g == 0)
    def _():
        z_ref[...] = jnp.zeros(z_ref.shape, F32)

    @pl.when(g % tiles_per_seq == 0)
    def _():
        carry_ref[...] = jnp.zeros(carry_ref.shape, F32)

    @pl.when(g == n_tiles)
    def _():
        o_ref[...] = _layer_norm(z_ref[...], g_ref[...], b_ref[...])

    @pl.when(g < n_tiles)
    def _():
        o_ref[...] = _layer_norm(z_ref[...], g_ref[...], b_ref[...])
        _ffn_tile(alpha, tile, n_chunks, d_ff, h_ref, win_ref, cw_ref, cb_ref, wo_ref, carry_ref, z_ref, act_ref)


def _ffn_tile(alpha, tile, n_chunks, d_ff, h_ref, win_ref, cw_ref, cb_ref, wo_ref, carry_ref, z_ref, act_ref):
    h = h_ref[...]
    hb = h.astype(BF16)
    row = lax.broadcasted_iota(jnp.int32, (CONV_HALO, FF_CHUNK), 0)

    def cols(c, base=0):
        return slice(base + c * FF_CHUNK, base + (c + 1) * FF_CHUNK)

    def project(c):
        return (jnp.dot(hb, win_ref[:, cols(c)], preferred_element_type=F32),
                jnp.dot(hb, win_ref[:, cols(c, d_ff)], preferred_element_type=F32))

    def shifted(a, prev, back):
        rolled = pltpu.roll(a, back, 0)
        top = rolled[0:CONV_HALO]
        for r in range(back):
            top = jnp.where(row == r, prev[CONV_HALO - back + r:CONV_HALO - back + r + 1, :], top)
        return jnp.concatenate([top, rolled[CONV_HALO:]], axis=0)

    nxt = project(0)
    for c in range(n_chunks):
        a, u = nxt
        if c + 1 < n_chunks:
            nxt = project(c + 1)
        prev = carry_ref[c]
        carry_ref[c] = a[tile - CONV_HALO:tile, :]
        cw = cw_ref[:, cols(c)]
        y = cb_ref[:, cols(c)]
        for tap in range(CONV_WIDTH):
            back = CONV_WIDTH - 1 - tap
            src = a if back == 0 else shifted(a, prev, back)
            y = y + src * cw[tap:tap + 1, :]
        act = 0.5 * y * (1.0 + lax.erf(y * math.sqrt(0.5))) * u
        act_ref[:, cols(c)] = act.astype(BF16)
    split = (n_chunks - 1) * FF_CHUNK
    f = (jnp.dot(act_ref[:, :split], wo_ref[:split, :], preferred_element_type=F32)
         + jnp.dot(act_ref[:, split:], wo_ref[split:, :], preferred_element_type=F32))
    z_ref[...] = alpha * h + f


def _conv_ffn(alpha, h, win, cw, cb, wo, ln_g, ln_b):
    b, s, d = h.shape
    d_ff = wo.shape[0]
    tiles_per_seq = s // FFN_TILE
    n_tiles = b * tiles_per_seq
    consts = (win, cw, cb, wo, ln_g, ln_b)

    def tile_index(g):
        return (g // tiles_per_seq, g % tiles_per_seq, 0)

    return pl.pallas_call(
        functools.partial(_ffn_kernel, alpha, tiles_per_seq, n_tiles),
        grid=(n_tiles + 1,),
        in_specs=[pl.BlockSpec((None, FFN_TILE, d), lambda g: tile_index(jnp.minimum(g, n_tiles - 1)))]
                 + [_const_spec(c.shape) for c in consts],
        out_specs=pl.BlockSpec((None, FFN_TILE, d), lambda g: tile_index(jnp.maximum(g - 1, 0))),
        out_shape=jax.ShapeDtypeStruct((b, s, d), F32),
        scratch_shapes=[pltpu.VMEM((d_ff // FF_CHUNK, CONV_HALO, FF_CHUNK), F32),
                        pltpu.VMEM((FFN_TILE, d), F32),
                        pltpu.VMEM((FFN_TILE, d_ff), BF16)],
        compiler_params=_params(1),
        name="conv_ffn",
    )(h, *consts)


def kernel(x, w_in, rel_bias, w_pool_group, pool_scale, w_branch_attn, w_branch_pool, w_out, ln1_g,
           ln1_b, w_ffn_in, conv_w, conv_b, w_ffn_out, ln2_g, ln2_b):
    b, s, d = x.shape
    depth = w_in.shape[0]
    d_ff = w_ffn_out.shape[1]
    assert s % TOKEN_TILE == 0 and TOKEN_TILE % MOBA_BLOCK == 0 and d_ff % FF_CHUNK == 0
    assert s % FFN_TILE == 0 and s % QKV_TILE == 0 and QKV_TILE % MOBA_BLOCK == 0
    assert s % (MOBA_BLOCK * Q_BLOCKS_PER_STEP) == 0
    assert w_in.shape[2] == 3 * ATTN_WIDTH + POOL_WIDTH + 2 * d
    alpha = (2.0 * depth) ** 0.25
    assert (K_PAD - HEAD_DIM - PEN_SPLIT * (s // MOBA_BLOCK)) % BF16_SUBLANES == 0
    assert K_PAD - HEAD_DIM - PEN_SPLIT * (s // MOBA_BLOCK) > 0
    scale = LOG2E / math.sqrt(HEAD_DIM)

    bias_own, bias_prev = _rel_bias_tables(rel_bias)
    h = x
    for l in range(depth):
        w = w_in[l]
        wq_t = (w[:, :ATTN_WIDTH] * scale).T.astype(BF16)
        wk = w[:, ATTN_WIDTH:2 * ATTN_WIDTH].astype(BF16)
        wv_t = w[:, 2 * ATTN_WIDTH:3 * ATTN_WIDTH].T.astype(BF16)
        qt, k, vt = _qkv_proj(h, wq_t, wk, wv_t)
        attn_t = _moba_attention(rel_bias, qt, k, vt, bias_own, bias_prev)
        h1 = _mixer(alpha, h, attn_t,
                    w[:, 3 * ATTN_WIDTH:3 * ATTN_WIDTH + POOL_WIDTH].astype(BF16),
                    w[:, 3 * ATTN_WIDTH + POOL_WIDTH:].astype(BF16),
                    w_pool_group[l].astype(BF16),
                    pool_scale[l].reshape(1, POOL_WIDTH),
                    w_branch_attn[l].astype(BF16),
                    w_branch_pool[l].astype(BF16),
                    w_out[l].astype(BF16),
                    ln1_g[l].reshape(1, d), ln1_b[l].reshape(1, d))
        h = _conv_ffn(alpha, h1, w_ffn_in[l].astype(BF16), conv_w[l], conv_b[l].reshape(1, d_ff),
                      w_ffn_out[l].astype(BF16), ln2_g[l].reshape(1, d), ln2_b[l].reshape(1, d))
    return h
```

```python
import functools
import math

import jax
import jax.numpy as jnp
from jax import lax
from jax.experimental import pallas as pl
from jax.experimental.pallas import tpu as pltpu

N_HEADS = 8
HEAD_DIM = 64
ATTN_WIDTH = N_HEADS * HEAD_DIM
MOBA_BLOCK = 256
MOBA_TOPK = 3
POOL_WINDOWS = (2, 4, 8, 16)
POOL_GROUP = 128
POOL_WIDTH = POOL_GROUP * len(POOL_WINDOWS)
N_BUCKETS = 32
MAX_DISTANCE = 128
CONV_WIDTH = 3
LN_EPS = 1e-5
NEG = -1e30
M_INIT = 0.5 * NEG
LOG2E = math.log2(math.e)
FILL_AHEAD = 2
PEN_SPLIT = 3

LANES = 128
K_PAD = LANES
BF16_SUBLANES = 16
MXU_TILE = 256
V_ONES = BF16_SUBLANES
SCORE_SLOTS = 3
POOL_HALO = max(POOL_WINDOWS)
CONV_HALO = 8
FF_CHUNK = MXU_TILE
TOKEN_TILE = 1024
QKV_TILE = 1024
FFN_TILE = 1024
VMEM_LIMIT = 56 * 1024 * 1024

F32 = jnp.float32
BF16 = jnp.bfloat16
NT_DIMS = (((1,), (1,)), ((), ()))
TN_DIMS = (((0,), (0,)), ((), ()))


def _params(n_grid):
    return pltpu.CompilerParams(dimension_semantics=("arbitrary",) * n_grid,
                                vmem_limit_bytes=VMEM_LIMIT)


def _const_spec(shape):
    nd = len(shape)
    return pl.BlockSpec(shape, lambda *_: (0,) * nd)


def _rel_bias_kernel(rb_ref, own_ref, prev_ref):
    h = pl.program_id(0)
    kk = lax.broadcasted_iota(jnp.int32, (MOBA_BLOCK, MOBA_BLOCK), 0)
    qq = lax.broadcasted_iota(jnp.int32, (MOBA_BLOCK, MOBA_BLOCK), 1)
    max_exact = N_BUCKETS // 2
    n_log = N_BUCKETS - max_exact
    first_n = [math.ceil(max_exact * (MAX_DISTANCE / max_exact) ** (k / n_log)) for k in range(n_log)]

    def table(dist):
        n = jnp.maximum(dist, 0)
        out = jnp.zeros(dist.shape, F32)
        for b in range(max_exact):
            out = jnp.where(n == b, rb_ref[h, b], out)
        for k in range(n_log):
            out = jnp.where(n >= first_n[k], rb_ref[h, max_exact + k], out)
        return out

    own_ref[...] = jnp.where(qq >= kk, table(qq - kk) * LOG2E, NEG)
    prev_ref[...] = table(qq - kk + MOBA_BLOCK) * LOG2E


def _rel_bias_tables(rel_bias):
    shape = jax.ShapeDtypeStruct((N_HEADS, MOBA_BLOCK, MOBA_BLOCK), F32)
    spec = pl.BlockSpec((None, MOBA_BLOCK, MOBA_BLOCK), lambda h: (h, 0, 0))
    return pl.pallas_call(
        _rel_bias_kernel,
        grid=(N_HEADS,),
        in_specs=[pl.BlockSpec(memory_space=pltpu.SMEM)],
        out_specs=[spec, spec],
        out_shape=[shape, shape],
        compiler_params=_params(1),
        name="rel_bias_tables",
    )(rel_bias)


def _qkv_kernel(x_ref, wq_ref, wk_ref, wv_ref, qt_ref, k_ref, vt_ref):
    xb = x_ref[...].astype(BF16)
    qt = lax.dot_general(wq_ref[...], xb, NT_DIMS, preferred_element_type=F32)
    vt = lax.dot_general(wv_ref[...], xb, NT_DIMS, preferred_element_type=F32)
    k_ref[...] = jnp.dot(xb, wk_ref[...], preferred_element_type=F32).astype(BF16)
    for c in range(qt_ref.shape[0]):
        cols = slice(c * MOBA_BLOCK, (c + 1) * MOBA_BLOCK)
        qt_ref[c] = qt[:, cols].astype(BF16)
        vt_ref[c] = vt[:, cols].astype(BF16)


def _qkv_proj(x, wq_t, wk, wv_t):
    b, s, d = x.shape
    nb = s // MOBA_BLOCK
    bpt = QKV_TILE // MOBA_BLOCK
    t_shape = jax.ShapeDtypeStruct((b, nb, ATTN_WIDTH, MOBA_BLOCK), BF16)
    t_spec = pl.BlockSpec((None, bpt, ATTN_WIDTH, MOBA_BLOCK), lambda i, t: (i, t, 0, 0))
    return pl.pallas_call(
        _qkv_kernel,
        grid=(b, s // QKV_TILE),
        in_specs=[pl.BlockSpec((None, QKV_TILE, d), lambda i, t: (i, t, 0)),
                  _const_spec(wq_t.shape), _const_spec(wk.shape), _const_spec(wv_t.shape)],
        out_specs=[t_spec,
                   pl.BlockSpec((None, QKV_TILE, ATTN_WIDTH), lambda i, t: (i, t, 0)),
                   t_spec],
        out_shape=[t_shape, jax.ShapeDtypeStruct((b, s, ATTN_WIDTH), BF16), t_shape],
        compiler_params=_params(2),
        name="qkv_proj",
    )(x, wq_t, wk, wv_t)


def _split_bf16(x):
    parts = []
    rem = x
    for _ in range(PEN_SPLIT):
        part = rem.astype(BF16)
        parts.append(part)
        rem = rem - part.astype(F32)
    return parts


def _select_blocks(gate, i):
    nb = gate.shape[0]
    bidx = lax.broadcasted_iota(jnp.int32, gate.shape, 0)
    avail = jnp.where(bidx < i, 1, 0)
    chosen = jnp.zeros(gate.shape, jnp.int32)
    for _ in range(MOBA_TOPK):
        live = avail == 1
        top = jnp.max(jnp.where(live, gate, -jnp.inf), axis=0, keepdims=True)
        cand = jnp.where(live, jnp.where(gate == top, bidx, nb), nb)
        pick = jnp.where(bidx == jnp.min(cand, axis=0, keepdims=True), 1, 0)
        chosen = chosen + pick
        avail = avail - pick
    return chosen


N0, N1 = SCORE_SLOTS, SCORE_SLOTS + 1


def _attn_kernel(rb_ref, qt_ref, qnext_ref, k_ref, vt_ref, bown_ref, bprev_ref, o_ref,
                 kmean_ref, kpad_ref, qa_ref, s_ref, m_ref, acc_ref):
    i = pl.program_id(1)
    nb = vt_ref.shape[0]
    blk = MOBA_BLOCK
    cur = i & 1
    nxt = 1 - cur
    bidx = lax.broadcasted_iota(jnp.int32, (nb, blk), 0)

    def put_scores(h, j, slot, half):
        start = pl.multiple_of(j * blk, blk)
        s_ref[slot, h] = jnp.dot(kpad_ref[pl.ds(start, blk), h * K_PAD:(h + 1) * K_PAD],
                                 qa_ref[half, h], preferred_element_type=F32)

    def score_operands(q_ref, tile):
        gates = []
        for h in range(N_HEADS):
            q = q_ref[h * HEAD_DIM:(h + 1) * HEAD_DIM, :]
            parts = _split_bf16(kmean_ref[:, h * HEAD_DIM:(h + 1) * HEAD_DIM])
            g3 = jnp.dot(jnp.concatenate(parts, axis=0), q, preferred_element_type=F32)
            gate = g3[0:nb]
            for r in range(1, PEN_SPLIT):
                gate = gate + g3[r * nb:(r + 1) * nb]
            gates.append(gate)
        operands = []
        for h in range(N_HEADS):
            chosen = _select_blocks(gates[h], tile)
            far_bias = jnp.where(bidx < tile - 1, rb_ref[h, N_BUCKETS - 1] * LOG2E, 0.0)
            pen = jnp.where(bidx == tile, 0.0, jnp.where(chosen == 1, far_bias, NEG))
            operands.append(jnp.concatenate(
                [q_ref[h * HEAD_DIM:(h + 1) * HEAD_DIM, :]] + _split_bf16(pen)
                + [jnp.zeros((K_PAD - HEAD_DIM - PEN_SPLIT * nb, blk), BF16)], axis=0))
        return operands

    def store_operands(operands, half):
        for h in range(N_HEADS):
            qa_ref[half, h] = operands[h]

    @pl.when(i == 0)
    def _():
        pad_lane = lax.broadcasted_iota(jnp.int32, (blk, K_PAD - HEAD_DIM), 1)
        for j in range(nb):
            rows = slice(j * blk, (j + 1) * blk)
            kj = k_ref[rows, :]
            kmean_ref[j:j + 1, :] = jnp.sum(kj.astype(F32), axis=0, keepdims=True) * (1.0 / blk)
            hot = pad_lane == j
            for rep in range(1, PEN_SPLIT):
                hot = hot | (pad_lane == j + rep * nb)
            one_hot = jnp.where(hot, 1.0, 0.0).astype(BF16)
            for h in range(N_HEADS):
                kpad_ref[rows, h * K_PAD:(h + 1) * K_PAD] = jnp.concatenate(
                    [kj[:, h * HEAD_DIM:(h + 1) * HEAD_DIM], one_hot], axis=1)
        store_operands(score_operands(qt_ref, 0), 0)
        for h in range(N_HEADS):
            put_scores(h, 0, N0, 0)

    m_ref[...] = jnp.full(m_ref.shape, M_INIT, F32)
    acc_ref[...] = jnp.zeros(acc_ref.shape, F32)

    ones = jnp.ones((V_ONES, blk), BF16)

    def absorb(h, j, slot, kind):
        s = s_ref[slot, h]
        if kind == "own":
            s = s + bown_ref[h]
        elif kind == "prev":
            s = s + bprev_ref[h]
        m_old = m_ref[h]
        m_new = jnp.maximum(m_old, jnp.max(s, axis=0, keepdims=True))
        p = jnp.exp2(s - m_new).astype(BF16)
        va = jnp.concatenate([vt_ref[j, h * HEAD_DIM:(h + 1) * HEAD_DIM, :], ones], axis=0)
        acc_ref[h] = jnp.exp2(m_old - m_new) * acc_ref[h] + jnp.dot(va, p, preferred_element_type=F32)
        m_ref[h] = m_new

    def stage(j, slot, kind, fill=None):
        if fill is not None:
            for h in range(FILL_AHEAD):
                put_scores(h, *fill)
        for h in range(N_HEADS):
            if fill is not None and h + FILL_AHEAD < N_HEADS:
                put_scores(h + FILL_AHEAD, *fill)
            absorb(h, j, slot, kind)

    next0, next1 = (0, N0, nxt), (1, N1, nxt)

    @pl.when(i == 0)
    def _():
        ops_next = score_operands(qnext_ref, i + 1)
        stage(0, N0, "own")
        store_operands(ops_next, nxt)
        for h in range(N_HEADS):
            put_scores(h, *next0)
            put_scores(h, *next1)

    @pl.when(i == 1)
    def _():
        ops_next = score_operands(qnext_ref, i + 1)
        stage(0, N0, "prev")
        store_operands(ops_next, nxt)
        stage(1, N1, "own", fill=next0)
        for h in range(N_HEADS):
            put_scores(h, *next1)

    @pl.when(i == 2)
    def _():
        ops_next = score_operands(qnext_ref, i + 1)
        stage(0, N0, "far", fill=(2, 2, cur))
        store_operands(ops_next, nxt)
        stage(1, N1, "prev", fill=next0)
        stage(2, 2, "own", fill=next1)

    @pl.when(i >= 3)
    def _():
        ops_next = score_operands(qnext_ref, i + 1)
        stage(0, N0, "far", fill=(2, 2, cur))
        stage(1, N1, "far", fill=(3, 0, cur))
        store_operands(ops_next, nxt)

    def triple_body(a, carry):
        j0 = 2 + 3 * a
        stage(j0, 2, "far", fill=(j0 + 2, 1, cur))
        stage(j0 + 1, 0, "far", fill=(j0 + 3, 2, cur))
        stage(j0 + 2, 1, "far", fill=(j0 + 4, 0, cur))
        return carry

    n_triples = lax.div(jnp.maximum(i - 3, 0), 3)
    lax.fori_loop(0, n_triples, triple_body, 0)
    rest = jnp.where(i >= 3, i - 1 - 3 * n_triples, 0)

    @pl.when(rest == 2)
    def _():
        stage(i - 1, 2, "prev", fill=next0)
        stage(i, 0, "own", fill=next1)

    @pl.when(rest == 3)
    def _():
        stage(i - 2, 2, "far", fill=(i, 1, cur))
        stage(i - 1, 0, "prev", fill=next0)
        stage(i, 1, "own", fill=next1)

    @pl.when(rest == 4)
    def _():
        stage(i - 3, 2, "far", fill=(i - 1, 1, cur))
        stage(i - 2, 0, "far", fill=(i, 2, cur))
        stage(i - 1, 1, "prev", fill=next0)
        stage(i, 2, "own", fill=next1)

    for h in range(N_HEADS):
        acc = acc_ref[h]
        o_ref[h * HEAD_DIM:(h + 1) * HEAD_DIM, :] = (
            acc[0:HEAD_DIM] / acc[HEAD_DIM:HEAD_DIM + 1]).astype(o_ref.dtype)


def _moba_attention(rel_bias, qt, k, vt, bias_own, bias_prev):
    b, nb, _, blk = qt.shape
    s = nb * blk
    q_tile = pl.BlockSpec((None, None, ATTN_WIDTH, blk), lambda bi, i: (bi, i, 0, 0))
    return pl.pallas_call(
        _attn_kernel,
        grid=(b, nb),
        in_specs=[pl.BlockSpec(memory_space=pltpu.SMEM),
                  q_tile,
                  pl.BlockSpec((None, None, ATTN_WIDTH, blk),
                               lambda bi, i: (bi, jnp.minimum(i + 1, nb - 1), 0, 0)),
                  pl.BlockSpec((None, s, ATTN_WIDTH), lambda bi, i: (bi, 0, 0)),
                  pl.BlockSpec((None, nb, ATTN_WIDTH, blk), lambda bi, i: (bi, 0, 0, 0)),
                  _const_spec(bias_own.shape), _const_spec(bias_prev.shape)],
        out_specs=q_tile,
        out_shape=jax.ShapeDtypeStruct((b, nb, ATTN_WIDTH, blk), BF16),
        scratch_shapes=[pltpu.VMEM((nb, ATTN_WIDTH), F32),
                        pltpu.VMEM((s, N_HEADS * K_PAD), BF16),
                        pltpu.VMEM((2, N_HEADS, K_PAD, blk), BF16),
                        pltpu.VMEM((SCORE_SLOTS + 2, N_HEADS, blk, blk), F32),
                        pltpu.VMEM((N_HEADS, 1, blk), F32),
                        pltpu.VMEM((N_HEADS, HEAD_DIM + V_ONES, blk), F32)],
        compiler_params=_params(2),
        name="moba_attention",
    )(rel_bias, qt, qt, k, vt, bias_own, bias_prev)


def _layer_norm(z, g, b):
    mu = jnp.mean(z, axis=-1, keepdims=True)
    zc = z - mu
    var = jnp.mean(zc * zc, axis=-1, keepdims=True)
    return zc * lax.rsqrt(var + LN_EPS) * g + b


def _mixer_kernel(alpha, x_ref, at_ref, wp_ref, wg_ref, wpool_ref, pscale_ref, wba_ref, wbp_ref,
                  wout_ref, g_ref, b_ref, h_ref, pbuf_ref):
    t = pl.program_id(1)
    tile = x_ref.shape[0]
    d = x_ref.shape[1]

    @pl.when(t == 0)
    def _():
        pbuf_ref[0:POOL_HALO, :] = jnp.zeros((POOL_HALO, POOL_WIDTH), F32)

    x = x_ref[...]
    xb = x.astype(BF16)
    p = jnp.dot(xb, wp_ref[...], preferred_element_type=F32)
    gate_logits = jnp.dot(xb, wg_ref[...], preferred_element_type=F32)
    y_attn = jnp.concatenate(
        [lax.dot_general(at_ref[c], wba_ref[...], TN_DIMS, preferred_element_type=F32)
         for c in range(at_ref.shape[0])], axis=0)

    pbuf_ref[POOL_HALO:POOL_HALO + tile, :] = p
    pos = t * tile + lax.broadcasted_iota(jnp.int32, (tile, POOL_GROUP), 0)
    ys = []
    for gi, w in enumerate(POOL_WINDOWS):
        cols = slice(gi * POOL_GROUP, (gi + 1) * POOL_GROUP)
        pg = p[:, cols]
        win = pg
        for back in range(1, w):
            win = win + pbuf_ref[POOL_HALO - back:POOL_HALO - back + tile, cols]
        cnt = jnp.minimum(pos + 1, w).astype(F32)
        diff = win / cnt - pg
        ys.append(jnp.dot(diff.astype(BF16), wpool_ref[gi], preferred_element_type=F32))
    pbuf_ref[0:POOL_HALO, :] = pbuf_ref[tile:tile + POOL_HALO, :]
    y = jnp.concatenate(ys, axis=1) * pscale_ref[...]
    y_pool = jnp.dot(y.astype(BF16), wbp_ref[...], preferred_element_type=F32)

    gates = 1.0 / (1.0 + jnp.exp(-gate_logits))
    mixed = (gates[:, :d] * y_attn + gates[:, d:] * y_pool).astype(BF16)
    for rows in (slice(0, tile // 2), slice(tile // 2, tile)):
        z = alpha * x[rows] + jnp.dot(mixed[rows], wout_ref[...], preferred_element_type=F32)
        h_ref[rows, :] = _layer_norm(z, g_ref[...], b_ref[...])


def _mixer(alpha, x, attn_t, wp, wg, wpool, pscale, wba, wbp, wout, ln_g, ln_b):
    b, s, d = x.shape
    bpt = TOKEN_TILE // MOBA_BLOCK
    consts = (wp, wg, wpool, pscale, wba, wbp, wout, ln_g, ln_b)
    return pl.pallas_call(
        functools.partial(_mixer_kernel, alpha),
        grid=(b, s // TOKEN_TILE),
        in_specs=[pl.BlockSpec((None, TOKEN_TILE, d), lambda i, t: (i, t, 0)),
                  pl.BlockSpec((None, bpt, ATTN_WIDTH, MOBA_BLOCK), lambda i, t: (i, t, 0, 0))]
                 + [_const_spec(c.shape) for c in consts],
        out_specs=pl.BlockSpec((None, TOKEN_TILE, d), lambda i, t: (i, t, 0)),
        out_shape=jax.ShapeDtypeStruct((b, s, d), F32),
        scratch_shapes=[pltpu.VMEM((POOL_HALO + TOKEN_TILE, POOL_WIDTH), F32)],
        compiler_params=_params(2),
        name="mixer",
    )(x, attn_t, *consts)


def _ffn_kernel(alpha, tiles_per_seq, n_tiles, h_ref, win_ref, cw_ref, cb_ref, wo_ref, g_ref, b_ref,
                o_ref, carry_ref, z_ref, act_ref):
    g = pl.program_id(0)
    tile = h_ref.shape[0]
    d_ff = wo_ref.shape[0]
    n_chunks = d_ff // FF_CHUNK

    @pl.when(g == 0)
    def _():
        z_ref[...] = jnp.zeros(z_ref.shape, F32)

    @pl.when(g % tiles_per_seq == 0)
    def _():
        carry_ref[...] = jnp.zeros(carry_ref.shape, F32)

    @pl.when(g == n_tiles)
    def _():
        o_ref[...] = _layer_norm(z_ref[...], g_ref[...], b_ref[...])

    @pl.when(g < n_tiles)
    def _():
        o_ref[...] = _layer_norm(z_ref[...], g_ref[...], b_ref[...])
        _ffn_tile(alpha, tile, n_chunks, d_ff, h_ref, win_ref, cw_ref, cb_ref, wo_ref, carry_ref, z_ref, act_ref)


def _ffn_tile(alpha, tile, n_chunks, d_ff, h_ref, win_ref, cw_ref, cb_ref, wo_ref, carry_ref, z_ref, act_ref):
    h = h_ref[...]
    hb = h.astype(BF16)
    row = lax.broadcasted_iota(jnp.int32, (CONV_HALO, FF_CHUNK), 0)

    def cols(c, base=0):
        return slice(base + c * FF_CHUNK, base + (c + 1) * FF_CHUNK)

    def project(c):
        return (jnp.dot(hb, win_ref[:, cols(c)], preferred_element_type=F32),
                jnp.dot(hb, win_ref[:, cols(c, d_ff)], preferred_element_type=F32))

    def shifted(a, prev, back):
        rolled = pltpu.roll(a, back, 0)
        top = rolled[0:CONV_HALO]
        for r in range(back):
            top = jnp.where(row == r, prev[CONV_HALO - back + r:CONV_HALO - back + r + 1, :], top)
        return jnp.concatenate([top, rolled[CONV_HALO:]], axis=0)

    nxt = project(0)
    for c in range(n_chunks):
        a, u = nxt
        if c + 1 < n_chunks:
            nxt = project(c + 1)
        prev = carry_ref[c]
        carry_ref[c] = a[tile - CONV_HALO:tile, :]
        cw = cw_ref[:, cols(c)]
        y = cb_ref[:, cols(c)]
        for tap in range(CONV_WIDTH):
            back = CONV_WIDTH - 1 - tap
            src = a if back == 0 else shifted(a, prev, back)
            y = y + src * cw[tap:tap + 1, :]
        act = 0.5 * y * (1.0 + lax.erf(y * math.sqrt(0.5))) * u
        act_ref[:, cols(c)] = act.astype(BF16)
    split = (n_chunks - 1) * FF_CHUNK
    f = (jnp.dot(act_ref[:, :split], wo_ref[:split, :], preferred_element_type=F32)
         + jnp.dot(act_ref[:, split:], wo_ref[split:, :], preferred_element_type=F32))
    z_ref[...] = alpha * h + f


def _conv_ffn(alpha, h, win, cw, cb, wo, ln_g, ln_b):
    b, s, d = h.shape
    d_ff = wo.shape[0]
    tiles_per_seq = s // FFN_TILE
    n_tiles = b * tiles_per_seq
    consts = (win, cw, cb, wo, ln_g, ln_b)

    def tile_index(g):
        return (g // tiles_per_seq, g % tiles_per_seq, 0)

    return pl.pallas_call(
        functools.partial(_ffn_kernel, alpha, tiles_per_seq, n_tiles),
        grid=(n_tiles + 1,),
        in_specs=[pl.BlockSpec((None, FFN_TILE, d), lambda g: tile_index(jnp.minimum(g, n_tiles - 1)))]
                 + [_const_spec(c.shape) for c in consts],
        out_specs=pl.BlockSpec((None, FFN_TILE, d), lambda g: tile_index(jnp.maximum(g - 1, 0))),
        out_shape=jax.ShapeDtypeStruct((b, s, d), F32),
        scratch_shapes=[pltpu.VMEM((d_ff // FF_CHUNK, CONV_HALO, FF_CHUNK), F32),
                        pltpu.VMEM((FFN_TILE, d), F32),
                        pltpu.VMEM((FFN_TILE, d_ff), BF16)],
        compiler_params=_params(1),
        name="conv_ffn",
    )(h, *consts)


def kernel(x, w_in, rel_bias, w_pool_group, pool_scale, w_branch_attn, w_branch_pool, w_out, ln1_g,
           ln1_b, w_ffn_in, conv_w, conv_b, w_ffn_out, ln2_g, ln2_b):
    b, s, d = x.shape
    depth = w_in.shape[0]
    d_ff = w_ffn_out.shape[1]
    assert s % TOKEN_TILE == 0 and TOKEN_TILE % MOBA_BLOCK == 0 and d_ff % FF_CHUNK == 0
    assert s % FFN_TILE == 0 and s % QKV_TILE == 0 and QKV_TILE % MOBA_BLOCK == 0
    assert w_in.shape[2] == 3 * ATTN_WIDTH + POOL_WIDTH + 2 * d
    alpha = (2.0 * depth) ** 0.25
    assert (K_PAD - HEAD_DIM - PEN_SPLIT * (s // MOBA_BLOCK)) % BF16_SUBLANES == 0
    assert K_PAD - HEAD_DIM - PEN_SPLIT * (s // MOBA_BLOCK) > 0
    scale = LOG2E / math.sqrt(HEAD_DIM)

    bias_own, bias_prev = _rel_bias_tables(rel_bias)
    h = x
    for l in range(depth):
        w = w_in[l]
        wq_t = (w[:, :ATTN_WIDTH] * scale).T.astype(BF16)
        wk = w[:, ATTN_WIDTH:2 * ATTN_WIDTH].astype(BF16)
        wv_t = w[:, 2 * ATTN_WIDTH:3 * ATTN_WIDTH].T.astype(BF16)
        qt, k, vt = _qkv_proj(h, wq_t, wk, wv_t)
        attn_t = _moba_attention(rel_bias, qt, k, vt, bias_own, bias_prev)
        h1 = _mixer(alpha, h, attn_t,
                    w[:, 3 * ATTN_WIDTH:3 * ATTN_WIDTH + POOL_WIDTH].astype(BF16),
                    w[:, 3 * ATTN_WIDTH + POOL_WIDTH:].astype(BF16),
                    w_pool_group[l].astype(BF16),
                    pool_scale[l].reshape(1, POOL_WIDTH),
                    w_branch_attn[l].astype(BF16),
                    w_branch_pool[l].astype(BF16),
                    w_out[l].astype(BF16),
                    ln1_g[l].reshape(1, d), ln1_b[l].reshape(1, d))
        h = _conv_ffn(alpha, h1, w_ffn_in[l].astype(BF16), conv_w[l], conv_b[l].reshape(1, d_ff),
                      w_ffn_out[l].astype(BF16), ln2_g[l].reshape(1, d), ln2_b[l].reshape(1, d))
    return h
```

```python
import functools
import math

import jax
import jax.numpy as jnp
from jax import lax
from jax.experimental import pallas as pl
from jax.experimental.pallas import tpu as pltpu

N_HEADS = 8
HEAD_DIM = 64
ATTN_WIDTH = N_HEADS * HEAD_DIM
MOBA_BLOCK = 256
MOBA_TOPK = 3
POOL_WINDOWS = (2, 4, 8, 16)
POOL_GROUP = 128
POOL_WIDTH = POOL_GROUP * len(POOL_WINDOWS)
N_BUCKETS = 32
MAX_DISTANCE = 128
CONV_WIDTH = 3
LN_EPS = 1e-5
NEG = -1e30
M_INIT = 0.5 * NEG
LOG2E = math.log2(math.e)
FILL_AHEAD = 2
PEN_SPLIT = 3

LANES = 128
K_PAD = LANES
BF16_SUBLANES = 16
MXU_TILE = 256
V_ONES = BF16_SUBLANES
SCORE_SLOTS = 3
POOL_HALO = max(POOL_WINDOWS)
CONV_HALO = 8
FF_CHUNK = MXU_TILE
TOKEN_TILE = 1024
QKV_TILE = 1024
FFN_TILE = 1024
VMEM_LIMIT = 56 * 1024 * 1024

F32 = jnp.float32
BF16 = jnp.bfloat16
NT_DIMS = (((1,), (1,)), ((), ()))
TN_DIMS = (((0,), (0,)), ((), ()))


def _params(n_grid):
    return pltpu.CompilerParams(dimension_semantics=("arbitrary",) * n_grid,
                                vmem_limit_bytes=VMEM_LIMIT)


def _const_spec(shape):
    nd = len(shape)
    return pl.BlockSpec(shape, lambda *_: (0,) * nd)


def _rel_bias_kernel(rb_ref, own_ref, prev_ref):
    h = pl.program_id(0)
    kk = lax.broadcasted_iota(jnp.int32, (MOBA_BLOCK, MOBA_BLOCK), 0)
    qq = lax.broadcasted_iota(jnp.int32, (MOBA_BLOCK, MOBA_BLOCK), 1)
    max_exact = N_BUCKETS // 2
    n_log = N_BUCKETS - max_exact
    first_n = [math.ceil(max_exact * (MAX_DISTANCE / max_exact) ** (k / n_log)) for k in range(n_log)]

    def table(dist):
        n = jnp.maximum(dist, 0)
        out = jnp.zeros(dist.shape, F32)
        for b in range(max_exact):
            out = jnp.where(n == b, rb_ref[h, b], out)
        for k in range(n_log):
            out = jnp.where(n >= first_n[k], rb_ref[h, max_exact + k], out)
        return out

    own_ref[...] = jnp.where(qq >= kk, table(qq - kk) * LOG2E, NEG)
    prev_ref[...] = table(qq - kk + MOBA_BLOCK) * LOG2E


def _rel_bias_tables(rel_bias):
    shape = jax.ShapeDtypeStruct((N_HEADS, MOBA_BLOCK, MOBA_BLOCK), F32)
    spec = pl.BlockSpec((None, MOBA_BLOCK, MOBA_BLOCK), lambda h: (h, 0, 0))
    return pl.pallas_call(
        _rel_bias_kernel,
        grid=(N_HEADS,),
        in_specs=[pl.BlockSpec(memory_space=pltpu.SMEM)],
        out_specs=[spec, spec],
        out_shape=[shape, shape],
        compiler_params=_params(1),
        name="rel_bias_tables",
    )(rel_bias)


def _qkv_kernel(x_ref, wq_ref, wk_ref, wv_ref, qt_ref, k_ref, vt_ref):
    xb = x_ref[...].astype(BF16)
    qt = lax.dot_general(wq_ref[...], xb, NT_DIMS, preferred_element_type=F32)
    vt = lax.dot_general(wv_ref[...], xb, NT_DIMS, preferred_element_type=F32)
    k_ref[...] = jnp.dot(xb, wk_ref[...], preferred_element_type=F32).astype(BF16)
    for c in range(qt_ref.shape[0]):
        cols = slice(c * MOBA_BLOCK, (c + 1) * MOBA_BLOCK)
        qt_ref[c] = qt[:, cols].astype(BF16)
        vt_ref[c] = vt[:, cols].astype(BF16)


def _qkv_proj(x, wq_t, wk, wv_t):
    b, s, d = x.shape
    nb = s // MOBA_BLOCK
    bpt = QKV_TILE // MOBA_BLOCK
    t_shape = jax.ShapeDtypeStruct((b, nb, ATTN_WIDTH, MOBA_BLOCK), BF16)
    t_spec = pl.BlockSpec((None, bpt, ATTN_WIDTH, MOBA_BLOCK), lambda i, t: (i, t, 0, 0))
    return pl.pallas_call(
        _qkv_kernel,
        grid=(b, s // QKV_TILE),
        in_specs=[pl.BlockSpec((None, QKV_TILE, d), lambda i, t: (i, t, 0)),
                  _const_spec(wq_t.shape), _const_spec(wk.shape), _const_spec(wv_t.shape)],
        out_specs=[t_spec,
                   pl.BlockSpec((None, QKV_TILE, ATTN_WIDTH), lambda i, t: (i, t, 0)),
                   t_spec],
        out_shape=[t_shape, jax.ShapeDtypeStruct((b, s, ATTN_WIDTH), BF16), t_shape],
        compiler_params=_params(2),
        name="qkv_proj",
    )(x, wq_t, wk, wv_t)


def _split_bf16(x):
    parts = []
    rem = x
    for _ in range(PEN_SPLIT):
        part = rem.astype(BF16)
        parts.append(part)
        rem = rem - part.astype(F32)
    return parts


def _select_blocks(gate, i):
    nb = gate.shape[0]
    bidx = lax.broadcasted_iota(jnp.int32, gate.shape, 0)
    avail = jnp.where(bidx < i, 1, 0)
    chosen = jnp.zeros(gate.shape, jnp.int32)
    for _ in range(MOBA_TOPK):
        live = avail == 1
        top = jnp.max(jnp.where(live, gate, -jnp.inf), axis=0, keepdims=True)
        cand = jnp.where(live, jnp.where(gate == top, bidx, nb), nb)
        pick = jnp.where(bidx == jnp.min(cand, axis=0, keepdims=True), 1, 0)
        chosen = chosen + pick
        avail = avail - pick
    return chosen


N0, N1 = SCORE_SLOTS, SCORE_SLOTS + 1


def _attn_kernel(rb_ref, qt_ref, qnext_ref, k_ref, vt_ref, bown_ref, bprev_ref, o_ref,
                 kmean_ref, kpad_ref, qa_ref, s_ref, m_ref, acc_ref):
    i = pl.program_id(1)
    nb = vt_ref.shape[0]
    blk = MOBA_BLOCK
    cur = i & 1
    nxt = 1 - cur
    bidx = lax.broadcasted_iota(jnp.int32, (nb, blk), 0)

    def put_scores(h, j, slot, half):
        start = pl.multiple_of(j * blk, blk)
        s_ref[slot, h] = jnp.dot(kpad_ref[pl.ds(start, blk), h * K_PAD:(h + 1) * K_PAD],
                                 qa_ref[half, h], preferred_element_type=F32)

    def score_operands(q_ref, tile):
        gates = []
        for h in range(N_HEADS):
            q = q_ref[h * HEAD_DIM:(h + 1) * HEAD_DIM, :]
            parts = _split_bf16(kmean_ref[:, h * HEAD_DIM:(h + 1) * HEAD_DIM])
            g3 = jnp.dot(jnp.concatenate(parts, axis=0), q, preferred_element_type=F32)
            gate = g3[0:nb]
            for r in range(1, PEN_SPLIT):
                gate = gate + g3[r * nb:(r + 1) * nb]
            gates.append(gate)
        operands = []
        for h in range(N_HEADS):
            chosen = _select_blocks(gates[h], tile)
            far_bias = jnp.where(bidx < tile - 1, rb_ref[h, N_BUCKETS - 1] * LOG2E, 0.0)
            pen = jnp.where(bidx == tile, 0.0, jnp.where(chosen == 1, far_bias, NEG))
            operands.append(jnp.concatenate(
                [q_ref[h * HEAD_DIM:(h + 1) * HEAD_DIM, :]] + _split_bf16(pen)
                + [jnp.zeros((K_PAD - HEAD_DIM - PEN_SPLIT * nb, blk), BF16)], axis=0))
        return operands

    def store_operands(operands, half):
        for h in range(N_HEADS):
            qa_ref[half, h] = operands[h]

    @pl.when(i == 0)
    def _():
        pad_lane = lax.broadcasted_iota(jnp.int32, (blk, K_PAD - HEAD_DIM), 1)
        for j in range(nb):
            rows = slice(j * blk, (j + 1) * blk)
            kj = k_ref[rows, :]
            kmean_ref[j:j + 1, :] = jnp.sum(kj.astype(F32), axis=0, keepdims=True) * (1.0 / blk)
            hot = pad_lane == j
            for rep in range(1, PEN_SPLIT):
                hot = hot | (pad_lane == j + rep * nb)
            one_hot = jnp.where(hot, 1.0, 0.0).astype(BF16)
            for h in range(N_HEADS):
                kpad_ref[rows, h * K_PAD:(h + 1) * K_PAD] = jnp.concatenate(
                    [kj[:, h * HEAD_DIM:(h + 1) * HEAD_DIM], one_hot], axis=1)
        store_operands(score_operands(qt_ref, 0), 0)
        for h in range(N_HEADS):
            put_scores(h, 0, N0, 0)

    m_ref[...] = jnp.full(m_ref.shape, M_INIT, F32)
    acc_ref[...] = jnp.zeros(acc_ref.shape, F32)

    ones = jnp.ones((V_ONES, blk), BF16)

    half = blk // 2

    def absorb(h, j, slot, kind):
        m_old = m_ref[h]
        if kind == "own":
            top = s_ref[slot, h, 0:half, :] + bown_ref[h, 0:half, :]
            bot = s_ref[slot, h, half:blk, half:blk] + bown_ref[h, half:blk, half:blk]
            m_top = jnp.max(top, axis=0, keepdims=True)
            m_blk = jnp.concatenate(
                [m_top[:, 0:half],
                 jnp.maximum(m_top[:, half:blk], jnp.max(bot, axis=0, keepdims=True))], axis=1)
            m_new = jnp.maximum(m_old, m_blk)
            p_bot = jnp.exp2(bot - m_new[:, half:blk]).astype(BF16)
            p = jnp.concatenate(
                [jnp.exp2(top - m_new).astype(BF16),
                 jnp.concatenate([jnp.zeros((half, half), BF16), p_bot], axis=1)], axis=0)
        else:
            s = s_ref[slot, h]
            if kind == "prev":
                s = s + bprev_ref[h]
            m_new = jnp.maximum(m_old, jnp.max(s, axis=0, keepdims=True))
            p = jnp.exp2(s - m_new).astype(BF16)
        va = jnp.concatenate([vt_ref[j, h * HEAD_DIM:(h + 1) * HEAD_DIM, :], ones], axis=0)
        acc_ref[h] = jnp.exp2(m_old - m_new) * acc_ref[h] + jnp.dot(va, p, preferred_element_type=F32)
        m_ref[h] = m_new

    def stage(j, slot, kind, fill=None):
        if fill is not None:
            for h in range(FILL_AHEAD):
                put_scores(h, *fill)
        for h in range(N_HEADS):
            if fill is not None and h + FILL_AHEAD < N_HEADS:
                put_scores(h + FILL_AHEAD, *fill)
            absorb(h, j, slot, kind)

    next0, next1 = (0, N0, nxt), (1, N1, nxt)

    @pl.when(i == 0)
    def _():
        ops_next = score_operands(qnext_ref, i + 1)
        stage(0, N0, "own")
        store_operands(ops_next, nxt)
        for h in range(N_HEADS):
            put_scores(h, *next0)
            put_scores(h, *next1)

    @pl.when(i == 1)
    def _():
        ops_next = score_operands(qnext_ref, i + 1)
        stage(0, N0, "prev")
        store_operands(ops_next, nxt)
        stage(1, N1, "own", fill=next0)
        for h in range(N_HEADS):
            put_scores(h, *next1)

    @pl.when(i == 2)
    def _():
        ops_next = score_operands(qnext_ref, i + 1)
        stage(0, N0, "far", fill=(2, 2, cur))
        store_operands(ops_next, nxt)
        stage(1, N1, "prev", fill=next0)
        stage(2, 2, "own", fill=next1)

    @pl.when(i >= 3)
    def _():
        ops_next = score_operands(qnext_ref, i + 1)
        stage(0, N0, "far", fill=(2, 2, cur))
        stage(1, N1, "far", fill=(3, 0, cur))
        store_operands(ops_next, nxt)

    def triple_body(a, carry):
        j0 = 2 + 3 * a
        stage(j0, 2, "far", fill=(j0 + 2, 1, cur))
        stage(j0 + 1, 0, "far", fill=(j0 + 3, 2, cur))
        stage(j0 + 2, 1, "far", fill=(j0 + 4, 0, cur))
        return carry

    n_triples = lax.div(jnp.maximum(i - 3, 0), 3)
    lax.fori_loop(0, n_triples, triple_body, 0)
    rest = jnp.where(i >= 3, i - 1 - 3 * n_triples, 0)

    @pl.when(rest == 2)
    def _():
        stage(i - 1, 2, "prev", fill=next0)
        stage(i, 0, "own", fill=next1)

    @pl.when(rest == 3)
    def _():
        stage(i - 2, 2, "far", fill=(i, 1, cur))
        stage(i - 1, 0, "prev", fill=next0)
        stage(i, 1, "own", fill=next1)

    @pl.when(rest == 4)
    def _():
        stage(i - 3, 2, "far", fill=(i - 1, 1, cur))
        stage(i - 2, 0, "far", fill=(i, 2, cur))
        stage(i - 1, 1, "prev", fill=next0)
        stage(i, 2, "own", fill=next1)

    for h in range(N_HEADS):
        acc = acc_ref[h]
        o_ref[h * HEAD_DIM:(h + 1) * HEAD_DIM, :] = (
            acc[0:HEAD_DIM] / acc[HEAD_DIM:HEAD_DIM + 1]).astype(o_ref.dtype)


def _moba_attention(rel_bias, qt, k, vt, bias_own, bias_prev):
    b, nb, _, blk = qt.shape
    s = nb * blk
    q_tile = pl.BlockSpec((None, None, ATTN_WIDTH, blk), lambda bi, i: (bi, i, 0, 0))
    return pl.pallas_call(
        _attn_kernel,
        grid=(b, nb),
        in_specs=[pl.BlockSpec(memory_space=pltpu.SMEM),
                  q_tile,
                  pl.BlockSpec((None, None, ATTN_WIDTH, blk),
                               lambda bi, i: (bi, jnp.minimum(i + 1, nb - 1), 0, 0)),
                  pl.BlockSpec((None, s, ATTN_WIDTH), lambda bi, i: (bi, 0, 0)),
                  pl.BlockSpec((None, nb, ATTN_WIDTH, blk), lambda bi, i: (bi, 0, 0, 0)),
                  _const_spec(bias_own.shape), _const_spec(bias_prev.shape)],
        out_specs=q_tile,
        out_shape=jax.ShapeDtypeStruct((b, nb, ATTN_WIDTH, blk), BF16),
        scratch_shapes=[pltpu.VMEM((nb, ATTN_WIDTH), F32),
                        pltpu.VMEM((s, N_HEADS * K_PAD), BF16),
                        pltpu.VMEM((2, N_HEADS, K_PAD, blk), BF16),
                        pltpu.VMEM((SCORE_SLOTS + 2, N_HEADS, blk, blk), F32),
                        pltpu.VMEM((N_HEADS, 1, blk), F32),
                        pltpu.VMEM((N_HEADS, HEAD_DIM + V_ONES, blk), F32)],
        compiler_params=_params(2),
        name="moba_attention",
    )(rel_bias, qt, qt, k, vt, bias_own, bias_prev)


def _layer_norm(z, g, b):
    mu = jnp.mean(z, axis=-1, keepdims=True)
    zc = z - mu
    var = jnp.mean(zc * zc, axis=-1, keepdims=True)
    return zc * lax.rsqrt(var + LN_EPS) * g + b


def _mixer_kernel(alpha, x_ref, at_ref, wp_ref, wg_ref, wpool_ref, pscale_ref, wba_ref, wbp_ref,
                  wout_ref, g_ref, b_ref, h_ref, pbuf_ref):
    t = pl.program_id(1)
    tile = x_ref.shape[0]
    d = x_ref.shape[1]

    @pl.when(t == 0)
    def _():
        pbuf_ref[0:POOL_HALO, :] = jnp.zeros((POOL_HALO, POOL_WIDTH), F32)

    x = x_ref[...]
    xb = x.astype(BF16)
    p = jnp.dot(xb, wp_ref[...], preferred_element_type=F32)
    gate_logits = jnp.dot(xb, wg_ref[...], preferred_element_type=F32)
    y_attn = jnp.concatenate(
        [lax.dot_general(at_ref[c], wba_ref[...], TN_DIMS, preferred_element_type=F32)
         for c in range(at_ref.shape[0])], axis=0)

    pbuf_ref[POOL_HALO:POOL_HALO + tile, :] = p
    pos = t * tile + lax.broadcasted_iota(jnp.int32, (tile, POOL_GROUP), 0)
    ys = []
    for gi, w in enumerate(POOL_WINDOWS):
        cols = slice(gi * POOL_GROUP, (gi + 1) * POOL_GROUP)
        pg = p[:, cols]
        win = pg
        for back in range(1, w):
            win = win + pbuf_ref[POOL_HALO - back:POOL_HALO - back + tile, cols]
        cnt = jnp.minimum(pos + 1, w).astype(F32)
        diff = win / cnt - pg
        ys.append(jnp.dot(diff.astype(BF16), wpool_ref[gi], preferred_element_type=F32))
    pbuf_ref[0:POOL_HALO, :] = pbuf_ref[tile:tile + POOL_HALO, :]
    y = jnp.concatenate(ys, axis=1) * pscale_ref[...]
    y_pool = jnp.dot(y.astype(BF16), wbp_ref[...], preferred_element_type=F32)

    gates = 1.0 / (1.0 + jnp.exp(-gate_logits))
    mixed = (gates[:, :d] * y_attn + gates[:, d:] * y_pool).astype(BF16)
    for rows in (slice(0, tile // 2), slice(tile // 2, tile)):
        z = alpha * x[rows] + jnp.dot(mixed[rows], wout_ref[...], preferred_element_type=F32)
        h_ref[rows, :] = _layer_norm(z, g_ref[...], b_ref[...])


def _mixer(alpha, x, attn_t, wp, wg, wpool, pscale, wba, wbp, wout, ln_g, ln_b):
    b, s, d = x.shape
    bpt = TOKEN_TILE // MOBA_BLOCK
    consts = (wp, wg, wpool, pscale, wba, wbp, wout, ln_g, ln_b)
    return pl.pallas_call(
        functools.partial(_mixer_kernel, alpha),
        grid=(b, s // TOKEN_TILE),
        in_specs=[pl.BlockSpec((None, TOKEN_TILE, d), lambda i, t: (i, t, 0)),
                  pl.BlockSpec((None, bpt, ATTN_WIDTH, MOBA_BLOCK), lambda i, t: (i, t, 0, 0))]
                 + [_const_spec(c.shape) for c in consts],
        out_specs=pl.BlockSpec((None, TOKEN_TILE, d), lambda i, t: (i, t, 0)),
        out_shape=jax.ShapeDtypeStruct((b, s, d), F32),
        scratch_shapes=[pltpu.VMEM((POOL_HALO + TOKEN_TILE, POOL_WIDTH), F32)],
        compiler_params=_params(2),
        name="mixer",
    )(x, attn_t, *consts)


def _ffn_kernel(alpha, tiles_per_seq, n_tiles, h_ref, win_ref, cw_ref, cb_ref, wo_ref, g_ref, b_ref,
                o_ref, carry_ref, z_ref, act_ref):
    g = pl.program_id(0)
    tile = h_ref.shape[0]
    d_ff = wo_ref.shape[0]
    n_chunks = d_ff // FF_CHUNK

    @pl.when(g == 0)
    def _():
        z_ref[...] = jnp.zeros(z_ref.shape, F32)

    @pl.when(g % tiles_per_seq == 0)
    def _():
        carry_ref[...] = jnp.zeros(carry_ref.shape, F32)

    @pl.when(g == n_tiles)
    def _():
        o_ref[...] = _layer_norm(z_ref[...], g_ref[...], b_ref[...])

    @pl.when(g < n_tiles)
    def _():
        o_ref[...] = _layer_norm(z_ref[...], g_ref[...], b_ref[...])
        _ffn_tile(alpha, tile, n_chunks, d_ff, h_ref, win_ref, cw_ref, cb_ref, wo_ref, carry_ref, z_ref, act_ref)


def _ffn_tile(alpha, tile, n_chunks, d_ff, h_ref, win_ref, cw_ref, cb_ref, wo_ref, carry_ref, z_ref, act_ref):
    h = h_ref[...]
    hb = h.astype(BF16)
    row = lax.broadcasted_iota(jnp.int32, (CONV_HALO, FF_CHUNK), 0)

    def cols(c, base=0):
        return slice(base + c * FF_CHUNK, base + (c + 1) * FF_CHUNK)

    def project(c):
        return (jnp.dot(hb, win_ref[:, cols(c)], preferred_element_type=F32),
                jnp.dot(hb, win_ref[:, cols(c, d_ff)], preferred_element_type=F32))

    def shifted(a, prev, back):
        rolled = pltpu.roll(a, back, 0)
        top = rolled[0:CONV_HALO]
        for r in range(back):
            top = jnp.where(row == r, prev[CONV_HALO - back + r:CONV_HALO - back + r + 1, :], top)
        return jnp.concatenate([top, rolled[CONV_HALO:]], axis=0)

    nxt = project(0)
    for c in range(n_chunks):
        a, u = nxt
        if c + 1 < n_chunks:
            nxt = project(c + 1)
        prev = carry_ref[c]
        carry_ref[c] = a[tile - CONV_HALO:tile, :]
        cw = cw_ref[:, cols(c)]
        y = cb_ref[:, cols(c)]
        for tap in range(CONV_WIDTH):
            back = CONV_WIDTH - 1 - tap
            src = a if back == 0 else shifted(a, prev, back)
            y = y + src * cw[tap:tap + 1, :]
        act = 0.5 * y * (1.0 + lax.erf(y * math.sqrt(0.5))) * u
        act_ref[:, cols(c)] = act.astype(BF16)
    split = (n_chunks - 1) * FF_CHUNK
    f = (jnp.dot(act_ref[:, :split], wo_ref[:split, :], preferred_element_type=F32)
         + jnp.dot(act_ref[:, split:], wo_ref[split:, :], preferred_element_type=F32))
    z_ref[...] = alpha * h + f


def _conv_ffn(alpha, h, win, cw, cb, wo, ln_g, ln_b):
    b, s, d = h.shape
    d_ff = wo.shape[0]
    tiles_per_seq = s // FFN_TILE
    n_tiles = b * tiles_per_seq
    consts = (win, cw, cb, wo, ln_g, ln_b)

    def tile_index(g):
        return (g // tiles_per_seq, g % tiles_per_seq, 0)

    return pl.pallas_call(
        functools.partial(_ffn_kernel, alpha, tiles_per_seq, n_tiles),
        grid=(n_tiles + 1,),
        in_specs=[pl.BlockSpec((None, FFN_TILE, d), lambda g: tile_index(jnp.minimum(g, n_tiles - 1)))]
                 + [_const_spec(c.shape) for c in consts],
        out_specs=pl.BlockSpec((None, FFN_TILE, d), lambda g: tile_index(jnp.maximum(g - 1, 0))),
        out_shape=jax.ShapeDtypeStruct((b, s, d), F32),
        scratch_shapes=[pltpu.VMEM((d_ff // FF_CHUNK, CONV_HALO, FF_CHUNK), F32),
                        pltpu.VMEM((FFN_TILE, d), F32),
                        pltpu.VMEM((FFN_TILE, d_ff), BF16)],
        compiler_params=_params(1),
        name="conv_ffn",
    )(h, *consts)


def kernel(x, w_in, rel_bias, w_pool_group, pool_scale, w_branch_attn, w_branch_pool, w_out, ln1_g,
           ln1_b, w_ffn_in, conv_w, conv_b, w_ffn_out, ln2_g, ln2_b):
    b, s, d = x.shape
    depth = w_in.shape[0]
    d_ff = w_ffn_out.shape[1]
    assert s % TOKEN_TILE == 0 and TOKEN_TILE % MOBA_BLOCK == 0 and d_ff % FF_CHUNK == 0
    assert s % FFN_TILE == 0 and s % QKV_TILE == 0 and QKV_TILE % MOBA_BLOCK == 0
    assert w_in.shape[2] == 3 * ATTN_WIDTH + POOL_WIDTH + 2 * d
    alpha = (2.0 * depth) ** 0.25
    assert (K_PAD - HEAD_DIM - PEN_SPLIT * (s // MOBA_BLOCK)) % BF16_SUBLANES == 0
    assert K_PAD - HEAD_DIM - PEN_SPLIT * (s // MOBA_BLOCK) > 0
    scale = LOG2E / math.sqrt(HEAD_DIM)

    bias_own, bias_prev = _rel_bias_tables(rel_bias)
    h = x
    for l in range(depth):
        w = w_in[l]
        wq_t = (w[:, :ATTN_WIDTH] * scale).T.astype(BF16)
        wk = w[:, ATTN_WIDTH:2 * ATTN_WIDTH].astype(BF16)
        wv_t = w[:, 2 * ATTN_WIDTH:3 * ATTN_WIDTH].T.astype(BF16)
        qt, k, vt = _qkv_proj(h, wq_t, wk, wv_t)
        attn_t = _moba_attention(rel_bias, qt, k, vt, bias_own, bias_prev)
        h1 = _mixer(alpha, h, attn_t,
                    w[:, 3 * ATTN_WIDTH:3 * ATTN_WIDTH + POOL_WIDTH].astype(BF16),
                    w[:, 3 * ATTN_WIDTH + POOL_WIDTH:].astype(BF16),
                    w_pool_group[l].astype(BF16),
                    pool_scale[l].reshape(1, POOL_WIDTH),
                    w_branch_attn[l].astype(BF16),
                    w_branch_pool[l].astype(BF16),
                    w_out[l].astype(BF16),
                    ln1_g[l].reshape(1, d), ln1_b[l].reshape(1, d))
        h = _conv_ffn(alpha, h1, w_ffn_in[l].astype(BF16), conv_w[l], conv_b[l].reshape(1, d_ff),
                      w_ffn_out[l].astype(BF16), ln2_g[l].reshape(1, d), ln2_b[l].reshape(1, d))
    return h
```

```python
import functools
import math

import jax
import jax.numpy as jnp
from jax import lax
from jax.experimental import pallas as pl
from jax.experimental.pallas import tpu as pltpu

N_HEADS = 8
HEAD_DIM = 64
ATTN_WIDTH = N_HEADS * HEAD_DIM
MOBA_BLOCK = 256
MOBA_TOPK = 3
POOL_WINDOWS = (2, 4, 8, 16)
POOL_GROUP = 128
POOL_WIDTH = POOL_GROUP * len(POOL_WINDOWS)
N_BUCKETS = 32
MAX_DISTANCE = 128
CONV_WIDTH = 3
LN_EPS = 1e-5
NEG = -1e30
M_INIT = 0.5 * NEG
LOG2E = math.log2(math.e)
FILL_AHEAD = 2
PEN_SPLIT = 3

LANES = 128
K_PAD = LANES
BF16_SUBLANES = 16
MXU_TILE = 256
V_ONES = BF16_SUBLANES
SCORE_SLOTS = 3
POOL_HALO = max(POOL_WINDOWS)
CONV_HALO = 8
FF_CHUNK = MXU_TILE
TOKEN_TILE = 1024
QKV_TILE = 1024
FFN_TILE = 1024
VMEM_LIMIT = 56 * 1024 * 1024

F32 = jnp.float32
BF16 = jnp.bfloat16
NT_DIMS = (((1,), (1,)), ((), ()))
TN_DIMS = (((0,), (0,)), ((), ()))


def _params(n_grid):
    return pltpu.CompilerParams(dimension_semantics=("arbitrary",) * n_grid,
                                vmem_limit_bytes=VMEM_LIMIT)


def _const_spec(shape):
    nd = len(shape)
    return pl.BlockSpec(shape, lambda *_: (0,) * nd)


def _rel_bias_kernel(rb_ref, own_ref, prev_ref):
    h = pl.program_id(0)
    kk = lax.broadcasted_iota(jnp.int32, (MOBA_BLOCK, MOBA_BLOCK), 0)
    qq = lax.broadcasted_iota(jnp.int32, (MOBA_BLOCK, MOBA_BLOCK), 1)
    max_exact = N_BUCKETS // 2
    n_log = N_BUCKETS - max_exact
    first_n = [math.ceil(max_exact * (MAX_DISTANCE / max_exact) ** (k / n_log)) for k in range(n_log)]

    def table(dist):
        n = jnp.maximum(dist, 0)
        out = jnp.zeros(dist.shape, F32)
        for b in range(max_exact):
            out = jnp.where(n == b, rb_ref[h, b], out)
        for k in range(n_log):
            out = jnp.where(n >= first_n[k], rb_ref[h, max_exact + k], out)
        return out

    own_ref[...] = jnp.where(qq >= kk, table(qq - kk) * LOG2E, NEG)
    prev_ref[...] = table(qq - kk + MOBA_BLOCK) * LOG2E


def _rel_bias_tables(rel_bias):
    shape = jax.ShapeDtypeStruct((N_HEADS, MOBA_BLOCK, MOBA_BLOCK), F32)
    spec = pl.BlockSpec((None, MOBA_BLOCK, MOBA_BLOCK), lambda h: (h, 0, 0))
    return pl.pallas_call(
        _rel_bias_kernel,
        grid=(N_HEADS,),
        in_specs=[pl.BlockSpec(memory_space=pltpu.SMEM)],
        out_specs=[spec, spec],
        out_shape=[shape, shape],
        compiler_params=_params(1),
        name="rel_bias_tables",
    )(rel_bias)


def _qkv_kernel(x_ref, wq_ref, wk_ref, wv_ref, qt_ref, k_ref, vt_ref):
    xb = x_ref[...].astype(BF16)
    qt = lax.dot_general(wq_ref[...], xb, NT_DIMS, preferred_element_type=F32)
    vt = lax.dot_general(wv_ref[...], xb, NT_DIMS, preferred_element_type=F32)
    k_ref[...] = jnp.dot(xb, wk_ref[...], preferred_element_type=F32).astype(BF16)
    for c in range(qt_ref.shape[0]):
        cols = slice(c * MOBA_BLOCK, (c + 1) * MOBA_BLOCK)
        qt_ref[c] = qt[:, cols].astype(BF16)
        vt_ref[c] = vt[:, cols].astype(BF16)


def _qkv_proj(x, wq_t, wk, wv_t):
    b, s, d = x.shape
    nb = s // MOBA_BLOCK
    bpt = QKV_TILE // MOBA_BLOCK
    t_shape = jax.ShapeDtypeStruct((b, nb, ATTN_WIDTH, MOBA_BLOCK), BF16)
    t_spec = pl.BlockSpec((None, bpt, ATTN_WIDTH, MOBA_BLOCK), lambda i, t: (i, t, 0, 0))
    return pl.pallas_call(
        _qkv_kernel,
        grid=(b, s // QKV_TILE),
        in_specs=[pl.BlockSpec((None, QKV_TILE, d), lambda i, t: (i, t, 0)),
                  _const_spec(wq_t.shape), _const_spec(wk.shape), _const_spec(wv_t.shape)],
        out_specs=[t_spec,
                   pl.BlockSpec((None, QKV_TILE, ATTN_WIDTH), lambda i, t: (i, t, 0)),
                   t_spec],
        out_shape=[t_shape, jax.ShapeDtypeStruct((b, s, ATTN_WIDTH), BF16), t_shape],
        compiler_params=_params(2),
        name="qkv_proj",
    )(x, wq_t, wk, wv_t)


def _split_bf16(x):
    parts = []
    rem = x
    for _ in range(PEN_SPLIT):
        part = rem.astype(BF16)
        parts.append(part)
        rem = rem - part.astype(F32)
    return parts


def _select_blocks(gate, i):
    nb = gate.shape[0]
    bidx = lax.broadcasted_iota(jnp.int32, gate.shape, 0)
    avail = jnp.where(bidx < i, 1, 0)
    chosen = jnp.zeros(gate.shape, jnp.int32)
    for _ in range(MOBA_TOPK):
        live = avail == 1
        top = jnp.max(jnp.where(live, gate, -jnp.inf), axis=0, keepdims=True)
        cand = jnp.where(live, jnp.where(gate == top, bidx, nb), nb)
        pick = jnp.where(bidx == jnp.min(cand, axis=0, keepdims=True), 1, 0)
        chosen = chosen + pick
        avail = avail - pick
    return chosen


N0, N1 = SCORE_SLOTS, SCORE_SLOTS + 1


def _attn_kernel(rb_ref, qt_ref, qnext_ref, k_ref, vt_ref, bown_ref, bprev_ref, o_ref,
                 kmean_ref, kpad_ref, qa_ref, s_ref, m_ref, acc_ref):
    i = pl.program_id(1)
    nb = vt_ref.shape[0]
    blk = MOBA_BLOCK
    cur = i & 1
    nxt = 1 - cur
    bidx = lax.broadcasted_iota(jnp.int32, (nb, blk), 0)

    def put_scores(h, j, slot, half):
        start = pl.multiple_of(j * blk, blk)
        s_ref[slot, h] = jnp.dot(kpad_ref[pl.ds(start, blk), h * K_PAD:(h + 1) * K_PAD],
                                 qa_ref[half, h], preferred_element_type=F32)

    def score_operands(q_ref, tile):
        gates = []
        for h in range(N_HEADS):
            q = q_ref[h * HEAD_DIM:(h + 1) * HEAD_DIM, :]
            parts = _split_bf16(kmean_ref[:, h * HEAD_DIM:(h + 1) * HEAD_DIM])
            g3 = jnp.dot(jnp.concatenate(parts, axis=0), q, preferred_element_type=F32)
            gate = g3[0:nb]
            for r in range(1, PEN_SPLIT):
                gate = gate + g3[r * nb:(r + 1) * nb]
            gates.append(gate)
        operands = []
        for h in range(N_HEADS):
            chosen = _select_blocks(gates[h], tile)
            far_bias = jnp.where(bidx < tile - 1, rb_ref[h, N_BUCKETS - 1] * LOG2E, 0.0)
            pen = jnp.where(bidx == tile, 0.0, jnp.where(chosen == 1, far_bias, NEG))
            operands.append(jnp.concatenate(
                [q_ref[h * HEAD_DIM:(h + 1) * HEAD_DIM, :]] + _split_bf16(pen)
                + [jnp.zeros((K_PAD - HEAD_DIM - PEN_SPLIT * nb, blk), BF16)], axis=0))
        return operands

    def store_operands(operands, half):
        for h in range(N_HEADS):
            qa_ref[half, h] = operands[h]

    @pl.when(i == 0)
    def _():
        pad_lane = lax.broadcasted_iota(jnp.int32, (blk, K_PAD - HEAD_DIM), 1)
        for j in range(nb):
            rows = slice(j * blk, (j + 1) * blk)
            kj = k_ref[rows, :]
            kmean_ref[j:j + 1, :] = jnp.sum(kj.astype(F32), axis=0, keepdims=True) * (1.0 / blk)
            hot = pad_lane == j
            for rep in range(1, PEN_SPLIT):
                hot = hot | (pad_lane == j + rep * nb)
            one_hot = jnp.where(hot, 1.0, 0.0).astype(BF16)
            for h in range(N_HEADS):
                kpad_ref[rows, h * K_PAD:(h + 1) * K_PAD] = jnp.concatenate(
                    [kj[:, h * HEAD_DIM:(h + 1) * HEAD_DIM], one_hot], axis=1)
        store_operands(score_operands(qt_ref, 0), 0)
        for h in range(N_HEADS):
            put_scores(h, 0, N0, 0)

    m_ref[...] = jnp.full(m_ref.shape, M_INIT, F32)
    acc_ref[...] = jnp.zeros(acc_ref.shape, F32)

    ones = jnp.ones((V_ONES, blk), BF16)

    half = blk // 2

    def absorb(h, j, slot, kind):
        m_old = m_ref[h]
        if kind == "own":
            top = s_ref[slot, h, 0:half, :] + bown_ref[h, 0:half, :]
            bot = s_ref[slot, h, half:blk, half:blk] + bown_ref[h, half:blk, half:blk]
            m_top = jnp.max(top, axis=0, keepdims=True)
            m_blk = jnp.concatenate(
                [m_top[:, 0:half],
                 jnp.maximum(m_top[:, half:blk], jnp.max(bot, axis=0, keepdims=True))], axis=1)
            m_new = jnp.maximum(m_old, m_blk)
            p_bot = jnp.exp2(bot - m_new[:, half:blk]).astype(BF16)
            p = jnp.concatenate(
                [jnp.exp2(top - m_new).astype(BF16),
                 jnp.concatenate([jnp.zeros((half, half), BF16), p_bot], axis=1)], axis=0)
        else:
            s = s_ref[slot, h]
            if kind == "prev":
                s = s + bprev_ref[h]
            m_new = jnp.maximum(m_old, jnp.max(s, axis=0, keepdims=True))
            p = jnp.exp2(s - m_new).astype(BF16)
        va = jnp.concatenate([vt_ref[j, h * HEAD_DIM:(h + 1) * HEAD_DIM, :], ones], axis=0)
        acc_ref[h] = jnp.exp2(m_old - m_new) * acc_ref[h] + jnp.dot(va, p, preferred_element_type=F32)
        m_ref[h] = m_new

    def stage(j, slot, kind, fill=None):
        if fill is not None:
            for h in range(FILL_AHEAD):
                put_scores(h, *fill)
        for h in range(N_HEADS):
            if fill is not None and h + FILL_AHEAD < N_HEADS:
                put_scores(h + FILL_AHEAD, *fill)
            absorb(h, j, slot, kind)

    next0, next1 = (0, N0, nxt), (1, N1, nxt)

    @pl.when(i == 0)
    def _():
        ops_next = score_operands(qnext_ref, i + 1)
        stage(0, N0, "own")
        store_operands(ops_next, nxt)
        for h in range(N_HEADS):
            put_scores(h, *next0)
            put_scores(h, *next1)

    @pl.when(i == 1)
    def _():
        ops_next = score_operands(qnext_ref, i + 1)
        stage(0, N0, "prev")
        store_operands(ops_next, nxt)
        stage(1, N1, "own", fill=next0)
        for h in range(N_HEADS):
            put_scores(h, *next1)

    @pl.when(i == 2)
    def _():
        ops_next = score_operands(qnext_ref, i + 1)
        stage(0, N0, "far", fill=(2, 2, cur))
        store_operands(ops_next, nxt)
        stage(1, N1, "prev", fill=next0)
        stage(2, 2, "own", fill=next1)

    @pl.when(i >= 3)
    def _():
        ops_next = score_operands(qnext_ref, i + 1)
        stage(0, N0, "far", fill=(2, 2, cur))
        stage(1, N1, "far", fill=(3, 0, cur))
        store_operands(ops_next, nxt)

    def triple_body(a, carry):
        j0 = 2 + 3 * a
        stage(j0, 2, "far", fill=(j0 + 2, 1, cur))
        stage(j0 + 1, 0, "far", fill=(j0 + 3, 2, cur))
        stage(j0 + 2, 1, "far", fill=(j0 + 4, 0, cur))
        return carry

    n_triples = lax.div(jnp.maximum(i - 3, 0), 3)
    lax.fori_loop(0, n_triples, triple_body, 0)
    rest = jnp.where(i >= 3, i - 1 - 3 * n_triples, 0)

    @pl.when(rest == 2)
    def _():
        stage(i - 1, 2, "prev", fill=next0)
        stage(i, 0, "own", fill=next1)

    @pl.when(rest == 3)
    def _():
        stage(i - 2, 2, "far", fill=(i, 1, cur))
        stage(i - 1, 0, "prev", fill=next0)
        stage(i, 1, "own", fill=next1)

    @pl.when(rest == 4)
    def _():
        stage(i - 3, 2, "far", fill=(i - 1, 1, cur))
        stage(i - 2, 0, "far", fill=(i, 2, cur))
        stage(i - 1, 1, "prev", fill=next0)
        stage(i, 2, "own", fill=next1)

    for h in range(N_HEADS):
        acc = acc_ref[h]
        o_ref[h * HEAD_DIM:(h + 1) * HEAD_DIM, :] = (
            acc[0:HEAD_DIM] / acc[HEAD_DIM:HEAD_DIM + 1]).astype(o_ref.dtype)


def _moba_attention(rel_bias, qt, k, vt, bias_own, bias_prev):
    b, nb, _, blk = qt.shape
    s = nb * blk
    q_tile = pl.BlockSpec((None, None, ATTN_WIDTH, blk), lambda bi, i: (bi, i, 0, 0))
    return pl.pallas_call(
        _attn_kernel,
        grid=(b, nb),
        in_specs=[pl.BlockSpec(memory_space=pltpu.SMEM),
                  q_tile,
                  pl.BlockSpec((None, None, ATTN_WIDTH, blk),
                               lambda bi, i: (bi, jnp.minimum(i + 1, nb - 1), 0, 0)),
                  pl.BlockSpec((None, s, ATTN_WIDTH), lambda bi, i: (bi, 0, 0)),
                  pl.BlockSpec((None, nb, ATTN_WIDTH, blk), lambda bi, i: (bi, 0, 0, 0)),
                  _const_spec(bias_own.shape), _const_spec(bias_prev.shape)],
        out_specs=q_tile,
        out_shape=jax.ShapeDtypeStruct((b, nb, ATTN_WIDTH, blk), BF16),
        scratch_shapes=[pltpu.VMEM((nb, ATTN_WIDTH), F32),
                        pltpu.VMEM((s, N_HEADS * K_PAD), BF16),
                        pltpu.VMEM((2, N_HEADS, K_PAD, blk), BF16),
                        pltpu.VMEM((SCORE_SLOTS + 2, N_HEADS, blk, blk), F32),
                        pltpu.VMEM((N_HEADS, 1, blk), F32),
                        pltpu.VMEM((N_HEADS, HEAD_DIM + V_ONES, blk), F32)],
        compiler_params=_params(2),
        name="moba_attention",
    )(rel_bias, qt, qt, k, vt, bias_own, bias_prev)


def _layer_norm(z, g, b):
    mu = jnp.mean(z, axis=-1, keepdims=True)
    zc = z - mu
    var = jnp.mean(zc * zc, axis=-1, keepdims=True)
    return zc * lax.rsqrt(var + LN_EPS) * g + b


def _mixer_kernel(alpha, x_ref, at_ref, wp_ref, wg_ref, wpool_ref, pscale_ref, wba_ref, wbp_ref,
                  wout_ref, g_ref, b_ref, h_ref, pbuf_ref, mixed_ref):
    t = pl.program_id(1)
    tile = x_ref.shape[0]
    d = x_ref.shape[1]

    @pl.when(t == 0)
    def _():
        pbuf_ref[0:POOL_HALO, :] = jnp.zeros((POOL_HALO, POOL_WIDTH), F32)

    x = x_ref[...]
    xb = x.astype(BF16)
    p = jnp.dot(xb, wp_ref[...], preferred_element_type=F32)

    def cols(c, base=0):
        return slice(base + c * MXU_TILE, base + (c + 1) * MXU_TILE)

    def gate_logits(c):
        return (jnp.dot(xb, wg_ref[:, cols(c)], preferred_element_type=F32),
                jnp.dot(xb, wg_ref[:, cols(c, d)], preferred_element_type=F32))

    nxt = gate_logits(0)
    attn = jnp.concatenate([at_ref[c].T for c in range(at_ref.shape[0])], axis=0)

    pbuf_ref[POOL_HALO:POOL_HALO + tile, :] = p
    pos = t * tile + lax.broadcasted_iota(jnp.int32, (tile, POOL_GROUP), 0)
    ys = []
    for gi, w in enumerate(POOL_WINDOWS):
        group = slice(gi * POOL_GROUP, (gi + 1) * POOL_GROUP)
        pg = p[:, group]
        win = pg
        for back in range(1, w):
            win = win + pbuf_ref[POOL_HALO - back:POOL_HALO - back + tile, group]
        cnt = jnp.minimum(pos + 1, w).astype(F32)
        diff = win / cnt - pg
        ys.append(jnp.dot(diff.astype(BF16), wpool_ref[gi], preferred_element_type=F32))
    pbuf_ref[0:POOL_HALO, :] = pbuf_ref[tile:tile + POOL_HALO, :]
    yb = (jnp.concatenate(ys, axis=1) * pscale_ref[...]).astype(BF16)

    for c in range(d // MXU_TILE):
        gl_attn, gl_pool = nxt
        if (c + 1) * MXU_TILE < d:
            nxt = gate_logits(c + 1)
        y_attn = jnp.dot(attn, wba_ref[:, cols(c)], preferred_element_type=F32)
        y_pool = jnp.dot(yb, wbp_ref[:, cols(c)], preferred_element_type=F32)
        mixed_ref[:, cols(c)] = (y_attn / (1.0 + jnp.exp(-gl_attn))
                                 + y_pool / (1.0 + jnp.exp(-gl_pool))).astype(BF16)
    for rows in (slice(0, tile // 2), slice(tile // 2, tile)):
        z = alpha * x[rows] + jnp.dot(mixed_ref[rows, :], wout_ref[...], preferred_element_type=F32)
        h_ref[rows, :] = _layer_norm(z, g_ref[...], b_ref[...])


def _mixer(alpha, x, attn_t, wp, wg, wpool, pscale, wba, wbp, wout, ln_g, ln_b):
    b, s, d = x.shape
    bpt = TOKEN_TILE // MOBA_BLOCK
    consts = (wp, wg, wpool, pscale, wba, wbp, wout, ln_g, ln_b)
    return pl.pallas_call(
        functools.partial(_mixer_kernel, alpha),
        grid=(b, s // TOKEN_TILE),
        in_specs=[pl.BlockSpec((None, TOKEN_TILE, d), lambda i, t: (i, t, 0)),
                  pl.BlockSpec((None, bpt, ATTN_WIDTH, MOBA_BLOCK), lambda i, t: (i, t, 0, 0))]
                 + [_const_spec(c.shape) for c in consts],
        out_specs=pl.BlockSpec((None, TOKEN_TILE, d), lambda i, t: (i, t, 0)),
        out_shape=jax.ShapeDtypeStruct((b, s, d), F32),
        scratch_shapes=[pltpu.VMEM((POOL_HALO + TOKEN_TILE, POOL_WIDTH), F32),
                        pltpu.VMEM((TOKEN_TILE, d), BF16)],
        compiler_params=_params(2),
        name="mixer",
    )(x, attn_t, *consts)


def _ffn_kernel(alpha, tiles_per_seq, n_tiles, h_ref, win_ref, cw_ref, cb_ref, wo_ref, g_ref, b_ref,
                o_ref, carry_ref, z_ref, act_ref):
    g = pl.program_id(0)
    tile = h_ref.shape[0]
    d_ff = wo_ref.shape[0]
    n_chunks = d_ff // FF_CHUNK

    @pl.when(g == 0)
    def _():
        z_ref[...] = jnp.zeros(z_ref.shape, F32)

    @pl.when(g % tiles_per_seq == 0)
    def _():
        carry_ref[...] = jnp.zeros(carry_ref.shape, F32)

    @pl.when(g == n_tiles)
    def _():
        o_ref[...] = _layer_norm(z_ref[...], g_ref[...], b_ref[...])

    @pl.when(g < n_tiles)
    def _():
        o_ref[...] = _layer_norm(z_ref[...], g_ref[...], b_ref[...])
        _ffn_tile(alpha, tile, n_chunks, d_ff, h_ref, win_ref, cw_ref, cb_ref, wo_ref, carry_ref, z_ref, act_ref)


def _ffn_tile(alpha, tile, n_chunks, d_ff, h_ref, win_ref, cw_ref, cb_ref, wo_ref, carry_ref, z_ref, act_ref):
    h = h_ref[...]
    hb = h.astype(BF16)
    row = lax.broadcasted_iota(jnp.int32, (CONV_HALO, FF_CHUNK), 0)

    def cols(c, base=0):
        return slice(base + c * FF_CHUNK, base + (c + 1) * FF_CHUNK)

    def project(c):
        return (jnp.dot(hb, win_ref[:, cols(c)], preferred_element_type=F32),
                jnp.dot(hb, win_ref[:, cols(c, d_ff)], preferred_element_type=F32))

    def shifted(a, prev, back):
        rolled = pltpu.roll(a, back, 0)
        top = rolled[0:CONV_HALO]
        for r in range(back):
            top = jnp.where(row == r, prev[CONV_HALO - back + r:CONV_HALO - back + r + 1, :], top)
        return jnp.concatenate([top, rolled[CONV_HALO:]], axis=0)

    nxt = project(0)
    for c in range(n_chunks):
        a, u = nxt
        if c + 1 < n_chunks:
            nxt = project(c + 1)
        prev = carry_ref[c]
        carry_ref[c] = a[tile - CONV_HALO:tile, :]
        cw = cw_ref[:, cols(c)]
        y = cb_ref[:, cols(c)]
        for tap in range(CONV_WIDTH):
            back = CONV_WIDTH - 1 - tap
            src = a if back == 0 else shifted(a, prev, back)
            y = y + src * cw[tap:tap + 1, :]
        act = 0.5 * y * (1.0 + lax.erf(y * math.sqrt(0.5))) * u
        act_ref[:, cols(c)] = act.astype(BF16)
    split = (n_chunks - 1) * FF_CHUNK
    f = (jnp.dot(act_ref[:, :split], wo_ref[:split, :], preferred_element_type=F32)
         + jnp.dot(act_ref[:, split:], wo_ref[split:, :], preferred_element_type=F32))
    z_ref[...] = alpha * h + f


def _conv_ffn(alpha, h, win, cw, cb, wo, ln_g, ln_b):
    b, s, d = h.shape
    d_ff = wo.shape[0]
    tiles_per_seq = s // FFN_TILE
    n_tiles = b * tiles_per_seq
    consts = (win, cw, cb, wo, ln_g, ln_b)

    def tile_index(g):
        return (g // tiles_per_seq, g % tiles_per_seq, 0)

    return pl.pallas_call(
        functools.partial(_ffn_kernel, alpha, tiles_per_seq, n_tiles),
        grid=(n_tiles + 1,),
        in_specs=[pl.BlockSpec((None, FFN_TILE, d), lambda g: tile_index(jnp.minimum(g, n_tiles - 1)))]
                 + [_const_spec(c.shape) for c in consts],
        out_specs=pl.BlockSpec((None, FFN_TILE, d), lambda g: tile_index(jnp.maximum(g - 1, 0))),
        out_shape=jax.ShapeDtypeStruct((b, s, d), F32),
        scratch_shapes=[pltpu.VMEM((d_ff // FF_CHUNK, CONV_HALO, FF_CHUNK), F32),
                        pltpu.VMEM((FFN_TILE, d), F32),
                        pltpu.VMEM((FFN_TILE, d_ff), BF16)],
        compiler_params=_params(1),
        name="conv_ffn",
    )(h, *consts)


def kernel(x, w_in, rel_bias, w_pool_group, pool_scale, w_branch_attn, w_branch_pool, w_out, ln1_g,
           ln1_b, w_ffn_in, conv_w, conv_b, w_ffn_out, ln2_g, ln2_b):
    b, s, d = x.shape
    depth = w_in.shape[0]
    d_ff = w_ffn_out.shape[1]
    assert s % TOKEN_TILE == 0 and TOKEN_TILE % MOBA_BLOCK == 0 and d_ff % FF_CHUNK == 0
    assert s % FFN_TILE == 0 and s % QKV_TILE == 0 and QKV_TILE % MOBA_BLOCK == 0
    assert w_in.shape[2] == 3 * ATTN_WIDTH + POOL_WIDTH + 2 * d
    alpha = (2.0 * depth) ** 0.25
    assert (K_PAD - HEAD_DIM - PEN_SPLIT * (s // MOBA_BLOCK)) % BF16_SUBLANES == 0
    assert K_PAD - HEAD_DIM - PEN_SPLIT * (s // MOBA_BLOCK) > 0
    scale = LOG2E / math.sqrt(HEAD_DIM)

    bias_own, bias_prev = _rel_bias_tables(rel_bias)
    h = x
    for l in range(depth):
        w = w_in[l]
        wq_t = (w[:, :ATTN_WIDTH] * scale).T.astype(BF16)
        wk = w[:, ATTN_WIDTH:2 * ATTN_WIDTH].astype(BF16)
        wv_t = w[:, 2 * ATTN_WIDTH:3 * ATTN_WIDTH].T.astype(BF16)
        qt, k, vt = _qkv_proj(h, wq_t, wk, wv_t)
        attn_t = _moba_attention(rel_bias, qt, k, vt, bias_own, bias_prev)
        h1 = _mixer(alpha, h, attn_t,
                    w[:, 3 * ATTN_WIDTH:3 * ATTN_WIDTH + POOL_WIDTH].astype(BF16),
                    w[:, 3 * ATTN_WIDTH + POOL_WIDTH:].astype(BF16),
                    w_pool_group[l].astype(BF16),
                    pool_scale[l].reshape(1, POOL_WIDTH),
                    w_branch_attn[l].astype(BF16),
                    w_branch_pool[l].astype(BF16),
                    w_out[l].astype(BF16),
                    ln1_g[l].reshape(1, d), ln1_b[l].reshape(1, d))
        h = _conv_ffn(alpha, h1, w_ffn_in[l].astype(BF16), conv_w[l], conv_b[l].reshape(1, d_ff),
                      w_ffn_out[l].astype(BF16), ln2_g[l].reshape(1, d), ln2_b[l].reshape(1, d))
    return h
```

```python
import functools
import math

import jax
import jax.numpy as jnp
from jax import lax
from jax.experimental import pallas as pl
from jax.experimental.pallas import tpu as pltpu

N_HEADS = 8
HEAD_DIM = 64
ATTN_WIDTH = N_HEADS * HEAD_DIM
MOBA_BLOCK = 256
MOBA_TOPK = 3
POOL_WINDOWS = (2, 4, 8, 16)
POOL_GROUP = 128
POOL_WIDTH = POOL_GROUP * len(POOL_WINDOWS)
N_BUCKETS = 32
MAX_DISTANCE = 128
CONV_WIDTH = 3
LN_EPS = 1e-5
NEG = -1e30
M_INIT = 0.5 * NEG
LOG2E = math.log2(math.e)
FILL_AHEAD = 2
PEN_SPLIT = 3

LANES = 128
K_PAD = LANES
BF16_SUBLANES = 16
MXU_TILE = 256
V_ONES = BF16_SUBLANES
SCORE_SLOTS = 3
TAIL_SPLIT = 4
POOL_HALO = max(POOL_WINDOWS)
CONV_HALO = 8
FF_CHUNK = MXU_TILE
TOKEN_TILE = 1024
QKV_TILE = 1024
FFN_TILE = 1024
VMEM_LIMIT = 56 * 1024 * 1024

F32 = jnp.float32
BF16 = jnp.bfloat16
NT_DIMS = (((1,), (1,)), ((), ()))
TN_DIMS = (((0,), (0,)), ((), ()))


def _params(n_grid):
    return pltpu.CompilerParams(dimension_semantics=("arbitrary",) * n_grid,
                                vmem_limit_bytes=VMEM_LIMIT)


def _const_spec(shape):
    nd = len(shape)
    return pl.BlockSpec(shape, lambda *_: (0,) * nd)


def _rel_bias_kernel(rb_ref, own_ref, prev_ref):
    h = pl.program_id(0)
    kk = lax.broadcasted_iota(jnp.int32, (MOBA_BLOCK, MOBA_BLOCK), 0)
    qq = lax.broadcasted_iota(jnp.int32, (MOBA_BLOCK, MOBA_BLOCK), 1)
    max_exact = N_BUCKETS // 2
    n_log = N_BUCKETS - max_exact
    first_n = [math.ceil(max_exact * (MAX_DISTANCE / max_exact) ** (k / n_log)) for k in range(n_log)]

    def table(dist):
        n = jnp.maximum(dist, 0)
        out = jnp.zeros(dist.shape, F32)
        for b in range(max_exact):
            out = jnp.where(n == b, rb_ref[h, b], out)
        for k in range(n_log):
            out = jnp.where(n >= first_n[k], rb_ref[h, max_exact + k], out)
        return out

    own_ref[...] = jnp.where(qq >= kk, table(qq - kk) * LOG2E, NEG)
    prev_ref[...] = table(qq - kk + MOBA_BLOCK) * LOG2E


def _rel_bias_tables(rel_bias):
    shape = jax.ShapeDtypeStruct((N_HEADS, MOBA_BLOCK, MOBA_BLOCK), F32)
    spec = pl.BlockSpec((None, MOBA_BLOCK, MOBA_BLOCK), lambda h: (h, 0, 0))
    return pl.pallas_call(
        _rel_bias_kernel,
        grid=(N_HEADS,),
        in_specs=[pl.BlockSpec(memory_space=pltpu.SMEM)],
        out_specs=[spec, spec],
        out_shape=[shape, shape],
        compiler_params=_params(1),
        name="rel_bias_tables",
    )(rel_bias)


def _qkv_kernel(x_ref, wq_ref, wk_ref, wv_ref, qt_ref, k_ref, vt_ref):
    xb = x_ref[...].astype(BF16)
    qt = lax.dot_general(wq_ref[...], xb, NT_DIMS, preferred_element_type=F32)
    vt = lax.dot_general(wv_ref[...], xb, NT_DIMS, preferred_element_type=F32)
    k_ref[...] = jnp.dot(xb, wk_ref[...], preferred_element_type=F32).astype(BF16)
    for c in range(qt_ref.shape[0]):
        cols = slice(c * MOBA_BLOCK, (c + 1) * MOBA_BLOCK)
        qt_ref[c] = qt[:, cols].astype(BF16)
        vt_ref[c] = vt[:, cols].astype(BF16)


def _qkv_proj(x, wq_t, wk, wv_t):
    b, s, d = x.shape
    nb = s // MOBA_BLOCK
    bpt = QKV_TILE // MOBA_BLOCK
    t_shape = jax.ShapeDtypeStruct((b, nb, ATTN_WIDTH, MOBA_BLOCK), BF16)
    t_spec = pl.BlockSpec((None, bpt, ATTN_WIDTH, MOBA_BLOCK), lambda i, t: (i, t, 0, 0))
    return pl.pallas_call(
        _qkv_kernel,
        grid=(b, s // QKV_TILE),
        in_specs=[pl.BlockSpec((None, QKV_TILE, d), lambda i, t: (i, t, 0)),
                  _const_spec(wq_t.shape), _const_spec(wk.shape), _const_spec(wv_t.shape)],
        out_specs=[t_spec,
                   pl.BlockSpec((None, QKV_TILE, ATTN_WIDTH), lambda i, t: (i, t, 0)),
                   t_spec],
        out_shape=[t_shape, jax.ShapeDtypeStruct((b, s, ATTN_WIDTH), BF16), t_shape],
        compiler_params=_params(2),
        name="qkv_proj",
    )(x, wq_t, wk, wv_t)


def _split_bf16(x):
    parts = []
    rem = x
    for _ in range(PEN_SPLIT):
        part = rem.astype(BF16)
        parts.append(part)
        rem = rem - part.astype(F32)
    return parts


def _select_blocks(gate, i):
    nb = gate.shape[0]
    bidx = lax.broadcasted_iota(jnp.int32, gate.shape, 0)
    avail = jnp.where(bidx < i, 1, 0)
    chosen = jnp.zeros(gate.shape, jnp.int32)
    for _ in range(MOBA_TOPK):
        live = avail == 1
        top = jnp.max(jnp.where(live, gate, -jnp.inf), axis=0, keepdims=True)
        cand = jnp.where(live, jnp.where(gate == top, bidx, nb), nb)
        pick = jnp.where(bidx == jnp.min(cand, axis=0, keepdims=True), 1, 0)
        chosen = chosen + pick
        avail = avail - pick
    return chosen


N0, N1 = SCORE_SLOTS, SCORE_SLOTS + 1


def _attn_kernel(rb_ref, qt_ref, qnext_ref, k_ref, vt_ref, bown_ref, bprev_ref, o_ref,
                 kmean_ref, kpad_ref, qa_ref, s_ref, m_ref, acc_ref):
    i = pl.program_id(1)
    nb = vt_ref.shape[0]
    blk = MOBA_BLOCK
    cur = i & 1
    nxt = 1 - cur
    bidx = lax.broadcasted_iota(jnp.int32, (nb, blk), 0)

    def put_scores(h, j, slot, half):
        start = pl.multiple_of(j * blk, blk)
        s_ref[slot, h] = jnp.dot(kpad_ref[pl.ds(start, blk), h * K_PAD:(h + 1) * K_PAD],
                                 qa_ref[half, h], preferred_element_type=F32)

    def score_operands(q_ref, tile):
        gates = []
        for h in range(N_HEADS):
            q = q_ref[h * HEAD_DIM:(h + 1) * HEAD_DIM, :]
            parts = _split_bf16(kmean_ref[:, h * HEAD_DIM:(h + 1) * HEAD_DIM])
            g3 = jnp.dot(jnp.concatenate(parts, axis=0), q, preferred_element_type=F32)
            gate = g3[0:nb]
            for r in range(1, PEN_SPLIT):
                gate = gate + g3[r * nb:(r + 1) * nb]
            gates.append(gate)
        operands = []
        for h in range(N_HEADS):
            chosen = _select_blocks(gates[h], tile)
            far_bias = jnp.where(bidx < tile - 1, rb_ref[h, N_BUCKETS - 1] * LOG2E, 0.0)
            pen = jnp.where(bidx == tile, 0.0, jnp.where(chosen == 1, far_bias, NEG))
            operands.append(jnp.concatenate(
                [q_ref[h * HEAD_DIM:(h + 1) * HEAD_DIM, :]] + _split_bf16(pen)
                + [jnp.zeros((K_PAD - HEAD_DIM - PEN_SPLIT * nb, blk), BF16)], axis=0))
        return operands

    def store_operands(operands, half):
        for h in range(N_HEADS):
            qa_ref[half, h] = operands[h]

    @pl.when(i == 0)
    def _():
        pad_lane = lax.broadcasted_iota(jnp.int32, (blk, K_PAD - HEAD_DIM), 1)
        for j in range(nb):
            rows = slice(j * blk, (j + 1) * blk)
            kj = k_ref[rows, :]
            kmean_ref[j:j + 1, :] = jnp.sum(kj.astype(F32), axis=0, keepdims=True) * (1.0 / blk)
            hot = pad_lane == j
            for rep in range(1, PEN_SPLIT):
                hot = hot | (pad_lane == j + rep * nb)
            one_hot = jnp.where(hot, 1.0, 0.0).astype(BF16)
            for h in range(N_HEADS):
                kpad_ref[rows, h * K_PAD:(h + 1) * K_PAD] = jnp.concatenate(
                    [kj[:, h * HEAD_DIM:(h + 1) * HEAD_DIM], one_hot], axis=1)
        store_operands(score_operands(qt_ref, 0), 0)
        for h in range(N_HEADS):
            put_scores(h, 0, N0, 0)

    m_ref[...] = jnp.full(m_ref.shape, M_INIT, F32)
    acc_ref[...] = jnp.zeros(acc_ref.shape, F32)

    ones = jnp.ones((V_ONES, blk), BF16)

    half = blk // 2

    def absorb(h, j, slot, kind):
        m_old = m_ref[h]
        if kind == "own":
            top = s_ref[slot, h, 0:half, :] + bown_ref[h, 0:half, :]
            bot = s_ref[slot, h, half:blk, half:blk] + bown_ref[h, half:blk, half:blk]
            m_top = jnp.max(top, axis=0, keepdims=True)
            m_blk = jnp.concatenate(
                [m_top[:, 0:half],
                 jnp.maximum(m_top[:, half:blk], jnp.max(bot, axis=0, keepdims=True))], axis=1)
            m_new = jnp.maximum(m_old, m_blk)
            p_bot = jnp.exp2(bot - m_new[:, half:blk]).astype(BF16)
            p = jnp.concatenate(
                [jnp.exp2(top - m_new).astype(BF16),
                 jnp.concatenate([jnp.zeros((half, half), BF16), p_bot], axis=1)], axis=0)
        else:
            s = s_ref[slot, h]
            if kind == "prev":
                s = s + bprev_ref[h]
            m_new = jnp.maximum(m_old, jnp.max(s, axis=0, keepdims=True))
            p = jnp.exp2(s - m_new).astype(BF16)
        va = jnp.concatenate([vt_ref[j, h * HEAD_DIM:(h + 1) * HEAD_DIM, :], ones], axis=0)
        acc_ref[h] = jnp.exp2(m_old - m_new) * acc_ref[h] + jnp.dot(va, p, preferred_element_type=F32)
        m_ref[h] = m_new

    def stage(j, slot, kind, fill=None):
        if fill is not None:
            for h in range(FILL_AHEAD):
                put_scores(h, *fill)
        for h in range(N_HEADS):
            if fill is not None and h + FILL_AHEAD < N_HEADS:
                put_scores(h + FILL_AHEAD, *fill)
            absorb(h, j, slot, kind)

    next0, next1 = (0, N0, nxt), (1, N1, nxt)

    @pl.when(i == 0)
    def _():
        ops_next = score_operands(qnext_ref, i + 1)
        stage(0, N0, "own")
        store_operands(ops_next, nxt)
        for h in range(N_HEADS):
            put_scores(h, *next0)
            put_scores(h, *next1)

    @pl.when(i == 1)
    def _():
        ops_next = score_operands(qnext_ref, i + 1)
        stage(0, N0, "prev")
        store_operands(ops_next, nxt)
        stage(1, N1, "own", fill=next0)
        for h in range(N_HEADS):
            put_scores(h, *next1)

    @pl.when(i == 2)
    def _():
        ops_next = score_operands(qnext_ref, i + 1)
        stage(0, N0, "far", fill=(2, 2, cur))
        store_operands(ops_next, nxt)
        stage(1, N1, "prev", fill=next0)
        stage(2, 2, "own", fill=next1)

    @pl.when(i >= 3)
    def _():
        ops_next = score_operands(qnext_ref, i + 1)
        stage(0, N0, "far", fill=(2, 2, cur))
        stage(1, N1, "far", fill=(3, 0, cur))
        store_operands(ops_next, nxt)

    def triple_body(a, carry):
        j0 = 2 + 3 * a
        stage(j0, 2, "far", fill=(j0 + 2, 1, cur))
        stage(j0 + 1, 0, "far", fill=(j0 + 3, 2, cur))
        stage(j0 + 2, 1, "far", fill=(j0 + 4, 0, cur))
        return carry

    n_triples = lax.div(jnp.maximum(i - 3, 0), 3)
    lax.fori_loop(0, n_triples, triple_body, 0)
    rest = jnp.where(i >= 3, i - 1 - 3 * n_triples, 0)

    @pl.when(rest == 2)
    def _():
        stage(i - 1, 2, "prev", fill=next0)
        stage(i, 0, "own", fill=next1)

    @pl.when(rest == 3)
    def _():
        stage(i - 2, 2, "far", fill=(i, 1, cur))
        stage(i - 1, 0, "prev", fill=next0)
        stage(i, 1, "own", fill=next1)

    @pl.when(rest == 4)
    def _():
        stage(i - 3, 2, "far", fill=(i - 1, 1, cur))
        stage(i - 2, 0, "far", fill=(i, 2, cur))
        stage(i - 1, 1, "prev", fill=next0)
        stage(i, 2, "own", fill=next1)

    for h in range(N_HEADS):
        acc = acc_ref[h]
        o_ref[h * HEAD_DIM:(h + 1) * HEAD_DIM, :] = (
            acc[0:HEAD_DIM] / acc[HEAD_DIM:HEAD_DIM + 1]).astype(o_ref.dtype)


def _moba_attention(rel_bias, qt, k, vt, bias_own, bias_prev):
    b, nb, _, blk = qt.shape
    s = nb * blk
    q_tile = pl.BlockSpec((None, None, ATTN_WIDTH, blk), lambda bi, i: (bi, i, 0, 0))
    return pl.pallas_call(
        _attn_kernel,
        grid=(b, nb),
        in_specs=[pl.BlockSpec(memory_space=pltpu.SMEM),
                  q_tile,
                  pl.BlockSpec((None, None, ATTN_WIDTH, blk),
                               lambda bi, i: (bi, jnp.minimum(i + 1, nb - 1), 0, 0)),
                  pl.BlockSpec((None, s, ATTN_WIDTH), lambda bi, i: (bi, 0, 0)),
                  pl.BlockSpec((None, nb, ATTN_WIDTH, blk), lambda bi, i: (bi, 0, 0, 0)),
                  _const_spec(bias_own.shape), _const_spec(bias_prev.shape)],
        out_specs=q_tile,
        out_shape=jax.ShapeDtypeStruct((b, nb, ATTN_WIDTH, blk), BF16),
        scratch_shapes=[pltpu.VMEM((nb, ATTN_WIDTH), F32),
                        pltpu.VMEM((s, N_HEADS * K_PAD), BF16),
                        pltpu.VMEM((2, N_HEADS, K_PAD, blk), BF16),
                        pltpu.VMEM((SCORE_SLOTS + 2, N_HEADS, blk, blk), F32),
                        pltpu.VMEM((N_HEADS, 1, blk), F32),
                        pltpu.VMEM((N_HEADS, HEAD_DIM + V_ONES, blk), F32)],
        compiler_params=_params(2),
        name="moba_attention",
    )(rel_bias, qt, qt, k, vt, bias_own, bias_prev)


def _layer_norm(z, g, b):
    mu = jnp.mean(z, axis=-1, keepdims=True)
    zc = z - mu
    var = jnp.mean(zc * zc, axis=-1, keepdims=True)
    return zc * lax.rsqrt(var + LN_EPS) * g + b


def _mixer_kernel(alpha, x_ref, at_ref, wp_ref, wg_ref, wpool_ref, pscale_ref, wba_ref, wbp_ref,
                  wout_ref, g_ref, b_ref, h_ref, pbuf_ref):
    t = pl.program_id(1)
    tile = x_ref.shape[0]
    d = x_ref.shape[1]

    @pl.when(t == 0)
    def _():
        pbuf_ref[0:POOL_HALO, :] = jnp.zeros((POOL_HALO, POOL_WIDTH), F32)

    x = x_ref[...]
    xb = x.astype(BF16)
    p = jnp.dot(xb, wp_ref[...], preferred_element_type=F32)
    gate_logits = jnp.dot(xb, wg_ref[...], preferred_element_type=F32)
    y_attn = jnp.concatenate(
        [lax.dot_general(at_ref[c], wba_ref[...], TN_DIMS, preferred_element_type=F32)
         for c in range(at_ref.shape[0])], axis=0)

    pbuf_ref[POOL_HALO:POOL_HALO + tile, :] = p
    pos = t * tile + lax.broadcasted_iota(jnp.int32, (tile, POOL_GROUP), 0)
    ys = []
    for gi, w in enumerate(POOL_WINDOWS):
        cols = slice(gi * POOL_GROUP, (gi + 1) * POOL_GROUP)
        pg = p[:, cols]
        win = pg
        for back in range(1, w):
            win = win + pbuf_ref[POOL_HALO - back:POOL_HALO - back + tile, cols]
        cnt = jnp.minimum(pos + 1, w).astype(F32)
        diff = win / cnt - pg
        ys.append(jnp.dot(diff.astype(BF16), wpool_ref[gi], preferred_element_type=F32))
    pbuf_ref[0:POOL_HALO, :] = pbuf_ref[tile:tile + POOL_HALO, :]
    y = jnp.concatenate(ys, axis=1) * pscale_ref[...]
    y_pool = jnp.dot(y.astype(BF16), wbp_ref[...], preferred_element_type=F32)

    gates = 1.0 / (1.0 + jnp.exp(-gate_logits))
    mixed = (gates[:, :d] * y_attn + gates[:, d:] * y_pool).astype(BF16)
    for rows in [slice(r, r + tile // TAIL_SPLIT) for r in range(0, tile, tile // TAIL_SPLIT)]:
        z = alpha * x[rows] + jnp.dot(mixed[rows], wout_ref[...], preferred_element_type=F32)
        h_ref[rows, :] = _layer_norm(z, g_ref[...], b_ref[...])


def _mixer(alpha, x, attn_t, wp, wg, wpool, pscale, wba, wbp, wout, ln_g, ln_b):
    b, s, d = x.shape
    bpt = TOKEN_TILE // MOBA_BLOCK
    consts = (wp, wg, wpool, pscale, wba, wbp, wout, ln_g, ln_b)
    return pl.pallas_call(
        functools.partial(_mixer_kernel, alpha),
        grid=(b, s // TOKEN_TILE),
        in_specs=[pl.BlockSpec((None, TOKEN_TILE, d), lambda i, t: (i, t, 0)),
                  pl.BlockSpec((None, bpt, ATTN_WIDTH, MOBA_BLOCK), lambda i, t: (i, t, 0, 0))]
                 + [_const_spec(c.shape) for c in consts],
        out_specs=pl.BlockSpec((None, TOKEN_TILE, d), lambda i, t: (i, t, 0)),
        out_shape=jax.ShapeDtypeStruct((b, s, d), F32),
        scratch_shapes=[pltpu.VMEM((POOL_HALO + TOKEN_TILE, POOL_WIDTH), F32)],
        compiler_params=_params(2),
        name="mixer",
    )(x, attn_t, *consts)


def _ffn_kernel(alpha, h_ref, win_ref, cw_ref, cb_ref, wo_ref, g_ref, b_ref, o_ref, carry_ref, act_ref):
    tile = h_ref.shape[0]
    d_ff = wo_ref.shape[0]
    n_chunks = d_ff // FF_CHUNK

    @pl.when(pl.program_id(1) == 0)
    def _():
        carry_ref[...] = jnp.zeros(carry_ref.shape, F32)

    h = h_ref[...]
    hb = h.astype(BF16)
    row = lax.broadcasted_iota(jnp.int32, (CONV_HALO, FF_CHUNK), 0)

    def cols(c, base=0):
        return slice(base + c * FF_CHUNK, base + (c + 1) * FF_CHUNK)

    def project(c):
        return (jnp.dot(hb, win_ref[:, cols(c)], preferred_element_type=F32),
                jnp.dot(hb, win_ref[:, cols(c, d_ff)], preferred_element_type=F32))

    def shifted(a, prev, back):
        rolled = pltpu.roll(a, back, 0)
        top = rolled[0:CONV_HALO]
        for r in range(back):
            top = jnp.where(row == r, prev[CONV_HALO - back + r:CONV_HALO - back + r + 1, :], top)
        return jnp.concatenate([top, rolled[CONV_HALO:]], axis=0)

    nxt = project(0)
    for c in range(n_chunks):
        a, u = nxt
        if c + 1 < n_chunks:
            nxt = project(c + 1)
        prev = carry_ref[c]
        carry_ref[c] = a[tile - CONV_HALO:tile, :]
        cw = cw_ref[:, cols(c)]
        y = cb_ref[:, cols(c)]
        for tap in range(CONV_WIDTH):
            back = CONV_WIDTH - 1 - tap
            src = a if back == 0 else shifted(a, prev, back)
            y = y + src * cw[tap:tap + 1, :]
        act = 0.5 * y * (1.0 + lax.erf(y * math.sqrt(0.5))) * u
        act_ref[:, cols(c)] = act.astype(BF16)
    split = (n_chunks - 1) * FF_CHUNK
    for rows in [slice(r, r + tile // TAIL_SPLIT) for r in range(0, tile, tile // TAIL_SPLIT)]:
        f = (jnp.dot(act_ref[rows, :split], wo_ref[:split, :], preferred_element_type=F32)
             + jnp.dot(act_ref[rows, split:], wo_ref[split:, :], preferred_element_type=F32))
        o_ref[rows, :] = _layer_norm(alpha * h[rows] + f, g_ref[...], b_ref[...])


def _conv_ffn(alpha, h, win, cw, cb, wo, ln_g, ln_b):
    b, s, d = h.shape
    d_ff = wo.shape[0]
    consts = (win, cw, cb, wo, ln_g, ln_b)
    return pl.pallas_call(
        functools.partial(_ffn_kernel, alpha),
        grid=(b, s // FFN_TILE),
        in_specs=[pl.BlockSpec((None, FFN_TILE, d), lambda i, t: (i, t, 0))]
                 + [_const_spec(c.shape) for c in consts],
        out_specs=pl.BlockSpec((None, FFN_TILE, d), lambda i, t: (i, t, 0)),
        out_shape=jax.ShapeDtypeStruct((b, s, d), F32),
        scratch_shapes=[pltpu.VMEM((d_ff // FF_CHUNK, CONV_HALO, FF_CHUNK), F32),
                        pltpu.VMEM((FFN_TILE, d_ff), BF16)],
        compiler_params=_params(2),
        name="conv_ffn",
    )(h, *consts)


def kernel(x, w_in, rel_bias, w_pool_group, pool_scale, w_branch_attn, w_branch_pool, w_out, ln1_g,
           ln1_b, w_ffn_in, conv_w, conv_b, w_ffn_out, ln2_g, ln2_b):
    b, s, d = x.shape
    depth = w_in.shape[0]
    d_ff = w_ffn_out.shape[1]
    assert s % TOKEN_TILE == 0 and TOKEN_TILE % MOBA_BLOCK == 0 and d_ff % FF_CHUNK == 0
    assert s % FFN_TILE == 0 and s % QKV_TILE == 0 and QKV_TILE % MOBA_BLOCK == 0
    assert w_in.shape[2] == 3 * ATTN_WIDTH + POOL_WIDTH + 2 * d
    alpha = (2.0 * depth) ** 0.25
    assert (K_PAD - HEAD_DIM - PEN_SPLIT * (s // MOBA_BLOCK)) % BF16_SUBLANES == 0
    assert K_PAD - HEAD_DIM - PEN_SPLIT * (s // MOBA_BLOCK) > 0
    scale = LOG2E / math.sqrt(HEAD_DIM)

    bias_own, bias_prev = _rel_bias_tables(rel_bias)
    h = x
    for l in range(depth):
        w = w_in[l]
        wq_t = (w[:, :ATTN_WIDTH] * scale).T.astype(BF16)
        wk = w[:, ATTN_WIDTH:2 * ATTN_WIDTH].astype(BF16)
        wv_t = w[:, 2 * ATTN_WIDTH:3 * ATTN_WIDTH].T.astype(BF16)
        qt, k, vt = _qkv_proj(h, wq_t, wk, wv_t)
        attn_t = _moba_attention(rel_bias, qt, k, vt, bias_own, bias_prev)
        h1 = _mixer(alpha, h, attn_t,
                    w[:, 3 * ATTN_WIDTH:3 * ATTN_WIDTH + POOL_WIDTH].astype(BF16),
                    w[:, 3 * ATTN_WIDTH + POOL_WIDTH:].astype(BF16),
                    w_pool_group[l].astype(BF16),
                    pool_scale[l].reshape(1, POOL_WIDTH),
                    w_branch_attn[l].astype(BF16),
                    w_branch_pool[l].astype(BF16),
                    w_out[l].astype(BF16),
                    ln1_g[l].reshape(1, d), ln1_b[l].reshape(1, d))
        h = _conv_ffn(alpha, h1, w_ffn_in[l].astype(BF16), conv_w[l], conv_b[l].reshape(1, d_ff),
                      w_ffn_out[l].astype(BF16), ln2_g[l].reshape(1, d), ln2_b[l].reshape(1, d))
    return h
```

```python
import functools
import math

import jax
import jax.numpy as jnp
from jax import lax
from jax.experimental import pallas as pl
from jax.experimental.pallas import tpu as pltpu

N_HEADS = 8
HEAD_DIM = 64
ATTN_WIDTH = N_HEADS * HEAD_DIM
MOBA_BLOCK = 256
MOBA_TOPK = 3
POOL_WINDOWS = (2, 4, 8, 16)
POOL_GROUP = 128
POOL_WIDTH = POOL_GROUP * len(POOL_WINDOWS)
N_BUCKETS = 32
MAX_DISTANCE = 128
CONV_WIDTH = 3
LN_EPS = 1e-5
NEG = -1e30
M_INIT = 0.5 * NEG
LOG2E = math.log2(math.e)
FILL_AHEAD = 2
PEN_SPLIT = 3

LANES = 128
K_PAD = LANES
BF16_SUBLANES = 16
MXU_TILE = 256
V_ONES = BF16_SUBLANES
SCORE_SLOTS = 3
TAIL_SPLIT = 4
POOL_HALO = max(POOL_WINDOWS)
CONV_HALO = 8
FF_CHUNK = MXU_TILE
TOKEN_TILE = 1024
QKV_TILE = 2048
FFN_TILE = 1024
VMEM_LIMIT = 56 * 1024 * 1024

F32 = jnp.float32
BF16 = jnp.bfloat16
NT_DIMS = (((1,), (1,)), ((), ()))
TN_DIMS = (((0,), (0,)), ((), ()))


def _params(n_grid):
    return pltpu.CompilerParams(dimension_semantics=("arbitrary",) * n_grid,
                                vmem_limit_bytes=VMEM_LIMIT)


def _const_spec(shape):
    nd = len(shape)
    return pl.BlockSpec(shape, lambda *_: (0,) * nd)


def _rel_bias_kernel(rb_ref, own_ref, prev_ref):
    h = pl.program_id(0)
    kk = lax.broadcasted_iota(jnp.int32, (MOBA_BLOCK, MOBA_BLOCK), 0)
    qq = lax.broadcasted_iota(jnp.int32, (MOBA_BLOCK, MOBA_BLOCK), 1)
    max_exact = N_BUCKETS // 2
    n_log = N_BUCKETS - max_exact
    first_n = [math.ceil(max_exact * (MAX_DISTANCE / max_exact) ** (k / n_log)) for k in range(n_log)]

    def table(dist):
        n = jnp.maximum(dist, 0)
        out = jnp.zeros(dist.shape, F32)
        for b in range(max_exact):
            out = jnp.where(n == b, rb_ref[h, b], out)
        for k in range(n_log):
            out = jnp.where(n >= first_n[k], rb_ref[h, max_exact + k], out)
        return out

    own_ref[...] = jnp.where(qq >= kk, table(qq - kk) * LOG2E, NEG)
    prev_ref[...] = table(qq - kk + MOBA_BLOCK) * LOG2E


def _rel_bias_tables(rel_bias):
    shape = jax.ShapeDtypeStruct((N_HEADS, MOBA_BLOCK, MOBA_BLOCK), F32)
    spec = pl.BlockSpec((None, MOBA_BLOCK, MOBA_BLOCK), lambda h: (h, 0, 0))
    return pl.pallas_call(
        _rel_bias_kernel,
        grid=(N_HEADS,),
        in_specs=[pl.BlockSpec(memory_space=pltpu.SMEM)],
        out_specs=[spec, spec],
        out_shape=[shape, shape],
        compiler_params=_params(1),
        name="rel_bias_tables",
    )(rel_bias)


def _qkv_kernel(x_ref, wq_ref, wk_ref, wv_ref, qt_ref, k_ref, vt_ref):
    xb = x_ref[...].astype(BF16)
    qt = lax.dot_general(wq_ref[...], xb, NT_DIMS, preferred_element_type=F32)
    vt = lax.dot_general(wv_ref[...], xb, NT_DIMS, preferred_element_type=F32)
    k_ref[...] = jnp.dot(xb, wk_ref[...], preferred_element_type=F32).astype(BF16)
    for c in range(qt_ref.shape[0]):
        cols = slice(c * MOBA_BLOCK, (c + 1) * MOBA_BLOCK)
        qt_ref[c] = qt[:, cols].astype(BF16)
        vt_ref[c] = vt[:, cols].astype(BF16)


def _qkv_proj(x, wq_t, wk, wv_t):
    b, s, d = x.shape
    nb = s // MOBA_BLOCK
    bpt = QKV_TILE // MOBA_BLOCK
    t_shape = jax.ShapeDtypeStruct((b, nb, ATTN_WIDTH, MOBA_BLOCK), BF16)
    t_spec = pl.BlockSpec((None, bpt, ATTN_WIDTH, MOBA_BLOCK), lambda i, t: (i, t, 0, 0))
    return pl.pallas_call(
        _qkv_kernel,
        grid=(b, s // QKV_TILE),
        in_specs=[pl.BlockSpec((None, QKV_TILE, d), lambda i, t: (i, t, 0)),
                  _const_spec(wq_t.shape), _const_spec(wk.shape), _const_spec(wv_t.shape)],
        out_specs=[t_spec,
                   pl.BlockSpec((None, QKV_TILE, ATTN_WIDTH), lambda i, t: (i, t, 0)),
                   t_spec],
        out_shape=[t_shape, jax.ShapeDtypeStruct((b, s, ATTN_WIDTH), BF16), t_shape],
        compiler_params=_params(2),
        name="qkv_proj",
    )(x, wq_t, wk, wv_t)


def _split_bf16(x):
    parts = []
    rem = x
    for _ in range(PEN_SPLIT):
        part = rem.astype(BF16)
        parts.append(part)
        rem = rem - part.astype(F32)
    return parts


def _select_blocks(gate, i):
    nb = gate.shape[0]
    bidx = lax.broadcasted_iota(jnp.int32, gate.shape, 0)
    avail = jnp.where(bidx < i, 1, 0)
    chosen = jnp.zeros(gate.shape, jnp.int32)
    for _ in range(MOBA_TOPK):
        live = avail == 1
        top = jnp.max(jnp.where(live, gate, -jnp.inf), axis=0, keepdims=True)
        cand = jnp.where(live, jnp.where(gate == top, bidx, nb), nb)
        pick = jnp.where(bidx == jnp.min(cand, axis=0, keepdims=True), 1, 0)
        chosen = chosen + pick
        avail = avail - pick
    return chosen


N0, N1 = SCORE_SLOTS, SCORE_SLOTS + 1


def _attn_kernel(rb_ref, qt_ref, qnext_ref, k_ref, vt_ref, bown_ref, bprev_ref, o_ref,
                 kmean_ref, kpad_ref, qa_ref, s_ref, m_ref, acc_ref):
    i = pl.program_id(1)
    nb = vt_ref.shape[0]
    blk = MOBA_BLOCK
    cur = i & 1
    nxt = 1 - cur
    bidx = lax.broadcasted_iota(jnp.int32, (nb, blk), 0)

    def put_scores(h, j, slot, half):
        start = pl.multiple_of(j * blk, blk)
        s_ref[slot, h] = jnp.dot(kpad_ref[pl.ds(start, blk), h * K_PAD:(h + 1) * K_PAD],
                                 qa_ref[half, h], preferred_element_type=F32)

    def score_operands(q_ref, tile):
        gates = []
        for h in range(N_HEADS):
            q = q_ref[h * HEAD_DIM:(h + 1) * HEAD_DIM, :]
            parts = _split_bf16(kmean_ref[:, h * HEAD_DIM:(h + 1) * HEAD_DIM])
            g3 = jnp.dot(jnp.concatenate(parts, axis=0), q, preferred_element_type=F32)
            gate = g3[0:nb]
            for r in range(1, PEN_SPLIT):
                gate = gate + g3[r * nb:(r + 1) * nb]
            gates.append(gate)
        operands = []
        for h in range(N_HEADS):
            chosen = _select_blocks(gates[h], tile)
            far_bias = jnp.where(bidx < tile - 1, rb_ref[h, N_BUCKETS - 1] * LOG2E, 0.0)
            pen = jnp.where(bidx == tile, 0.0, jnp.where(chosen == 1, far_bias, NEG))
            operands.append(jnp.concatenate(
                [q_ref[h * HEAD_DIM:(h + 1) * HEAD_DIM, :]] + _split_bf16(pen)
                + [jnp.zeros((K_PAD - HEAD_DIM - PEN_SPLIT * nb, blk), BF16)], axis=0))
        return operands

    def store_operands(operands, half):
        for h in range(N_HEADS):
            qa_ref[half, h] = operands[h]

    @pl.when(i == 0)
    def _():
        pad_lane = lax.broadcasted_iota(jnp.int32, (blk, K_PAD - HEAD_DIM), 1)
        for j in range(nb):
            rows = slice(j * blk, (j + 1) * blk)
            kj = k_ref[rows, :]
            kmean_ref[j:j + 1, :] = jnp.sum(kj.astype(F32), axis=0, keepdims=True) * (1.0 / blk)
            hot = pad_lane == j
            for rep in range(1, PEN_SPLIT):
                hot = hot | (pad_lane == j + rep * nb)
            one_hot = jnp.where(hot, 1.0, 0.0).astype(BF16)
            for h in range(N_HEADS):
                kpad_ref[rows, h * K_PAD:(h + 1) * K_PAD] = jnp.concatenate(
                    [kj[:, h * HEAD_DIM:(h + 1) * HEAD_DIM], one_hot], axis=1)
        store_operands(score_operands(qt_ref, 0), 0)
        for h in range(N_HEADS):
            put_scores(h, 0, N0, 0)

    m_ref[...] = jnp.full(m_ref.shape, M_INIT, F32)
    acc_ref[...] = jnp.zeros(acc_ref.shape, F32)

    ones = jnp.ones((V_ONES, blk), BF16)

    half = blk // 2

    def absorb(h, j, slot, kind):
        m_old = m_ref[h]
        if kind == "own":
            top = s_ref[slot, h, 0:half, :] + bown_ref[h, 0:half, :]
            bot = s_ref[slot, h, half:blk, half:blk] + bown_ref[h, half:blk, half:blk]
            m_top = jnp.max(top, axis=0, keepdims=True)
            m_blk = jnp.concatenate(
                [m_top[:, 0:half],
                 jnp.maximum(m_top[:, half:blk], jnp.max(bot, axis=0, keepdims=True))], axis=1)
            m_new = jnp.maximum(m_old, m_blk)
            p_bot = jnp.exp2(bot - m_new[:, half:blk]).astype(BF16)
            p = jnp.concatenate(
                [jnp.exp2(top - m_new).astype(BF16),
                 jnp.concatenate([jnp.zeros((half, half), BF16), p_bot], axis=1)], axis=0)
        else:
            s = s_ref[slot, h]
            if kind == "prev":
                s = s + bprev_ref[h]
            m_new = jnp.maximum(m_old, jnp.max(s, axis=0, keepdims=True))
            p = jnp.exp2(s - m_new).astype(BF16)
        va = jnp.concatenate([vt_ref[j, h * HEAD_DIM:(h + 1) * HEAD_DIM, :], ones], axis=0)
        acc_ref[h] = jnp.exp2(m_old - m_new) * acc_ref[h] + jnp.dot(va, p, preferred_element_type=F32)
        m_ref[h] = m_new

    def stage(j, slot, kind, fill=None):
        if fill is not None:
            for h in range(FILL_AHEAD):
                put_scores(h, *fill)
        for h in range(N_HEADS):
            if fill is not None and h + FILL_AHEAD < N_HEADS:
                put_scores(h + FILL_AHEAD, *fill)
            absorb(h, j, slot, kind)

    next0, next1 = (0, N0, nxt), (1, N1, nxt)

    @pl.when(i == 0)
    def _():
        ops_next = score_operands(qnext_ref, i + 1)
        stage(0, N0, "own")
        store_operands(ops_next, nxt)
        for h in range(N_HEADS):
            put_scores(h, *next0)
            put_scores(h, *next1)

    @pl.when(i == 1)
    def _():
        ops_next = score_operands(qnext_ref, i + 1)
        stage(0, N0, "prev")
        store_operands(ops_next, nxt)
        stage(1, N1, "own", fill=next0)
        for h in range(N_HEADS):
            put_scores(h, *next1)

    @pl.when(i == 2)
    def _():
        ops_next = score_operands(qnext_ref, i + 1)
        stage(0, N0, "far", fill=(2, 2, cur))
        store_operands(ops_next, nxt)
        stage(1, N1, "prev", fill=next0)
        stage(2, 2, "own", fill=next1)

    @pl.when(i >= 3)
    def _():
        ops_next = score_operands(qnext_ref, i + 1)
        stage(0, N0, "far", fill=(2, 2, cur))
        stage(1, N1, "far", fill=(3, 0, cur))
        store_operands(ops_next, nxt)

    def triple_body(a, carry):
        j0 = 2 + 3 * a
        stage(j0, 2, "far", fill=(j0 + 2, 1, cur))
        stage(j0 + 1, 0, "far", fill=(j0 + 3, 2, cur))
        stage(j0 + 2, 1, "far", fill=(j0 + 4, 0, cur))
        return carry

    n_triples = lax.div(jnp.maximum(i - 3, 0), 3)
    lax.fori_loop(0, n_triples, triple_body, 0)
    rest = jnp.where(i >= 3, i - 1 - 3 * n_triples, 0)

    @pl.when(rest == 2)
    def _():
        stage(i - 1, 2, "prev", fill=next0)
        stage(i, 0, "own", fill=next1)

    @pl.when(rest == 3)
    def _():
        stage(i - 2, 2, "far", fill=(i, 1, cur))
        stage(i - 1, 0, "prev", fill=next0)
        stage(i, 1, "own", fill=next1)

    @pl.when(rest == 4)
    def _():
        stage(i - 3, 2, "far", fill=(i - 1, 1, cur))
        stage(i - 2, 0, "far", fill=(i, 2, cur))
        stage(i - 1, 1, "prev", fill=next0)
        stage(i, 2, "own", fill=next1)

    for h in range(N_HEADS):
        acc = acc_ref[h]
        o_ref[h * HEAD_DIM:(h + 1) * HEAD_DIM, :] = (
            acc[0:HEAD_DIM] / acc[HEAD_DIM:HEAD_DIM + 1]).astype(o_ref.dtype)


def _moba_attention(rel_bias, qt, k, vt, bias_own, bias_prev):
    b, nb, _, blk = qt.shape
    s = nb * blk
    q_tile = pl.BlockSpec((None, None, ATTN_WIDTH, blk), lambda bi, i: (bi, i, 0, 0))
    return pl.pallas_call(
        _attn_kernel,
        grid=(b, nb),
        in_specs=[pl.BlockSpec(memory_space=pltpu.SMEM),
                  q_tile,
                  pl.BlockSpec((None, None, ATTN_WIDTH, blk),
                               lambda bi, i: (bi, jnp.minimum(i + 1, nb - 1), 0, 0)),
                  pl.BlockSpec((None, s, ATTN_WIDTH), lambda bi, i: (bi, 0, 0)),
                  pl.BlockSpec((None, nb, ATTN_WIDTH, blk), lambda bi, i: (bi, 0, 0, 0)),
                  _const_spec(bias_own.shape), _const_spec(bias_prev.shape)],
        out_specs=q_tile,
        out_shape=jax.ShapeDtypeStruct((b, nb, ATTN_WIDTH, blk), BF16),
        scratch_shapes=[pltpu.VMEM((nb, ATTN_WIDTH), F32),
                        pltpu.VMEM((s, N_HEADS * K_PAD), BF16),
                        pltpu.VMEM((2, N_HEADS, K_PAD, blk), BF16),
                        pltpu.VMEM((SCORE_SLOTS + 2, N_HEADS, blk, blk), F32),
                        pltpu.VMEM((N_HEADS, 1, blk), F32),
                        pltpu.VMEM((N_HEADS, HEAD_DIM + V_ONES, blk), F32)],
        compiler_params=_params(2),
        name="moba_attention",
    )(rel_bias, qt, qt, k, vt, bias_own, bias_prev)


def _layer_norm(z, g, b):
    mu = jnp.mean(z, axis=-1, keepdims=True)
    zc = z - mu
    var = jnp.mean(zc * zc, axis=-1, keepdims=True)
    return zc * lax.rsqrt(var + LN_EPS) * g + b


def _mixer_kernel(alpha, x_ref, at_ref, wp_ref, wg_ref, wpool_ref, pscale_ref, wba_ref, wbp_ref,
                  wout_ref, g_ref, b_ref, h_ref, pbuf_ref):
    t = pl.program_id(1)
    tile = x_ref.shape[0]
    d = x_ref.shape[1]

    @pl.when(t == 0)
    def _():
        pbuf_ref[0:POOL_HALO, :] = jnp.zeros((POOL_HALO, POOL_WIDTH), F32)

    x = x_ref[...]
    xb = x.astype(BF16)
    p = jnp.dot(xb, wp_ref[...], preferred_element_type=F32)
    gate_logits = jnp.dot(xb, wg_ref[...], preferred_element_type=F32)
    y_attn = jnp.concatenate(
        [lax.dot_general(at_ref[c], wba_ref[...], TN_DIMS, preferred_element_type=F32)
         for c in range(at_ref.shape[0])], axis=0)

    pbuf_ref[POOL_HALO:POOL_HALO + tile, :] = p
    pos = t * tile + lax.broadcasted_iota(jnp.int32, (tile, POOL_GROUP), 0)
    ys = []
    for gi, w in enumerate(POOL_WINDOWS):
        cols = slice(gi * POOL_GROUP, (gi + 1) * POOL_GROUP)
        pg = p[:, cols]
        win = pg
        for back in range(1, w):
            win = win + pbuf_ref[POOL_HALO - back:POOL_HALO - back + tile, cols]
        cnt = jnp.minimum(pos + 1, w).astype(F32)
        diff = win / cnt - pg
        ys.append(jnp.dot(diff.astype(BF16), wpool_ref[gi], preferred_element_type=F32))
    pbuf_ref[0:POOL_HALO, :] = pbuf_ref[tile:tile + POOL_HALO, :]
    y = jnp.concatenate(ys, axis=1) * pscale_ref[...]
    y_pool = jnp.dot(y.astype(BF16), wbp_ref[...], preferred_element_type=F32)

    gates = 1.0 / (1.0 + jnp.exp(-gate_logits))
    mixed = (gates[:, :d] * y_attn + gates[:, d:] * y_pool).astype(BF16)
    for rows in [slice(r, r + tile // TAIL_SPLIT) for r in range(0, tile, tile // TAIL_SPLIT)]:
        z = alpha * x[rows] + jnp.dot(mixed[rows], wout_ref[...], preferred_element_type=F32)
        h_ref[rows, :] = _layer_norm(z, g_ref[...], b_ref[...])


def _mixer(alpha, x, attn_t, wp, wg, wpool, pscale, wba, wbp, wout, ln_g, ln_b):
    b, s, d = x.shape
    bpt = TOKEN_TILE // MOBA_BLOCK
    consts = (wp, wg, wpool, pscale, wba, wbp, wout, ln_g, ln_b)
    return pl.pallas_call(
        functools.partial(_mixer_kernel, alpha),
        grid=(b, s // TOKEN_TILE),
        in_specs=[pl.BlockSpec((None, TOKEN_TILE, d), lambda i, t: (i, t, 0)),
                  pl.BlockSpec((None, bpt, ATTN_WIDTH, MOBA_BLOCK), lambda i, t: (i, t, 0, 0))]
                 + [_const_spec(c.shape) for c in consts],
        out_specs=pl.BlockSpec((None, TOKEN_TILE, d), lambda i, t: (i, t, 0)),
        out_shape=jax.ShapeDtypeStruct((b, s, d), F32),
        scratch_shapes=[pltpu.VMEM((POOL_HALO + TOKEN_TILE, POOL_WIDTH), F32)],
        compiler_params=_params(2),
        name="mixer",
    )(x, attn_t, *consts)


def _ffn_kernel(alpha, h_ref, win_ref, cw_ref, cb_ref, wo_ref, g_ref, b_ref, o_ref, carry_ref, act_ref):
    tile = h_ref.shape[0]
    d_ff = wo_ref.shape[0]
    n_chunks = d_ff // FF_CHUNK

    @pl.when(pl.program_id(1) == 0)
    def _():
        carry_ref[...] = jnp.zeros(carry_ref.shape, F32)

    h = h_ref[...]
    hb = h.astype(BF16)
    row = lax.broadcasted_iota(jnp.int32, (CONV_HALO, FF_CHUNK), 0)

    def cols(c, base=0):
        return slice(base + c * FF_CHUNK, base + (c + 1) * FF_CHUNK)

    def project(c):
        return (jnp.dot(hb, win_ref[:, cols(c)], preferred_element_type=F32),
                jnp.dot(hb, win_ref[:, cols(c, d_ff)], preferred_element_type=F32))

    def shifted(a, prev, back):
        rolled = pltpu.roll(a, back, 0)
        top = rolled[0:CONV_HALO]
        for r in range(back):
            top = jnp.where(row == r, prev[CONV_HALO - back + r:CONV_HALO - back + r + 1, :], top)
        return jnp.concatenate([top, rolled[CONV_HALO:]], axis=0)

    nxt = project(0)
    for c in range(n_chunks):
        a, u = nxt
        if c + 1 < n_chunks:
            nxt = project(c + 1)
        prev = carry_ref[c]
        carry_ref[c] = a[tile - CONV_HALO:tile, :]
        cw = cw_ref[:, cols(c)]
        y = cb_ref[:, cols(c)]
        for tap in range(CONV_WIDTH):
            back = CONV_WIDTH - 1 - tap
            src = a if back == 0 else shifted(a, prev, back)
            y = y + src * cw[tap:tap + 1, :]
        act = 0.5 * y * (1.0 + lax.erf(y * math.sqrt(0.5))) * u
        act_ref[:, cols(c)] = act.astype(BF16)
    split = (n_chunks - 1) * FF_CHUNK
    for rows in [slice(r, r + tile // TAIL_SPLIT) for r in range(0, tile, tile // TAIL_SPLIT)]:
        f = (jnp.dot(act_ref[rows, :split], wo_ref[:split, :], preferred_element_type=F32)
             + jnp.dot(act_ref[rows, split:], wo_ref[split:, :], preferred_element_type=F32))
        o_ref[rows, :] = _layer_norm(alpha * h[rows] + f, g_ref[...], b_ref[...])


def _conv_ffn(alpha, h, win, cw, cb, wo, ln_g, ln_b):
    b, s, d = h.shape
    d_ff = wo.shape[0]
    consts = (win, cw, cb, wo, ln_g, ln_b)
    return pl.pallas_call(
        functools.partial(_ffn_kernel, alpha),
        grid=(b, s // FFN_TILE),
        in_specs=[pl.BlockSpec((None, FFN_TILE, d), lambda i, t: (i, t, 0))]
                 + [_const_spec(c.shape) for c in consts],
        out_specs=pl.BlockSpec((None, FFN_TILE, d), lambda i, t: (i, t, 0)),
        out_shape=jax.ShapeDtypeStruct((b, s, d), F32),
        scratch_shapes=[pltpu.VMEM((d_ff // FF_CHUNK, CONV_HALO, FF_CHUNK), F32),
                        pltpu.VMEM((FFN_TILE, d_ff), BF16)],
        compiler_params=_params(2),
        name="conv_ffn",
    )(h, *consts)


def kernel(x, w_in, rel_bias, w_pool_group, pool_scale, w_branch_attn, w_branch_pool, w_out, ln1_g,
           ln1_b, w_ffn_in, conv_w, conv_b, w_ffn_out, ln2_g, ln2_b):
    b, s, d = x.shape
    depth = w_in.shape[0]
    d_ff = w_ffn_out.shape[1]
    assert s % TOKEN_TILE == 0 and TOKEN_TILE % MOBA_BLOCK == 0 and d_ff % FF_CHUNK == 0
    assert s % FFN_TILE == 0 and s % QKV_TILE == 0 and QKV_TILE % MOBA_BLOCK == 0
    assert w_in.shape[2] == 3 * ATTN_WIDTH + POOL_WIDTH + 2 * d
    alpha = (2.0 * depth) ** 0.25
    assert (K_PAD - HEAD_DIM - PEN_SPLIT * (s // MOBA_BLOCK)) % BF16_SUBLANES == 0
    assert K_PAD - HEAD_DIM - PEN_SPLIT * (s // MOBA_BLOCK) > 0
    scale = LOG2E / math.sqrt(HEAD_DIM)

    bias_own, bias_prev = _rel_bias_tables(rel_bias)
    h = x
    for l in range(depth):
        w = w_in[l]
        wq_t = (w[:, :ATTN_WIDTH] * scale).T.astype(BF16)
        wk = w[:, ATTN_WIDTH:2 * ATTN_WIDTH].astype(BF16)
        wv_t = w[:, 2 * ATTN_WIDTH:3 * ATTN_WIDTH].T.astype(BF16)
        qt, k, vt = _qkv_proj(h, wq_t, wk, wv_t)
        attn_t = _moba_attention(rel_bias, qt, k, vt, bias_own, bias_prev)
        h1 = _mixer(alpha, h, attn_t,
                    w[:, 3 * ATTN_WIDTH:3 * ATTN_WIDTH + POOL_WIDTH].astype(BF16),
                    w[:, 3 * ATTN_WIDTH + POOL_WIDTH:].astype(BF16),
                    w_pool_group[l].astype(BF16),
                    pool_scale[l].reshape(1, POOL_WIDTH),
                    w_branch_attn[l].astype(BF16),
                    w_branch_pool[l].astype(BF16),
                    w_out[l].astype(BF16),
                    ln1_g[l].reshape(1, d), ln1_b[l].reshape(1, d))
        h = _conv_ffn(alpha, h1, w_ffn_in[l].astype(BF16), conv_w[l], conv_b[l].reshape(1, d_ff),
                      w_ffn_out[l].astype(BF16), ln2_g[l].reshape(1, d), ln2_b[l].reshape(1, d))
    return h
```

```python
import functools
import math

import jax
import jax.numpy as jnp
from jax import lax
from jax.experimental import pallas as pl
from jax.experimental.pallas import tpu as pltpu

N_HEADS = 8
HEAD_DIM = 64
ATTN_WIDTH = N_HEADS * HEAD_DIM
MOBA_BLOCK = 256
MOBA_TOPK = 3
POOL_WINDOWS = (2, 4, 8, 16)
POOL_GROUP = 128
POOL_WIDTH = POOL_GROUP * len(POOL_WINDOWS)
N_BUCKETS = 32
MAX_DISTANCE = 128
CONV_WIDTH = 3
LN_EPS = 1e-5
NEG = -1e30
M_INIT = 0.5 * NEG
LOG2E = math.log2(math.e)
FILL_AHEAD = 2
PEN_SPLIT = 3

LANES = 128
K_PAD = LANES
BF16_SUBLANES = 16
MXU_TILE = 256
V_ONES = BF16_SUBLANES
SCORE_SLOTS = 3
TAIL_SPLIT = 4
POOL_HALO = max(POOL_WINDOWS)
CONV_HALO = 8
FF_CHUNK = MXU_TILE
TOKEN_TILE = 1024
QKV_TILE = 1024
FFN_TILE = 1024
VMEM_LIMIT = 56 * 1024 * 1024
FUSED_TILE = 1024
FUSED_VMEM_LIMIT = 60 * 1024 * 1024

F32 = jnp.float32
BF16 = jnp.bfloat16
NT_DIMS = (((1,), (1,)), ((), ()))
TN_DIMS = (((0,), (0,)), ((), ()))


def _params(n_grid):
    return pltpu.CompilerParams(dimension_semantics=("arbitrary",) * n_grid,
                                vmem_limit_bytes=VMEM_LIMIT)


def _const_spec(shape):
    nd = len(shape)
    return pl.BlockSpec(shape, lambda *_: (0,) * nd)


def _rel_bias_kernel(rb_ref, own_ref, prev_ref):
    h = pl.program_id(0)
    kk = lax.broadcasted_iota(jnp.int32, (MOBA_BLOCK, MOBA_BLOCK), 0)
    qq = lax.broadcasted_iota(jnp.int32, (MOBA_BLOCK, MOBA_BLOCK), 1)
    max_exact = N_BUCKETS // 2
    n_log = N_BUCKETS - max_exact
    first_n = [math.ceil(max_exact * (MAX_DISTANCE / max_exact) ** (k / n_log)) for k in range(n_log)]

    def table(dist):
        n = jnp.maximum(dist, 0)
        out = jnp.zeros(dist.shape, F32)
        for b in range(max_exact):
            out = jnp.where(n == b, rb_ref[h, b], out)
        for k in range(n_log):
            out = jnp.where(n >= first_n[k], rb_ref[h, max_exact + k], out)
        return out

    own_ref[...] = jnp.where(qq >= kk, table(qq - kk) * LOG2E, NEG)
    prev_ref[...] = table(qq - kk + MOBA_BLOCK) * LOG2E


def _rel_bias_tables(rel_bias):
    shape = jax.ShapeDtypeStruct((N_HEADS, MOBA_BLOCK, MOBA_BLOCK), F32)
    spec = pl.BlockSpec((None, MOBA_BLOCK, MOBA_BLOCK), lambda h: (h, 0, 0))
    return pl.pallas_call(
        _rel_bias_kernel,
        grid=(N_HEADS,),
        in_specs=[pl.BlockSpec(memory_space=pltpu.SMEM)],
        out_specs=[spec, spec],
        out_shape=[shape, shape],
        compiler_params=_params(1),
        name="rel_bias_tables",
    )(rel_bias)


def _qkv_kernel(x_ref, wq_ref, wk_ref, wv_ref, qt_ref, k_ref, vt_ref):
    xb = x_ref[...].astype(BF16)
    qt = lax.dot_general(wq_ref[...], xb, NT_DIMS, preferred_element_type=F32)
    vt = lax.dot_general(wv_ref[...], xb, NT_DIMS, preferred_element_type=F32)
    k_ref[...] = jnp.dot(xb, wk_ref[...], preferred_element_type=F32).astype(BF16)
    for c in range(qt_ref.shape[0]):
        cols = slice(c * MOBA_BLOCK, (c + 1) * MOBA_BLOCK)
        qt_ref[c] = qt[:, cols].astype(BF16)
        vt_ref[c] = vt[:, cols].astype(BF16)


def _qkv_proj(x, wq_t, wk, wv_t):
    b, s, d = x.shape
    nb = s // MOBA_BLOCK
    bpt = QKV_TILE // MOBA_BLOCK
    t_shape = jax.ShapeDtypeStruct((b, nb, ATTN_WIDTH, MOBA_BLOCK), BF16)
    t_spec = pl.BlockSpec((None, bpt, ATTN_WIDTH, MOBA_BLOCK), lambda i, t: (i, t, 0, 0))
    return pl.pallas_call(
        _qkv_kernel,
        grid=(b, s // QKV_TILE),
        in_specs=[pl.BlockSpec((None, QKV_TILE, d), lambda i, t: (i, t, 0)),
                  _const_spec(wq_t.shape), _const_spec(wk.shape), _const_spec(wv_t.shape)],
        out_specs=[t_spec,
                   pl.BlockSpec((None, QKV_TILE, ATTN_WIDTH), lambda i, t: (i, t, 0)),
                   t_spec],
        out_shape=[t_shape, jax.ShapeDtypeStruct((b, s, ATTN_WIDTH), BF16), t_shape],
        compiler_params=_params(2),
        name="qkv_proj",
    )(x, wq_t, wk, wv_t)


def _split_bf16(x):
    parts = []
    rem = x
    for _ in range(PEN_SPLIT):
        part = rem.astype(BF16)
        parts.append(part)
        rem = rem - part.astype(F32)
    return parts


def _select_blocks(gate, i):
    nb = gate.shape[0]
    bidx = lax.broadcasted_iota(jnp.int32, gate.shape, 0)
    avail = jnp.where(bidx < i, 1, 0)
    chosen = jnp.zeros(gate.shape, jnp.int32)
    for _ in range(MOBA_TOPK):
        live = avail == 1
        top = jnp.max(jnp.where(live, gate, -jnp.inf), axis=0, keepdims=True)
        cand = jnp.where(live, jnp.where(gate == top, bidx, nb), nb)
        pick = jnp.where(bidx == jnp.min(cand, axis=0, keepdims=True), 1, 0)
        chosen = chosen + pick
        avail = avail - pick
    return chosen


N0, N1 = SCORE_SLOTS, SCORE_SLOTS + 1


def _attn_kernel(rb_ref, qt_ref, qnext_ref, k_ref, vt_ref, bown_ref, bprev_ref, o_ref,
                 kmean_ref, kpad_ref, qa_ref, s_ref, m_ref, acc_ref):
    i = pl.program_id(1)
    nb = vt_ref.shape[0]
    blk = MOBA_BLOCK
    cur = i & 1
    nxt = 1 - cur
    bidx = lax.broadcasted_iota(jnp.int32, (nb, blk), 0)

    def put_scores(h, j, slot, half):
        start = pl.multiple_of(j * blk, blk)
        s_ref[slot, h] = jnp.dot(kpad_ref[pl.ds(start, blk), h * K_PAD:(h + 1) * K_PAD],
                                 qa_ref[half, h], preferred_element_type=F32)

    def score_operands(q_ref, tile):
        gates = []
        for h in range(N_HEADS):
            q = q_ref[h * HEAD_DIM:(h + 1) * HEAD_DIM, :]
            parts = _split_bf16(kmean_ref[:, h * HEAD_DIM:(h + 1) * HEAD_DIM])
            g3 = jnp.dot(jnp.concatenate(parts, axis=0), q, preferred_element_type=F32)
            gate = g3[0:nb]
            for r in range(1, PEN_SPLIT):
                gate = gate + g3[r * nb:(r + 1) * nb]
            gates.append(gate)
        operands = []
        for h in range(N_HEADS):
            chosen = _select_blocks(gates[h], tile)
            far_bias = jnp.where(bidx < tile - 1, rb_ref[h, N_BUCKETS - 1] * LOG2E, 0.0)
            pen = jnp.where(bidx == tile, 0.0, jnp.where(chosen == 1, far_bias, NEG))
            operands.append(jnp.concatenate(
                [q_ref[h * HEAD_DIM:(h + 1) * HEAD_DIM, :]] + _split_bf16(pen)
                + [jnp.zeros((K_PAD - HEAD_DIM - PEN_SPLIT * nb, blk), BF16)], axis=0))
        return operands

    def store_operands(operands, half):
        for h in range(N_HEADS):
            qa_ref[half, h] = operands[h]

    @pl.when(i == 0)
    def _():
        pad_lane = lax.broadcasted_iota(jnp.int32, (blk, K_PAD - HEAD_DIM), 1)
        for j in range(nb):
            rows = slice(j * blk, (j + 1) * blk)
            kj = k_ref[rows, :]
            kmean_ref[j:j + 1, :] = jnp.sum(kj.astype(F32), axis=0, keepdims=True) * (1.0 / blk)
            hot = pad_lane == j
            for rep in range(1, PEN_SPLIT):
                hot = hot | (pad_lane == j + rep * nb)
            one_hot = jnp.where(hot, 1.0, 0.0).astype(BF16)
            for h in range(N_HEADS):
                kpad_ref[rows, h * K_PAD:(h + 1) * K_PAD] = jnp.concatenate(
                    [kj[:, h * HEAD_DIM:(h + 1) * HEAD_DIM], one_hot], axis=1)
        store_operands(score_operands(qt_ref, 0), 0)
        for h in range(N_HEADS):
            put_scores(h, 0, N0, 0)

    m_ref[...] = jnp.full(m_ref.shape, M_INIT, F32)
    acc_ref[...] = jnp.zeros(acc_ref.shape, F32)

    ones = jnp.ones((V_ONES, blk), BF16)

    half = blk // 2

    def absorb(h, j, slot, kind):
        m_old = m_ref[h]
        if kind == "own":
            top = s_ref[slot, h, 0:half, :] + bown_ref[h, 0:half, :]
            bot = s_ref[slot, h, half:blk, half:blk] + bown_ref[h, half:blk, half:blk]
            m_top = jnp.max(top, axis=0, keepdims=True)
            m_blk = jnp.concatenate(
                [m_top[:, 0:half],
                 jnp.maximum(m_top[:, half:blk], jnp.max(bot, axis=0, keepdims=True))], axis=1)
            m_new = jnp.maximum(m_old, m_blk)
            p_bot = jnp.exp2(bot - m_new[:, half:blk]).astype(BF16)
            p = jnp.concatenate(
                [jnp.exp2(top - m_new).astype(BF16),
                 jnp.concatenate([jnp.zeros((half, half), BF16), p_bot], axis=1)], axis=0)
        else:
            s = s_ref[slot, h]
            if kind == "prev":
                s = s + bprev_ref[h]
            m_new = jnp.maximum(m_old, jnp.max(s, axis=0, keepdims=True))
            p = jnp.exp2(s - m_new).astype(BF16)
        va = jnp.concatenate([vt_ref[j, h * HEAD_DIM:(h + 1) * HEAD_DIM, :], ones], axis=0)
        acc_ref[h] = jnp.exp2(m_old - m_new) * acc_ref[h] + jnp.dot(va, p, preferred_element_type=F32)
        m_ref[h] = m_new

    def stage(j, slot, kind, fill=None):
        if fill is not None:
            for h in range(FILL_AHEAD):
                put_scores(h, *fill)
        for h in range(N_HEADS):
            if fill is not None and h + FILL_AHEAD < N_HEADS:
                put_scores(h + FILL_AHEAD, *fill)
            absorb(h, j, slot, kind)

    next0, next1 = (0, N0, nxt), (1, N1, nxt)

    @pl.when(i == 0)
    def _():
        ops_next = score_operands(qnext_ref, i + 1)
        stage(0, N0, "own")
        store_operands(ops_next, nxt)
        for h in range(N_HEADS):
            put_scores(h, *next0)
            put_scores(h, *next1)

    @pl.when(i == 1)
    def _():
        ops_next = score_operands(qnext_ref, i + 1)
        stage(0, N0, "prev")
        store_operands(ops_next, nxt)
        stage(1, N1, "own", fill=next0)
        for h in range(N_HEADS):
            put_scores(h, *next1)

    @pl.when(i == 2)
    def _():
        ops_next = score_operands(qnext_ref, i + 1)
        stage(0, N0, "far", fill=(2, 2, cur))
        store_operands(ops_next, nxt)
        stage(1, N1, "prev", fill=next0)
        stage(2, 2, "own", fill=next1)

    @pl.when(i >= 3)
    def _():
        ops_next = score_operands(qnext_ref, i + 1)
        stage(0, N0, "far", fill=(2, 2, cur))
        stage(1, N1, "far", fill=(3, 0, cur))
        store_operands(ops_next, nxt)

    def triple_body(a, carry):
        j0 = 2 + 3 * a
        stage(j0, 2, "far", fill=(j0 + 2, 1, cur))
        stage(j0 + 1, 0, "far", fill=(j0 + 3, 2, cur))
        stage(j0 + 2, 1, "far", fill=(j0 + 4, 0, cur))
        return carry

    n_triples = lax.div(jnp.maximum(i - 3, 0), 3)
    lax.fori_loop(0, n_triples, triple_body, 0)
    rest = jnp.where(i >= 3, i - 1 - 3 * n_triples, 0)

    @pl.when(rest == 2)
    def _():
        stage(i - 1, 2, "prev", fill=next0)
        stage(i, 0, "own", fill=next1)

    @pl.when(rest == 3)
    def _():
        stage(i - 2, 2, "far", fill=(i, 1, cur))
        stage(i - 1, 0, "prev", fill=next0)
        stage(i, 1, "own", fill=next1)

    @pl.when(rest == 4)
    def _():
        stage(i - 3, 2, "far", fill=(i - 1, 1, cur))
        stage(i - 2, 0, "far", fill=(i, 2, cur))
        stage(i - 1, 1, "prev", fill=next0)
        stage(i, 2, "own", fill=next1)

    for h in range(N_HEADS):
        acc = acc_ref[h]
        o_ref[h * HEAD_DIM:(h + 1) * HEAD_DIM, :] = (
            acc[0:HEAD_DIM] / acc[HEAD_DIM:HEAD_DIM + 1]).astype(o_ref.dtype)


def _moba_attention(rel_bias, qt, k, vt, bias_own, bias_prev):
    b, nb, _, blk = qt.shape
    s = nb * blk
    q_tile = pl.BlockSpec((None, None, ATTN_WIDTH, blk), lambda bi, i: (bi, i, 0, 0))
    return pl.pallas_call(
        _attn_kernel,
        grid=(b, nb),
        in_specs=[pl.BlockSpec(memory_space=pltpu.SMEM),
                  q_tile,
                  pl.BlockSpec((None, None, ATTN_WIDTH, blk),
                               lambda bi, i: (bi, jnp.minimum(i + 1, nb - 1), 0, 0)),
                  pl.BlockSpec((None, s, ATTN_WIDTH), lambda bi, i: (bi, 0, 0)),
                  pl.BlockSpec((None, nb, ATTN_WIDTH, blk), lambda bi, i: (bi, 0, 0, 0)),
                  _const_spec(bias_own.shape), _const_spec(bias_prev.shape)],
        out_specs=q_tile,
        out_shape=jax.ShapeDtypeStruct((b, nb, ATTN_WIDTH, blk), BF16),
        scratch_shapes=[pltpu.VMEM((nb, ATTN_WIDTH), F32),
                        pltpu.VMEM((s, N_HEADS * K_PAD), BF16),
                        pltpu.VMEM((2, N_HEADS, K_PAD, blk), BF16),
                        pltpu.VMEM((SCORE_SLOTS + 2, N_HEADS, blk, blk), F32),
                        pltpu.VMEM((N_HEADS, 1, blk), F32),
                        pltpu.VMEM((N_HEADS, HEAD_DIM + V_ONES, blk), F32)],
        compiler_params=_params(2),
        name="moba_attention",
    )(rel_bias, qt, qt, k, vt, bias_own, bias_prev)


def _layer_norm(z, g, b):
    mu = jnp.mean(z, axis=-1, keepdims=True)
    zc = z - mu
    var = jnp.mean(zc * zc, axis=-1, keepdims=True)
    return zc * lax.rsqrt(var + LN_EPS) * g + b


def _mixer_kernel(alpha, x_ref, at_ref, wp_ref, wg_ref, wpool_ref, pscale_ref, wba_ref, wbp_ref,
                  wout_ref, g_ref, b_ref, h_ref, pbuf_ref):
    t = pl.program_id(1)
    tile = x_ref.shape[0]
    d = x_ref.shape[1]

    @pl.when(t == 0)
    def _():
        pbuf_ref[0:POOL_HALO, :] = jnp.zeros((POOL_HALO, POOL_WIDTH), F32)

    x = x_ref[...]
    xb = x.astype(BF16)
    p = jnp.dot(xb, wp_ref[...], preferred_element_type=F32)
    gate_logits = jnp.dot(xb, wg_ref[...], preferred_element_type=F32)
    y_attn = jnp.concatenate(
        [lax.dot_general(at_ref[c], wba_ref[...], TN_DIMS, preferred_element_type=F32)
         for c in range(at_ref.shape[0])], axis=0)

    pbuf_ref[POOL_HALO:POOL_HALO + tile, :] = p
    pos = t * tile + lax.broadcasted_iota(jnp.int32, (tile, POOL_GROUP), 0)
    ys = []
    for gi, w in enumerate(POOL_WINDOWS):
        cols = slice(gi * POOL_GROUP, (gi + 1) * POOL_GROUP)
        pg = p[:, cols]
        win = pg
        for back in range(1, w):
            win = win + pbuf_ref[POOL_HALO - back:POOL_HALO - back + tile, cols]
        cnt = jnp.minimum(pos + 1, w).astype(F32)
        diff = win / cnt - pg
        ys.append(jnp.dot(diff.astype(BF16), wpool_ref[gi], preferred_element_type=F32))
    pbuf_ref[0:POOL_HALO, :] = pbuf_ref[tile:tile + POOL_HALO, :]
    y = jnp.concatenate(ys, axis=1) * pscale_ref[...]
    y_pool = jnp.dot(y.astype(BF16), wbp_ref[...], preferred_element_type=F32)

    gates = 1.0 / (1.0 + jnp.exp(-gate_logits))
    mixed = (gates[:, :d] * y_attn + gates[:, d:] * y_pool).astype(BF16)
    for rows in [slice(r, r + tile // TAIL_SPLIT) for r in range(0, tile, tile // TAIL_SPLIT)]:
        z = alpha * x[rows] + jnp.dot(mixed[rows], wout_ref[...], preferred_element_type=F32)
        h_ref[rows, :] = _layer_norm(z, g_ref[...], b_ref[...])


def _mixer(alpha, x, attn_t, wp, wg, wpool, pscale, wba, wbp, wout, ln_g, ln_b):
    b, s, d = x.shape
    bpt = TOKEN_TILE // MOBA_BLOCK
    consts = (wp, wg, wpool, pscale, wba, wbp, wout, ln_g, ln_b)
    return pl.pallas_call(
        functools.partial(_mixer_kernel, alpha),
        grid=(b, s // TOKEN_TILE),
        in_specs=[pl.BlockSpec((None, TOKEN_TILE, d), lambda i, t: (i, t, 0)),
                  pl.BlockSpec((None, bpt, ATTN_WIDTH, MOBA_BLOCK), lambda i, t: (i, t, 0, 0))]
                 + [_const_spec(c.shape) for c in consts],
        out_specs=pl.BlockSpec((None, TOKEN_TILE, d), lambda i, t: (i, t, 0)),
        out_shape=jax.ShapeDtypeStruct((b, s, d), F32),
        scratch_shapes=[pltpu.VMEM((POOL_HALO + TOKEN_TILE, POOL_WIDTH), F32)],
        compiler_params=_params(2),
        name="mixer",
    )(x, attn_t, *consts)


def _ffn_kernel(alpha, h_ref, win_ref, cw_ref, cb_ref, wo_ref, g_ref, b_ref, o_ref, carry_ref, act_ref):
    tile = h_ref.shape[0]
    d_ff = wo_ref.shape[0]
    n_chunks = d_ff // FF_CHUNK

    @pl.when(pl.program_id(1) == 0)
    def _():
        carry_ref[...] = jnp.zeros(carry_ref.shape, F32)

    h = h_ref[...]
    hb = h.astype(BF16)
    row = lax.broadcasted_iota(jnp.int32, (CONV_HALO, FF_CHUNK), 0)

    def cols(c, base=0):
        return slice(base + c * FF_CHUNK, base + (c + 1) * FF_CHUNK)

    def project(c):
        return (jnp.dot(hb, win_ref[:, cols(c)], preferred_element_type=F32),
                jnp.dot(hb, win_ref[:, cols(c, d_ff)], preferred_element_type=F32))

    def shifted(a, prev, back):
        rolled = pltpu.roll(a, back, 0)
        top = rolled[0:CONV_HALO]
        for r in range(back):
            top = jnp.where(row == r, prev[CONV_HALO - back + r:CONV_HALO - back + r + 1, :], top)
        return jnp.concatenate([top, rolled[CONV_HALO:]], axis=0)

    nxt = project(0)
    for c in range(n_chunks):
        a, u = nxt
        if c + 1 < n_chunks:
            nxt = project(c + 1)
        prev = carry_ref[c]
        carry_ref[c] = a[tile - CONV_HALO:tile, :]
        cw = cw_ref[:, cols(c)]
        y = cb_ref[:, cols(c)]
        for tap in range(CONV_WIDTH):
            back = CONV_WIDTH - 1 - tap
            src = a if back == 0 else shifted(a, prev, back)
            y = y + src * cw[tap:tap + 1, :]
        act = 0.5 * y * (1.0 + lax.erf(y * math.sqrt(0.5))) * u
        act_ref[:, cols(c)] = act.astype(BF16)
    split = (n_chunks - 1) * FF_CHUNK
    for rows in [slice(r, r + tile // TAIL_SPLIT) for r in range(0, tile, tile // TAIL_SPLIT)]:
        f = (jnp.dot(act_ref[rows, :split], wo_ref[:split, :], preferred_element_type=F32)
             + jnp.dot(act_ref[rows, split:], wo_ref[split:, :], preferred_element_type=F32))
        o_ref[rows, :] = _layer_norm(alpha * h[rows] + f, g_ref[...], b_ref[...])


def _conv_ffn(alpha, h, win, cw, cb, wo, ln_g, ln_b):
    b, s, d = h.shape
    d_ff = wo.shape[0]
    consts = (win, cw, cb, wo, ln_g, ln_b)
    return pl.pallas_call(
        functools.partial(_ffn_kernel, alpha),
        grid=(b, s // FFN_TILE),
        in_specs=[pl.BlockSpec((None, FFN_TILE, d), lambda i, t: (i, t, 0))]
                 + [_const_spec(c.shape) for c in consts],
        out_specs=pl.BlockSpec((None, FFN_TILE, d), lambda i, t: (i, t, 0)),
        out_shape=jax.ShapeDtypeStruct((b, s, d), F32),
        scratch_shapes=[pltpu.VMEM((d_ff // FF_CHUNK, CONV_HALO, FF_CHUNK), F32),
                        pltpu.VMEM((FFN_TILE, d_ff), BF16)],
        compiler_params=_params(2),
        name="conv_ffn",
    )(h, *consts)


N_MIXER_CONSTS = 9
N_FFN_CONSTS = 6


def _mixer_ffn_kernel(alpha, x_ref, at_ref, *refs):
    mixer_consts = refs[:N_MIXER_CONSTS]
    ffn_consts = refs[N_MIXER_CONSTS:N_MIXER_CONSTS + N_FFN_CONSTS]
    o_ref, h1_ref, pbuf_ref, carry_ref, act_ref = refs[N_MIXER_CONSTS + N_FFN_CONSTS:]
    _mixer_kernel(alpha, x_ref, at_ref, *mixer_consts, h1_ref, pbuf_ref)
    _ffn_kernel(alpha, h1_ref, *ffn_consts, o_ref, carry_ref, act_ref)


def _mixer_ffn(alpha, x, attn_t, mixer_consts, ffn_consts):
    b, s, d = x.shape
    d_ff = ffn_consts[3].shape[0]
    bpt = FUSED_TILE // MOBA_BLOCK
    consts = tuple(mixer_consts) + tuple(ffn_consts)
    assert len(mixer_consts) == N_MIXER_CONSTS and len(ffn_consts) == N_FFN_CONSTS
    return pl.pallas_call(
        functools.partial(_mixer_ffn_kernel, alpha),
        grid=(b, s // FUSED_TILE),
        in_specs=[pl.BlockSpec((None, FUSED_TILE, d), lambda i, t: (i, t, 0)),
                  pl.BlockSpec((None, bpt, ATTN_WIDTH, MOBA_BLOCK), lambda i, t: (i, t, 0, 0))]
                 + [_const_spec(c.shape) for c in consts],
        out_specs=pl.BlockSpec((None, FUSED_TILE, d), lambda i, t: (i, t, 0)),
        out_shape=jax.ShapeDtypeStruct((b, s, d), F32),
        scratch_shapes=[pltpu.VMEM((FUSED_TILE, d), F32),
                        pltpu.VMEM((POOL_HALO + FUSED_TILE, POOL_WIDTH), F32),
                        pltpu.VMEM((d_ff // FF_CHUNK, CONV_HALO, FF_CHUNK), F32),
                        pltpu.VMEM((FUSED_TILE, d_ff), BF16)],
        compiler_params=pltpu.CompilerParams(dimension_semantics=("arbitrary",) * 2,
                                             vmem_limit_bytes=FUSED_VMEM_LIMIT),
        name="mixer_ffn",
    )(x, attn_t, *consts)


def kernel(x, w_in, rel_bias, w_pool_group, pool_scale, w_branch_attn, w_branch_pool, w_out, ln1_g,
           ln1_b, w_ffn_in, conv_w, conv_b, w_ffn_out, ln2_g, ln2_b):
    b, s, d = x.shape
    depth = w_in.shape[0]
    d_ff = w_ffn_out.shape[1]
    assert s % TOKEN_TILE == 0 and TOKEN_TILE % MOBA_BLOCK == 0 and d_ff % FF_CHUNK == 0
    assert s % FFN_TILE == 0 and s % QKV_TILE == 0 and QKV_TILE % MOBA_BLOCK == 0
    assert w_in.shape[2] == 3 * ATTN_WIDTH + POOL_WIDTH + 2 * d
    alpha = (2.0 * depth) ** 0.25
    assert (K_PAD - HEAD_DIM - PEN_SPLIT * (s // MOBA_BLOCK)) % BF16_SUBLANES == 0
    assert K_PAD - HEAD_DIM - PEN_SPLIT * (s // MOBA_BLOCK) > 0
    scale = LOG2E / math.sqrt(HEAD_DIM)

    bias_own, bias_prev = _rel_bias_tables(rel_bias)
    h = x
    for l in range(depth):
        w = w_in[l]
        wq_t = (w[:, :ATTN_WIDTH] * scale).T.astype(BF16)
        wk = w[:, ATTN_WIDTH:2 * ATTN_WIDTH].astype(BF16)
        wv_t = w[:, 2 * ATTN_WIDTH:3 * ATTN_WIDTH].T.astype(BF16)
        qt, k, vt = _qkv_proj(h, wq_t, wk, wv_t)
        attn_t = _moba_attention(rel_bias, qt, k, vt, bias_own, bias_prev)
        mixer_consts = (w[:, 3 * ATTN_WIDTH:3 * ATTN_WIDTH + POOL_WIDTH].astype(BF16),
                        w[:, 3 * ATTN_WIDTH + POOL_WIDTH:].astype(BF16),
                        w_pool_group[l].astype(BF16),
                        pool_scale[l].reshape(1, POOL_WIDTH),
                        w_branch_attn[l].astype(BF16),
                        w_branch_pool[l].astype(BF16),
                        w_out[l].astype(BF16),
                        ln1_g[l].reshape(1, d), ln1_b[l].reshape(1, d))
        ffn_consts = (w_ffn_in[l].astype(BF16), conv_w[l], conv_b[l].reshape(1, d_ff),
                      w_ffn_out[l].astype(BF16), ln2_g[l].reshape(1, d), ln2_b[l].reshape(1, d))
        h = _mixer_ffn(alpha, h, attn_t, mixer_consts, ffn_consts)
    return h
```
